```python
import jax, jax.numpy as jnp
from jax import lax
import numpy as np

D_MODEL = 1024
BATCH = 16
SEQ = 4096
DEPTH = 4
DEC_BATCH = 32
DEC_SEQ = 2048
PAST_LEN = 128

N_MIXERS = 3
N_LAYERS_A = (DEPTH + 2) // 3
N_LAYERS_B = (DEPTH + 1) // 3
N_LAYERS_C = DEPTH // 3
NORM_EPS = 1e-6

MLA_HEADS = D_MODEL // 128
MLA_NOPE = 128
MLA_ROPE = 64
MLA_V = 128
MLA_Q_RANK = 3 * D_MODEL // 8
MLA_KV_RANK = D_MODEL // 4
MLA_IN = MLA_Q_RANK + MLA_KV_RANK + MLA_ROPE
ROPE_THETA = 10000.0
Q_BLOCK = 128

SSM_INNER = 2 * D_MODEL
SSM_HEAD_DIM = 64
SSM_HEADS = SSM_INNER // SSM_HEAD_DIM
SSM_GROUPS = 4
SSM_HPG = SSM_HEADS // SSM_GROUPS
SSM_STATE = 128
SSM_CONV = 5
SSM_CHUNK = 128
SSM_CONV_DIM = SSM_INNER + 2 * SSM_GROUPS * SSM_STATE
SSM_IN = SSM_INNER + SSM_CONV_DIM + 2 * SSM_HEADS

RWKV_HEAD_DIM = 64
RWKV_HEADS = D_MODEL // RWKV_HEAD_DIM
RWKV_DECAY_LORA = 64
RWKV_A_LORA = 64
RWKV_GATE_LORA = 160
RWKV_LN_EPS = 64e-5

D_FF = -(-8 * D_MODEL // (3 * 256)) * 256

kernel_name = 'hybrid_mla_ssd_rwkv7_encoder'


def rms_norm(x, g, eps=NORM_EPS):
    xf = x.astype(jnp.float32)
    y = xf * lax.rsqrt(jnp.mean(xf * xf, axis=-1, keepdims=True) + eps)
    return (y * g.astype(jnp.float32)).astype(x.dtype)


def rope_cos_sin(seq_len, dim):
    inv = 1.0 / (ROPE_THETA ** (jnp.arange(0, dim, 2, dtype=jnp.float32) / dim))
    ang = jnp.arange(seq_len, dtype=jnp.float32)[:, None] * inv[None, :]
    return jnp.cos(ang), jnp.sin(ang)


def apply_rope(x, cos, sin):
    xf = x.astype(jnp.float32)
    x1, x2 = jnp.split(xf, 2, axis=-1)
    return jnp.concatenate([x1 * cos - x2 * sin, x2 * cos + x1 * sin], axis=-1).astype(x.dtype)


def mla_mixer(x, w_in, q_norm, w_qb, kv_norm, w_kvb, w_o):
    bsz, length, _ = x.shape
    h = x @ w_in
    q_lat, kv_lat, k_rope = jnp.split(h, [MLA_Q_RANK, MLA_Q_RANK + MLA_KV_RANK], axis=-1)
    q = (rms_norm(q_lat, q_norm) @ w_qb).reshape(bsz, length, MLA_HEADS, MLA_NOPE + MLA_ROPE)
    kv = (rms_norm(kv_lat, kv_norm) @ w_kvb).reshape(bsz, length, MLA_HEADS, MLA_NOPE + MLA_V)
    q_nope, q_rope = q[..., :MLA_NOPE], q[..., MLA_NOPE:]
    k_nope, v = kv[..., :MLA_NOPE], kv[..., MLA_NOPE:]
    cos, sin = rope_cos_sin(length, MLA_ROPE)
    q_rope = apply_rope(q_rope, cos[:, None, :], sin[:, None, :])
    k_rope = apply_rope(k_rope, cos, sin)
    scale = (MLA_NOPE + MLA_ROPE) ** -0.5
    nb = length // Q_BLOCK
    qn_blocks = jnp.moveaxis(q_nope.reshape(bsz, nb, Q_BLOCK, MLA_HEADS, MLA_NOPE), 1, 0)
    qr_blocks = jnp.moveaxis(q_rope.reshape(bsz, nb, Q_BLOCK, MLA_HEADS, MLA_ROPE), 1, 0)

    def attend(blk):
        qn, qr = blk
        s = jnp.einsum('bqhd,bkhd->bhqk', qn, k_nope) + jnp.einsum('bqhr,bkr->bhqk', qr, k_rope)
        p = jax.nn.softmax(s.astype(jnp.float32) * scale, axis=-1).astype(v.dtype)
        return jnp.einsum('bhqk,bkhd->bqhd', p, v)

    o = lax.map(attend, (qn_blocks, qr_blocks))
    o = jnp.moveaxis(o, 0, 1).reshape(bsz, length, MLA_HEADS * MLA_V)
    return o @ w_o


def centred_depthwise_conv(x, w, b):
    width, ch = w.shape
    pad = width // 2
    y = lax.conv_general_dilated(x, w[:, None, :].astype(x.dtype), window_strides=(1,),
                                 padding=[(pad, pad)], dimension_numbers=('NWC', 'WIO', 'NWC'),
                                 feature_group_count=ch)
    return y + b


def ssd_scan(xdt, a_dt, b_in, c_in):
    bsz, length, g, hh, p = xdt.shape
    n = b_in.shape[-1]
    nc, q = length // SSM_CHUNK, SSM_CHUNK
    x = xdt.reshape(bsz, nc, q, g, hh, p)
    a = a_dt.reshape(bsz, nc, q, g, hh)
    bm = b_in.reshape(bsz, nc, q, g, n)
    cm = c_in.reshape(bsz, nc, q, g, n)
    a_cum = jnp.cumsum(a, axis=2)
    causal = jnp.tril(jnp.ones((q, q), dtype=bool))[:, :, None, None]
    seg = a_cum[:, :, :, None] - a_cum[:, :, None, :]
    decay_in = jnp.exp(jnp.where(causal, seg, -jnp.inf))
    cb = jnp.einsum('bclgn,bcsgn->bclsg', cm, bm)
    y_diag = jnp.einsum('bclsg,bclsgh,bcsghp->bclghp', cb, decay_in, x)
    decay_to_end = jnp.exp(a_cum[:, :, -1:] - a_cum)
    chunk_states = jnp.einsum('bclgn,bclgh,bclghp->bcghpn', bm, decay_to_end, x)
    chunk_decay = jnp.exp(a_cum[:, :, -1])

    def step(state, inp):
        st, dec = inp
        return state * dec[..., None, None] + st, state

    init = jnp.zeros((bsz, g, hh, p, n), jnp.float32)
    _, prev = lax.scan(step, init, (jnp.moveaxis(chunk_states, 1, 0), jnp.moveaxis(chunk_decay, 1, 0)))
    prev = jnp.moveaxis(prev, 0, 1)
    y_off = jnp.einsum('bclgn,bcghpn,bclgh->bclghp', cm, prev, jnp.exp(a_cum))
    return (y_diag + y_off).reshape(bsz, length, g, hh, p)


def mamba2_mixer(x, w_in, conv_w, conv_b, a_log, dt_bias, d_skip, gate_norm, w_out):
    bsz, length, _ = x.shape
    f32 = jnp.float32
    gn = SSM_GROUPS * SSM_STATE
    z, xbc, dt = jnp.split(x @ w_in, [SSM_INNER, SSM_INNER + SSM_CONV_DIM], axis=-1)
    xbc = jax.nn.silu(centred_depthwise_conv(xbc, conv_w, conv_b))
    xs, b_in, c_in = jnp.split(xbc.astype(f32), [SSM_INNER, SSM_INNER + gn], axis=-1)
    xs = xs.reshape(bsz, length, SSM_GROUPS, SSM_HPG, SSM_HEAD_DIM)
    b_in = b_in.reshape(bsz, length, SSM_GROUPS, SSM_STATE)
    c_in = c_in.reshape(bsz, length, SSM_GROUPS, SSM_STATE)
    dt = jax.nn.softplus(dt.astype(f32).reshape(bsz, length, 2, SSM_GROUPS, SSM_HPG)
                         + dt_bias.astype(f32).reshape(2, SSM_GROUPS, SSM_HPG))
    a_dt = dt * -jnp.exp(a_log.astype(f32).reshape(2, SSM_GROUPS, SSM_HPG))
    xdt = xs[:, :, None] * dt[..., None]
    flip = lambda t: jnp.flip(t, axis=1)
    y_f = ssd_scan(xdt[:, :, 0], a_dt[:, :, 0], b_in, c_in)
    y_b = flip(ssd_scan(flip(xdt[:, :, 1]), flip(a_dt[:, :, 1]), flip(b_in), flip(c_in)))
    y = y_f + y_b + xs * d_skip.astype(f32).reshape(SSM_GROUPS, SSM_HPG, 1)
    grp = (bsz, length, SSM_GROUPS, SSM_INNER // SSM_GROUPS)
    y = y.reshape(grp).astype(x.dtype) * jax.nn.silu(z).reshape(grp)
    y = rms_norm(y, gate_norm.reshape(SSM_GROUPS, SSM_INNER // SSM_GROUPS))
    return y.reshape(bsz, length, SSM_INNER) @ w_out


def rwkv7_mixer(x, mu, w_rkv, w0, w1, w2, a0, a1, a2, g1, g2, k_k, k_a, r_k, ln_w, ln_b, w_o):
    bsz, length, d = x.shape
    f32 = jnp.float32
    hshape = (bsz, length, RWKV_HEADS, RWKV_HEAD_DIM)
    x_prev = jnp.pad(x[:, :-1], ((0, 0), (1, 0), (0, 0)))
    x_next = jnp.pad(x[:, 1:], ((0, 0), (0, 1), (0, 0)))
    xx = 0.5 * (x_prev + x_next) - x
    xr, xw, xk, xv, xa, xg = (x + xx * mu[i] for i in range(6))
    r = xr @ w_rkv[0]
    k = xk @ w_rkv[1]
    v = xv @ w_rkv[2]

    def decay(dr):
        w_log = -jax.nn.softplus(-(w0[dr] + jnp.tanh(xw @ w1[dr]) @ w2[dr]).astype(f32)) - 0.5
        return jnp.exp(-jnp.exp(w_log)).reshape(hshape)

    a = jax.nn.sigmoid((a0 + (xa @ a1) @ a2).astype(f32))
    g = jax.nn.sigmoid(xg @ g1) @ g2
    kk = (k * k_k).astype(f32).reshape(hshape)
    kk = kk / jnp.maximum(jnp.sqrt(jnp.sum(kk * kk, axis=-1, keepdims=True)), 1e-12)
    k = k.astype(f32) * (1.0 + (a - 1.0) * k_a.astype(f32))
    r_h, k_h, v_h, a_h = (t.astype(f32).reshape(hshape) for t in (r, k, v, a))
    tm = lambda t: jnp.moveaxis(t, 1, 0)
    shared = (tm(r_h), tm(k_h), tm(v_h), tm(-kk), tm(kk * a_h))

    def step(state, inp):
        dec, r_t, k_t, v_t, rm_t, add_t = inp
        sa = jnp.einsum('bhij,bhj->bhi', state, rm_t)
        state = (state * dec[:, :, None, :] + sa[..., None] * add_t[:, :, None, :]
                 + v_t[..., None] * k_t[:, :, None, :])
        return state, jnp.einsum('bhij,bhj->bhi', state, r_t)

    init = jnp.zeros((bsz, RWKV_HEADS, RWKV_HEAD_DIM, RWKV_HEAD_DIM), f32)
    _, y_f = lax.scan(step, init, (tm(decay(0)),) + shared)
    _, y_b = lax.scan(step, init, (tm(decay(1)),) + shared, reverse=True)
    y = jnp.moveaxis(y_f + y_b, 0, 1)
    mean = jnp.mean(y, axis=-1, keepdims=True)
    var = jnp.mean(jnp.square(y - mean), axis=-1, keepdims=True)
    hk = (RWKV_HEADS, RWKV_HEAD_DIM)
    y = (y - mean) * lax.rsqrt(var + RWKV_LN_EPS) * ln_w.astype(f32).reshape(hk) + ln_b.astype(f32).reshape(hk)
    y = y + jnp.sum(r_h * k_h * r_k.astype(f32), axis=-1, keepdims=True) * v_h
    y = y.reshape(bsz, length, d).astype(x.dtype)
    return (y * g) @ w_o


def swiglu_ffn(x, w_in, w_out):
    gate, up = jnp.split(x @ w_in, 2, axis=-1)
    return (jax.nn.silu(gate) * up) @ w_out


def setup_inputs(seed: int = 0) -> dict:
    key = jax.random.key(seed)
    ks = iter(jax.random.split(key, 48))
    f32 = jnp.float32

    def nrm(shape, scale):
        return jax.random.normal(next(ks), shape, f32) * scale

    def gain(shape):
        return 1.0 + 0.05 * jax.random.normal(next(ks), shape, f32)

    def unif(shape, lo, hi):
        return jax.random.uniform(next(ks), shape, f32, lo, hi)

    nA, nB, nC, D = N_LAYERS_A, N_LAYERS_B, N_LAYERS_C, D_MODEL
    dt0 = jnp.exp(unif((nB, 2, SSM_HEADS), float(np.log(1e-3)), float(np.log(1e-1))))
    return {
        'x_prompt': nrm((BATCH, SEQ, D), 1.0),
        'x_sample': nrm((DEC_BATCH, DEC_SEQ, D), 1.0),
        'mix_pre_norm': gain((DEPTH, D)),
        'mix_post_norm': gain((DEPTH, D)),
        'ffn_pre_norm': gain((DEPTH, D)),
        'ffn_post_norm': gain((DEPTH, D)),
        'mla_w_in': nrm((nA, D, MLA_IN), D ** -0.5),
        'mla_q_norm': gain((nA, MLA_Q_RANK)),
        'mla_w_qb': nrm((nA, MLA_Q_RANK, MLA_HEADS * (MLA_NOPE + MLA_ROPE)), MLA_Q_RANK ** -0.5),
        'mla_kv_norm': gain((nA, MLA_KV_RANK)),
        'mla_w_kvb': nrm((nA, MLA_KV_RANK, MLA_HEADS * (MLA_NOPE + MLA_V)), MLA_KV_RANK ** -0.5),
        'mla_w_o': nrm((nA, MLA_HEADS * MLA_V, D), (MLA_HEADS * MLA_V) ** -0.5),
        'ssm_w_in': nrm((nB, D, SSM_IN), D ** -0.5),
        'ssm_conv_w': nrm((nB, SSM_CONV, SSM_CONV_DIM), SSM_CONV ** -0.5),
        'ssm_conv_b': nrm((nB, SSM_CONV_DIM), 0.02),
        'ssm_a_log': jnp.log(unif((nB, 2, SSM_HEADS), 1.0, 16.0)),
        'ssm_dt_bias': jnp.log(jnp.expm1(dt0)),
        'ssm_d_skip': gain((nB, SSM_HEADS)),
        'ssm_gate_norm': gain((nB, SSM_INNER)),
        'ssm_w_out': nrm((nB, SSM_INNER, D), SSM_INNER ** -0.5),
        'rwkv_mu': unif((nC, 6, D), 0.0, 1.0),
        'rwkv_w_rkv': nrm((nC, 3, D, D), D ** -0.5),
        'rwkv_w0': unif((nC, 2, D), -6.0, -1.0),
        'rwkv_w1': nrm((nC, 2, D, RWKV_DECAY_LORA), D ** -0.5),
        'rwkv_w2': nrm((nC, 2, RWKV_DECAY_LORA, D), 0.1 * RWKV_DECAY_LORA ** -0.5),
        'rwkv_a0': nrm((nC, D), 0.1),
        'rwkv_a1': nrm((nC, D, RWKV_A_LORA), D ** -0.5),
        'rwkv_a2': nrm((nC, RWKV_A_LORA, D), 0.1 * RWKV_A_LORA ** -0.5),
        'rwkv_g1': nrm((nC, D, RWKV_GATE_LORA), D ** -0.5),
        'rwkv_g2': nrm((nC, RWKV_GATE_LORA, D), RWKV_GATE_LORA ** -0.5),
        'rwkv_k_k': 0.85 + nrm((nC, D), 0.05),
        'rwkv_k_a': gain((nC, D)),
        'rwkv_r_k': nrm((nC, RWKV_HEADS, RWKV_HEAD_DIM), 0.1),
        'rwkv_ln_w': gain((nC, D)),
        'rwkv_ln_b': nrm((nC, D), 0.02),
        'rwkv_w_o': nrm((nC, D, D), D ** -0.5),
        'ffn_w_in': nrm((DEPTH, D, 2 * D_FF), D ** -0.5),
        'ffn_w_out': nrm((DEPTH, D_FF, D), D_FF ** -0.5),
    }


def reference(x_prompt, x_sample, mix_pre_norm, mix_post_norm, ffn_pre_norm, ffn_post_norm,
              mla_w_in, mla_q_norm, mla_w_qb, mla_kv_norm, mla_w_kvb, mla_w_o,
              ssm_w_in, ssm_conv_w, ssm_conv_b, ssm_a_log, ssm_dt_bias, ssm_d_skip, ssm_gate_norm, ssm_w_out,
              rwkv_mu, rwkv_w_rkv, rwkv_w0, rwkv_w1, rwkv_w2, rwkv_a0, rwkv_a1, rwkv_a2,
              rwkv_g1, rwkv_g2, rwkv_k_k, rwkv_k_a, rwkv_r_k, rwkv_ln_w, rwkv_ln_b, rwkv_w_o,
              ffn_w_in, ffn_w_out):
    def trunk(h):
        for i in range(DEPTH):
            kind, j = i % N_MIXERS, i // N_MIXERS
            u = rms_norm(h, mix_pre_norm[i])
            if kind == 0:
                m = mla_mixer(u, mla_w_in[j], mla_q_norm[j], mla_w_qb[j], mla_kv_norm[j], mla_w_kvb[j], mla_w_o[j])
            elif kind == 1:
                m = mamba2_mixer(u, ssm_w_in[j], ssm_conv_w[j], ssm_conv_b[j], ssm_a_log[j], ssm_dt_bias[j],
                                 ssm_d_skip[j], ssm_gate_norm[j], ssm_w_out[j])
            else:
                m = rwkv7_mixer(u, rwkv_mu[j], rwkv_w_rkv[j], rwkv_w0[j], rwkv_w1[j], rwkv_w2[j],
                                rwkv_a0[j], rwkv_a1[j], rwkv_a2[j], rwkv_g1[j], rwkv_g2[j],
                                rwkv_k_k[j], rwkv_k_a[j], rwkv_r_k[j], rwkv_ln_w[j], rwkv_ln_b[j], rwkv_w_o[j])
            h = h + rms_norm(m, mix_post_norm[i])
            u = rms_norm(h, ffn_pre_norm[i])
            h = h + rms_norm(swiglu_ffn(u, ffn_w_in[i], ffn_w_out[i]), ffn_post_norm[i])
        return h

    y_prompt = trunk(x_prompt)
    y_sample = trunk(x_sample)
    return (y_prompt, y_sample)
```

```python
import functools

import jax
import jax.numpy as jnp
import numpy as np
from jax import lax
from jax.experimental import pallas as pl
from jax.experimental.pallas import tpu as pltpu

F32 = jnp.float32
BF16 = jnp.bfloat16

NORM_EPS = 1e-6
N_MIXERS = 3

VMEM_LIMIT_BYTES = 56 * 1024 * 1024
LANES = 128
SUBLANES = 8

MLA_HEADS = 8
MLA_NOPE = 128
MLA_ROPE = 64
MLA_V = 128
MLA_QK = MLA_NOPE + 2 * MLA_ROPE
ROPE_THETA = 10000.0

SSM_HEAD_DIM = 64
SSM_GROUPS = 4
SSM_STATE = 128
SSM_CONV = 5
SSM_CHUNK = 128

RWKV_HEAD_DIM = 64
RWKV_LN_EPS = 64e-5
RWKV_CHUNK = 64


def _rms(x, g, eps=NORM_EPS):
    return x * lax.rsqrt(jnp.mean(x * x, axis=-1, keepdims=True) + eps) * g


def _dot(a, b):
    return jnp.dot(a, b, preferred_element_type=F32)


def _dot_nt(a, b):
    return lax.dot_general(a, b, (((1,), (1,)), ((), ())), preferred_element_type=F32)


def _dot_tn(a, b):
    return lax.dot_general(a, b, (((0,), (0,)), ((), ())), preferred_element_type=F32)


def _dot_f32(a, b):
    return jnp.dot(a, b, preferred_element_type=F32, precision=lax.Precision.HIGHEST)


def _silu(x):
    return x * jax.nn.sigmoid(x)


def _softplus(x):
    return jnp.maximum(x, 0.0) + jnp.log(1.0 + jnp.exp(-jnp.abs(x)))


def _const_spec(shape):
    nd = len(shape)
    return pl.BlockSpec(shape, lambda *_: (0,) * nd, pipeline_mode=pl.Buffered(1))


def _params(semantics):
    return pltpu.CompilerParams(dimension_semantics=semantics, vmem_limit_bytes=VMEM_LIMIT_BYTES)


def _row(v):
    return v.reshape(1, -1).astype(F32)


def _tail_body(h_ref, y_ref, wo_ref, gpost_ref, gpre_ref, win_ref, wout_ref, gfpost_ref, o_ref, *,
               ff_chunk):
    d_ff = wout_ref.shape[0]
    h1 = h_ref[...] + _rms(_dot(y_ref[...], wo_ref[...]), gpost_ref[...])
    u = _rms(h1, gpre_ref[...]).astype(BF16)
    acc = jnp.zeros(h1.shape, F32)
    for c in range(d_ff // ff_chunk):
        lo = c * ff_chunk
        gate = _dot(u, win_ref[:, lo:lo + ff_chunk])
        up = _dot(u, win_ref[:, d_ff + lo:d_ff + lo + ff_chunk])
        act = (_silu(gate) * up).astype(BF16)
        acc = acc + _dot(act, wout_ref[lo:lo + ff_chunk, :])
    o_ref[...] = h1 + _rms(acc, gfpost_ref[...])


def _layer_tail(h, y, w_o, g_post, g_pre, w_in, w_out, g_fpost, *, tm=512, ff_chunk=256):
    t, d = h.shape
    ky = y.shape[1]
    d_ff = w_out.shape[0]
    assert t % tm == 0 and d_ff % ff_chunk == 0 and ff_chunk % LANES == 0
    row = lambda c: pl.BlockSpec((tm, c), lambda i: (i, 0))
    return pl.pallas_call(
        functools.partial(_tail_body, ff_chunk=ff_chunk),
        grid=(t // tm,),
        in_specs=[row(d), row(ky), _const_spec((ky, d)), _const_spec((1, d)), _const_spec((1, d)),
                  _const_spec((d, 2 * d_ff)), _const_spec((d_ff, d)), _const_spec((1, d))],
        out_specs=row(d),
        out_shape=jax.ShapeDtypeStruct((t, d), F32),
        compiler_params=_params(("parallel",)),
        name="layer_tail",
    )(h, y, w_o.astype(BF16), _row(g_post), _row(g_pre), w_in.astype(BF16), w_out.astype(BF16),
      _row(g_fpost))


def _mla_in_body(h_ref, fq_ref, fk_ref, gpre_ref, win_ref, qn_ref, wq_ref, kvn_ref, wkv_ref,
                 q_ref, k_ref, v_ref, *, q_rank, kv_rank):
    u = _rms(h_ref[0], gpre_ref[...]).astype(BF16)
    lat = _dot(u, win_ref[...])
    q_lat = _rms(lat[:, :q_rank], qn_ref[...]).astype(BF16)
    kv_lat = _rms(lat[:, q_rank:q_rank + kv_rank], kvn_ref[...]).astype(BF16)
    kr = lat[:, q_rank + kv_rank:q_rank + kv_rank + LANES]
    kr_rot = lat[:, q_rank + kv_rank + LANES:q_rank + kv_rank + 2 * LANES]
    fk = fk_ref[...]
    k_rope = (kr * fk[:, :LANES] + kr_rot * fk[:, LANES:]).astype(BF16)
    fq = fq_ref[...]
    q = _dot(q_lat, wq_ref[...])
    kv = _dot(kv_lat, wkv_ref[...])
    for hd in range(MLA_HEADS):
        q_ref[0, :, hd * MLA_QK:(hd + 1) * MLA_QK] = (
            q[:, hd * MLA_QK:(hd + 1) * MLA_QK] * fq).astype(BF16)
        k_ref[0, :, hd * MLA_QK:hd * MLA_QK + MLA_NOPE] = (
            kv[:, hd * MLA_NOPE:(hd + 1) * MLA_NOPE].astype(BF16))
        k_ref[0, :, hd * MLA_QK + MLA_NOPE:(hd + 1) * MLA_QK] = k_rope
    v_ref[0] = kv[:, MLA_HEADS * MLA_NOPE:].astype(BF16)


def _mla_attn_body(q_ref, k_ref, v_ref, o_ref):
    s = _dot_nt(q_ref[0], k_ref[0])
    m = jnp.max(s, axis=-1, keepdims=True)
    p = jnp.exp(s - m)
    l = jnp.sum(p, axis=-1, keepdims=True)
    o = _dot(p.astype(BF16), v_ref[0])
    o_ref[0] = (o / l).astype(o_ref.dtype)


def _rot_cols(w):
    half = w.shape[-1] // 2
    return jnp.concatenate([-w[..., half:], w[..., :half]], axis=-1)


def _mla_mixer(h3, g_pre, w_in, q_norm, w_qb, kv_norm, w_kvb, *, tm=512, tq=256):
    bsz, length, d = h3.shape
    q_rank, kv_rank = q_norm.shape[0], kv_norm.shape[0]
    hd_q = MLA_NOPE + MLA_ROPE
    inv = 1.0 / (ROPE_THETA ** (jnp.arange(0, MLA_ROPE, 2, dtype=F32) / MLA_ROPE))
    ang = jnp.arange(length, dtype=F32)[:, None] * inv[None, :]
    cc = jnp.concatenate([jnp.cos(ang), jnp.cos(ang)], axis=-1)
    ss = jnp.concatenate([jnp.sin(ang), jnp.sin(ang)], axis=-1)
    scale = hd_q ** -0.5
    fq = jnp.concatenate([jnp.full((length, MLA_NOPE), scale, F32), cc * scale, ss * scale], axis=-1)
    fk = jnp.concatenate([cc, cc, ss, ss], axis=-1)
    w_kr = w_in[:, q_rank + kv_rank:]
    w_in_ext = jnp.concatenate([w_in[:, :q_rank + kv_rank], w_kr, w_kr, _rot_cols(w_kr), _rot_cols(w_kr)],
                               axis=-1).astype(BF16)
    wq = w_qb.reshape(q_rank, MLA_HEADS, hd_q)
    wq_ext = jnp.concatenate([wq, _rot_cols(wq[..., MLA_NOPE:])], axis=-1).reshape(
        q_rank, MLA_HEADS * MLA_QK).astype(BF16)
    wkv = w_kvb.reshape(kv_rank, MLA_HEADS, MLA_NOPE + MLA_V)
    wkv_perm = jnp.concatenate([wkv[..., :MLA_NOPE].reshape(kv_rank, -1),
                                wkv[..., MLA_NOPE:].reshape(kv_rank, -1)], axis=-1).astype(BF16)
    n_in = w_in_ext.shape[1]
    assert length % tm == 0 and length % tq == 0
    q, k, v = pl.pallas_call(
        functools.partial(_mla_in_body, q_rank=q_rank, kv_rank=kv_rank),
        grid=(bsz, length // tm),
        in_specs=[pl.BlockSpec((1, tm, d), lambda b, i: (b, i, 0)),
                  pl.BlockSpec((tm, MLA_QK), lambda b, i: (i, 0)),
                  pl.BlockSpec((tm, MLA_QK), lambda b, i: (i, 0)),
                  _const_spec((1, d)), _const_spec((d, n_in)), _const_spec((1, q_rank)),
                  _const_spec((q_rank, MLA_HEADS * MLA_QK)), _const_spec((1, kv_rank)),
                  _const_spec((kv_rank, MLA_HEADS * (MLA_NOPE + MLA_V)))],
        out_specs=[pl.BlockSpec((1, tm, MLA_HEADS * MLA_QK), lambda b, i: (b, i, 0)),
                   pl.BlockSpec((1, tm, MLA_HEADS * MLA_QK), lambda b, i: (b, i, 0)),
                   pl.BlockSpec((1, tm, MLA_HEADS * MLA_V), lambda b, i: (b, i, 0))],
        out_shape=[jax.ShapeDtypeStruct((bsz, length, MLA_HEADS * MLA_QK), BF16),
                   jax.ShapeDtypeStruct((bsz, length, MLA_HEADS * MLA_QK), BF16),
                   jax.ShapeDtypeStruct((bsz, length, MLA_HEADS * MLA_V), BF16)],
        compiler_params=_params(("parallel", "parallel")),
        name="mla_in",
    )(h3, fq, fk, _row(g_pre), w_in_ext, _row(q_norm), wq_ext, _row(kv_norm), wkv_perm)
    return pl.pallas_call(
        _mla_attn_body,
        grid=(bsz, MLA_HEADS, length // tq),
        in_specs=[pl.BlockSpec((1, tq, MLA_QK), lambda b, hd, i: (b, i, hd)),
                  pl.BlockSpec((1, length, MLA_QK), lambda b, hd, i: (b, 0, hd)),
                  pl.BlockSpec((1, length, MLA_V), lambda b, hd, i: (b, 0, hd))],
        out_specs=pl.BlockSpec((1, tq, MLA_V), lambda b, hd, i: (b, i, hd)),
        out_shape=jax.ShapeDtypeStruct((bsz, length, MLA_HEADS * MLA_V), BF16),
        compiler_params=_params(("parallel", "parallel", "parallel")),
        name="mla_attn",
    )(q, k, v)


def _ssd_in_body(h_ref, gpre_ref, wz_ref, wx_ref, wdt_ref, z_ref, xbc_ref, dt_ref):
    u = _rms(h_ref[...], gpre_ref[...]).astype(BF16)
    z_ref[...] = _dot(u, wz_ref[...])
    xbc_ref[...] = _dot(u, wx_ref[...])
    dt_ref[...] = _dot(u, wdt_ref[...])


def _ssd_conv_body(x_ref, w_ref, b_ref, o_ref):
    x = x_ref[0]
    length = x.shape[0]
    pad = SSM_CONV // 2
    t = lax.broadcasted_iota(jnp.int32, x.shape, 0)
    acc = x * w_ref[pad:pad + 1, :] + b_ref[...]
    for kk in range(SSM_CONV):
        off = kk - pad
        if off == 0:
            continue
        shifted = pltpu.roll(x, (-off) % length, axis=0)
        valid = (t >= -off) if off < 0 else (t < length - off)
        acc = acc + jnp.where(valid, shifted, 0.0) * w_ref[kk:kk + 1, :]
    o_ref[0] = _silu(acc)


def _ssd_direction(xbc, dt_raw, bias, a_neg, expand, state_ref, reverse, d_inner):
    q = xbc.shape[0]
    gn = SSM_GROUPS * SSM_STATE
    hpg = d_inner // SSM_HEAD_DIM // SSM_GROUPS
    xs = xbc[:, :d_inner]
    b_in = xbc[:, d_inner:d_inner + gn]
    c_in = xbc[:, d_inner + gn:]
    dt = _softplus(dt_raw + bias)
    a_dt = dt * a_neg
    ti = lax.broadcasted_iota(jnp.int32, (q, q), 0)
    si = lax.broadcasted_iota(jnp.int32, (q, q), 1)
    keep = (si >= ti) if reverse else (si <= ti)
    tri = jnp.where(keep, 1.0, 0.0).astype(F32)
    cum = _dot_f32(tri, a_dt)
    cum_t = jnp.transpose(cum)
    edge = cum[0:1, :] if reverse else cum[q - 1:q, :]
    x_dt = xs * _dot_f32(dt, expand)
    x_end = (xs * _dot_f32(dt * jnp.exp(edge - cum), expand)).astype(BF16)
    scale_in = _dot_f32(jnp.exp(cum), expand)
    decay_state = _dot_f32(jnp.exp(edge), expand)
    x_dt_b = x_dt.astype(BF16)
    outs = []
    for g in range(SSM_GROUPS):
        bg = b_in[:, g * SSM_STATE:(g + 1) * SSM_STATE].astype(BF16)
        cg = c_in[:, g * SSM_STATE:(g + 1) * SSM_STATE].astype(BF16)
        cb = _dot_nt(cg, bg)
        cols = slice(g * hpg * SSM_HEAD_DIM, (g + 1) * hpg * SSM_HEAD_DIM)
        st = state_ref[:, cols]
        y_off = _dot(cg, st.astype(BF16)) * scale_in[:, cols]
        for j in range(hpg):
            hd = g * hpg + j
            seg = cum[:, hd:hd + 1] - cum_t[hd:hd + 1, :]
            m = (cb * jnp.exp(jnp.where(keep, seg, -jnp.inf))).astype(BF16)
            hc = slice(hd * SSM_HEAD_DIM, (hd + 1) * SSM_HEAD_DIM)
            y_diag = _dot(m, x_dt_b[:, hc])
            outs.append(y_diag + y_off[:, j * SSM_HEAD_DIM:(j + 1) * SSM_HEAD_DIM])
        state_ref[:, cols] = st * decay_state[:, cols] + _dot_tn(bg, x_end[:, cols])
    return jnp.concatenate(outs, axis=-1)


def _ssd_scan_body(xf_ref, dtf_ref, xb_ref, dtb_ref, bias_ref, alog_ref, exp_ref,
                   yf_ref, yb_ref, sf_ref, sb_ref, *, d_inner):
    @pl.when(pl.program_id(1) == 0)
    def _():
        sf_ref[...] = jnp.zeros(sf_ref.shape, F32)
        sb_ref[...] = jnp.zeros(sb_ref.shape, F32)

    nh = d_inner // SSM_HEAD_DIM
    a_neg = -jnp.exp(alog_ref[...])
    expand = exp_ref[...]
    yf_ref[0] = _ssd_direction(xf_ref[0], dtf_ref[0][:, :nh], bias_ref[0:1, :], a_neg[0:1, :], expand,
                               sf_ref, False, d_inner)
    yb_ref[0] = _ssd_direction(xb_ref[0], dtb_ref[0][:, nh:], bias_ref[1:2, :], a_neg[1:2, :], expand,
                               sb_ref, True, d_inner)


def _ssd_out_body(yf_ref, yb_ref, xbc_ref, z_ref, dsk_ref, gn_ref, o_ref, *, d_inner):
    xs = xbc_ref[:, :d_inner]
    y = (yf_ref[...] + yb_ref[...] + xs * dsk_ref[...]) * _silu(z_ref[...])
    gw = d_inner // SSM_GROUPS
    for g in range(SSM_GROUPS):
        cols = slice(g * gw, (g + 1) * gw)
        o_ref[:, cols] = _rms(y[:, cols], gn_ref[:, cols]).astype(o_ref.dtype)


def _ssd_mixer(h3, g_pre, w_in, conv_w, conv_b, a_log, dt_bias, d_skip, gate_norm, *, tm=512, cb=256):
    bsz, length, d = h3.shape
    t = bsz * length
    nh = a_log.shape[-1]
    d_inner = nh * SSM_HEAD_DIM
    conv_dim = d_inner + 2 * SSM_GROUPS * SSM_STATE
    q = SSM_CHUNK
    nc = length // q
    assert t % tm == 0 and length % q == 0 and conv_dim % cb == 0
    row = lambda c: pl.BlockSpec((tm, c), lambda i: (i, 0))
    z, xbc, dt = pl.pallas_call(
        _ssd_in_body,
        grid=(t // tm,),
        in_specs=[row(d), _const_spec((1, d)), _const_spec((d, d_inner)), _const_spec((d, conv_dim)),
                  _const_spec((d, 2 * nh))],
        out_specs=[row(d_inner), row(conv_dim), row(2 * nh)],
        out_shape=[jax.ShapeDtypeStruct((t, d_inner), F32), jax.ShapeDtypeStruct((t, conv_dim), F32),
                   jax.ShapeDtypeStruct((t, 2 * nh), F32)],
        compiler_params=_params(("parallel",)),
        name="ssd_in",
    )(h3.reshape(t, d), _row(g_pre), w_in[:, :d_inner].astype(BF16),
      w_in[:, d_inner:d_inner + conv_dim].astype(BF16), w_in[:, d_inner + conv_dim:].astype(BF16))
    xbc = pl.pallas_call(
        _ssd_conv_body,
        grid=(bsz, conv_dim // cb),
        in_specs=[pl.BlockSpec((1, length, cb), lambda b, j: (b, 0, j)),
                  pl.BlockSpec((SSM_CONV, cb), lambda b, j: (0, j)),
                  pl.BlockSpec((1, cb), lambda b, j: (0, j))],
        out_specs=pl.BlockSpec((1, length, cb), lambda b, j: (b, 0, j)),
        out_shape=jax.ShapeDtypeStruct((bsz, length, conv_dim), F32),
        compiler_params=_params(("parallel", "parallel")),
        name="ssd_conv",
    )(xbc.reshape(bsz, length, conv_dim), conv_w.astype(F32), _row(conv_b))
    dt3 = dt.reshape(bsz, length, 2 * nh)
    expand = jnp.repeat(jnp.eye(nh, dtype=F32), SSM_HEAD_DIM, axis=1)
    fwd = lambda c: pl.BlockSpec((1, q, c), lambda b, i: (b, i, 0))
    bwd = lambda c: pl.BlockSpec((1, q, c), lambda b, i: (b, nc - 1 - i, 0))
    y_f, y_b = pl.pallas_call(
        functools.partial(_ssd_scan_body, d_inner=d_inner),
        grid=(bsz, nc),
        in_specs=[fwd(conv_dim), fwd(2 * nh), bwd(conv_dim), bwd(2 * nh),
                  _const_spec((2, nh)), _const_spec((2, nh)), _const_spec((nh, d_inner))],
        out_specs=[fwd(d_inner), bwd(d_inner)],
        out_shape=[jax.ShapeDtypeStruct((bsz, length, d_inner), F32)] * 2,
        scratch_shapes=[pltpu.VMEM((SSM_STATE, d_inner), F32)] * 2,
        compiler_params=_params(("parallel", "arbitrary")),
        name="ssd_scan",
    )(xbc, dt3, xbc, dt3, dt_bias.astype(F32), a_log.astype(F32), expand)
    return pl.pallas_call(
        functools.partial(_ssd_out_body, d_inner=d_inner),
        grid=(t // tm,),
        in_specs=[row(d_inner), row(d_inner), row(conv_dim), row(d_inner),
                  _const_spec((1, d_inner)), _const_spec((1, d_inner))],
        out_specs=row(d_inner),
        out_shape=jax.ShapeDtypeStruct((t, d_inner), BF16),
        compiler_params=_params(("parallel",)),
        name="ssd_out",
    )(y_f.reshape(t, d_inner), y_b.reshape(t, d_inner), xbc.reshape(t, conv_dim), z,
      _row(jnp.repeat(d_skip.astype(F32), SSM_HEAD_DIM)), _row(gate_norm))


def _rwkv_in_body(h_ref, hp_ref, hn_ref, gpre_ref, mu_ref, wr_ref, wk_ref, wv_ref, w0_ref, w1_ref,
                  w2f_ref, w2b_ref, a0_ref, a1_ref, a2_ref, g1_ref, g2_ref, kk_ref, ka_ref,
                  seg_ref, segt_ref,
                  r_out, k_out, v_out, lwf_out, lwb_out, rm_out, add_out, g_out):
    i = pl.program_id(1)
    n_i = pl.num_programs(1)
    gpre = gpre_ref[...]
    x = _rms(h_ref[0], gpre)
    tm = x.shape[0]
    prev_row = _rms(hp_ref[0][SUBLANES - 1:SUBLANES, :], gpre) * jnp.where(i > 0, 1.0, 0.0)
    next_row = _rms(hn_ref[0][0:1, :], gpre) * jnp.where(i < n_i - 1, 1.0, 0.0)
    t = lax.broadcasted_iota(jnp.int32, x.shape, 0)
    x_prev = jnp.where(t == 0, prev_row, pltpu.roll(x, 1, axis=0))
    x_next = jnp.where(t == tm - 1, next_row, pltpu.roll(x, tm - 1, axis=0))
    xx = 0.5 * (x_prev + x_next) - x
    mix = lambda j: (x + xx * mu_ref[j:j + 1, :]).astype(BF16)
    xr, xw, xk, xv, xa, xg = (mix(j) for j in range(6))
    r = _dot(xr, wr_ref[...])
    k = _dot(xk, wk_ref[...])
    v = _dot(xv, wv_ref[...])
    lora_w = jnp.tanh(_dot(xw, w1_ref[...])).astype(BF16)
    log_decay = lambda w: -np.float32(np.exp(-0.5)) * jax.nn.sigmoid(w)
    lwf_out[0] = log_decay(w0_ref[0:1, :] + _dot(lora_w, w2f_ref[...]))
    lwb_out[0] = log_decay(w0_ref[1:2, :] + _dot(lora_w, w2b_ref[...]))
    a = jax.nn.sigmoid(a0_ref[...] + _dot(_dot(xa, a1_ref[...]).astype(BF16), a2_ref[...]))
    g_out[0] = _dot(jax.nn.sigmoid(_dot(xg, g1_ref[...])).astype(BF16), g2_ref[...])
    kk = k * kk_ref[...]
    norm = jnp.maximum(jnp.sqrt(_dot_f32(kk * kk, seg_ref[...])), 1e-12)
    kk = kk * _dot_f32(1.0 / norm, segt_ref[...])
    r_out[0] = r
    k_out[0] = k * (1.0 + (a - 1.0) * ka_ref[...])
    v_out[0] = v
    rm_out[0] = -kk
    add_out[0] = kk * a


def _rwkv_direction(lw, r, k, v, rm, add, state_ref, reverse, n_heads):
    c = lw.shape[0]
    hd = RWKV_HEAD_DIM
    ti = lax.broadcasted_iota(jnp.int32, (c, c), 0)
    si = lax.broadcasted_iota(jnp.int32, (c, c), 1)
    incl = (si >= ti) if reverse else (si <= ti)
    strict = (si > ti) if reverse else (si < ti)
    eye = jnp.where(si == ti, 1.0, 0.0).astype(F32)
    tri = jnp.where(incl, 1.0, 0.0).astype(F32)
    cum = _dot_f32(tri, lw)
    tot = cum[0:1, :] if reverse else cum[c - 1:c, :]
    e_pos = jnp.exp(cum)
    e_neg = jnp.exp(-cum)
    e_end = jnp.exp(tot - cum)
    r_t = (r * e_pos).astype(BF16)
    rm_t = (rm * jnp.exp(cum - lw)).astype(BF16)
    add_t = (add * e_neg).astype(BF16)
    k_t = (k * e_neg).astype(BF16)
    add_h = (add * e_end).astype(BF16)
    k_h = (k * e_end).astype(BF16)
    v_b = v.astype(BF16)
    p_end = jnp.exp(tot)
    outs = []
    for j in range(n_heads):
        cols = slice(j * hd, (j + 1) * hd)
        lhs = jnp.concatenate([rm_t[:, cols], r_t[:, cols]], axis=0)
        ab = _dot_nt(lhs, add_t[:, cols])
        ak = _dot_nt(lhs, k_t[:, cols])
        a_ab = jnp.where(strict, ab[:c], 0.0).astype(BF16)
        a_ak = jnp.where(strict, ak[:c], 0.0).astype(BF16)
        a_rb = jnp.where(incl, ab[c:], 0.0).astype(BF16)
        a_rk = jnp.where(incl, ak[c:], 0.0).astype(BF16)
        inv = eye + a_ab.astype(F32)
        pw = a_ab
        span = 1
        while 2 * span < c:
            pw_f = _dot(pw, pw)
            pw = pw_f.astype(BF16)
            inv = inv + _dot(inv.astype(BF16), pw)
            span *= 2
        inv = inv.astype(BF16)
        vj = v_b[:, cols]
        s0 = state_ref[:, cols]
        s0_b = s0.astype(BF16)
        w_mat = _dot(inv, rm_t[:, cols]).astype(BF16)
        u_mat = _dot(inv, _dot(a_ak, vj).astype(BF16))
        sa = _dot_nt(w_mat, s0_b) + u_mat
        sa_b = sa.astype(BF16)
        outs.append(_dot_nt(r_t[:, cols], s0_b) + _dot(a_rb, sa_b) + _dot(a_rk, vj))
        state_ref[:, cols] = (s0 * p_end[:, cols] + _dot_tn(sa_b, add_h[:, cols])
                              + _dot_tn(vj, k_h[:, cols]))
    return jnp.concatenate(outs, axis=-1)


def _rwkv_scan_body(lwf_ref, rf_ref, kf_ref, vf_ref, rmf_ref, addf_ref,
                    lwb_ref, rb_ref, kb_ref, vb_ref, rmb_ref, addb_ref,
                    yf_ref, yb_ref, sf_ref, sb_ref, *, n_heads):
    @pl.when(pl.program_id(2) == 0)
    def _():
        sf_ref[...] = jnp.zeros(sf_ref.shape, F32)
        sb_ref[...] = jnp.zeros(sb_ref.shape, F32)

    yf_ref[0] = _rwkv_direction(lwf_ref[0], rf_ref[0], kf_ref[0], vf_ref[0], rmf_ref[0], addf_ref[0],
                                sf_ref, False, n_heads)
    yb_ref[0] = _rwkv_direction(lwb_ref[0], rb_ref[0], kb_ref[0], vb_ref[0], rmb_ref[0], addb_ref[0],
                                sb_ref, True, n_heads)


def _rwkv_out_body(yf_ref, yb_ref, r_ref, k_ref, v_ref, g_ref, rk_ref, lnw_ref, lnb_ref,
                   seg_ref, segt_ref, o_ref):
    seg, segt = seg_ref[...], segt_ref[...]
    head_mean = lambda x: _dot_f32(_dot_f32(x, seg) * (1.0 / RWKV_HEAD_DIM), segt)
    y = yf_ref[...] + yb_ref[...]
    cen = y - head_mean(y)
    var = head_mean(cen * cen)
    y = cen * lax.rsqrt(var + RWKV_LN_EPS) * lnw_ref[...] + lnb_ref[...]
    v = v_ref[...]
    bonus = head_mean(r_ref[...] * k_ref[...] * rk_ref[...]) * RWKV_HEAD_DIM
    o_ref[...] = ((y + bonus * v) * g_ref[...]).astype(o_ref.dtype)


def _pad_to(w, axis, size):
    pad = [(0, 0)] * w.ndim
    pad[axis] = (0, size - w.shape[axis])
    return jnp.pad(w, pad)


def _rwkv_mixer(h3, g_pre, mu, w_rkv, w0, w1, w2, a0, a1, a2, g1, g2, k_k, k_a, r_k, ln_w, ln_b, *,
                tm=256, heads_per_step=4):
    bsz, length, d = h3.shape
    t = bsz * length
    n_heads = d // RWKV_HEAD_DIM
    c = RWKV_CHUNK
    nc = length // c
    hw = heads_per_step * RWKV_HEAD_DIM
    assert length % tm == 0 and length % c == 0 and n_heads % heads_per_step == 0
    assert n_heads <= LANES
    rank_w = w1.shape[-1]
    w1_cat = jnp.concatenate([w1[0], w1[1]], axis=-1).astype(BF16)
    w2f = jnp.concatenate([w2[0], jnp.zeros_like(w2[1])], axis=0).astype(BF16)
    w2b = jnp.concatenate([jnp.zeros_like(w2[0]), w2[1]], axis=0).astype(BF16)
    a1p = _pad_to(a1, 1, LANES).astype(BF16)
    a2p = _pad_to(a2, 0, LANES).astype(BF16)
    g_rank = -(-g1.shape[1] // LANES) * LANES
    g1p = _pad_to(g1, 1, g_rank).astype(BF16)
    g2p = _pad_to(g2, 0, g_rank).astype(BF16)
    seg = _pad_to(jnp.repeat(jnp.eye(n_heads, dtype=F32), RWKV_HEAD_DIM, axis=0), 1, LANES)
    segt = seg.T
    hb = tm // SUBLANES
    nhb = length // SUBLANES
    cur = pl.BlockSpec((1, tm, d), lambda b, i: (b, i, 0))
    outs = pl.pallas_call(
        _rwkv_in_body,
        grid=(bsz, length // tm),
        in_specs=[cur,
                  pl.BlockSpec((1, SUBLANES, d), lambda b, i: (b, jnp.maximum(i * hb - 1, 0), 0)),
                  pl.BlockSpec((1, SUBLANES, d), lambda b, i: (b, jnp.minimum((i + 1) * hb, nhb - 1), 0)),
                  _const_spec((1, d)), _const_spec((6, d)),
                  _const_spec((d, d)), _const_spec((d, d)), _const_spec((d, d)),
                  _const_spec((2, d)), _const_spec((d, 2 * rank_w)),
                  _const_spec((2 * rank_w, d)), _const_spec((2 * rank_w, d)),
                  _const_spec((1, d)), _const_spec((d, LANES)), _const_spec((LANES, d)),
                  _const_spec((d, g_rank)), _const_spec((g_rank, d)),
                  _const_spec((1, d)), _const_spec((1, d)),
                  _const_spec((d, LANES)), _const_spec((LANES, d))],
        out_specs=[cur] * 8,
        out_shape=[jax.ShapeDtypeStruct((bsz, length, d), F32)] * 8,
        compiler_params=_params(("parallel", "parallel")),
        name="rwkv_in",
    )(h3, h3, h3, _row(g_pre), mu.astype(F32), w_rkv[0].astype(BF16), w_rkv[1].astype(BF16),
      w_rkv[2].astype(BF16), w0.astype(F32), w1_cat, w2f, w2b, _row(a0), a1p, a2p, g1p, g2p,
      _row(k_k), _row(k_a), seg, segt)
    r, k, v, lwf, lwb, rm, add, g = outs
    fwd = pl.BlockSpec((1, c, hw), lambda b, hg, i: (b, i, hg))
    bwd = pl.BlockSpec((1, c, hw), lambda b, hg, i: (b, nc - 1 - i, hg))
    y_f, y_b = pl.pallas_call(
        functools.partial(_rwkv_scan_body, n_heads=heads_per_step),
        grid=(bsz, n_heads // heads_per_step, nc),
        in_specs=[fwd] * 6 + [bwd] * 6,
        out_specs=[fwd, bwd],
        out_shape=[jax.ShapeDtypeStruct((bsz, length, d), F32)] * 2,
        scratch_shapes=[pltpu.VMEM((RWKV_HEAD_DIM, hw), F32)] * 2,
        compiler_params=_params(("parallel", "parallel", "arbitrary")),
        name="rwkv_scan",
    )(lwf, r, k, v, rm, add, lwb, r, k, v, rm, add)
    tmo = 512
    row = pl.BlockSpec((tmo, d), lambda i: (i, 0))
    flat = lambda x: x.reshape(t, d)
    return pl.pallas_call(
        _rwkv_out_body,
        grid=(t // tmo,),
        in_specs=[row] * 6 + [_const_spec((1, d))] * 3 + [_const_spec((d, LANES)), _const_spec((LANES, d))],
        out_specs=row,
        out_shape=jax.ShapeDtypeStruct((t, d), BF16),
        compiler_params=_params(("parallel",)),
        name="rwkv_out",
    )(flat(y_f), flat(y_b), flat(r), flat(k), flat(v), flat(g), _row(r_k), _row(ln_w), _row(ln_b),
      seg, segt)


def kernel(x_prompt, x_sample, mix_pre_norm, mix_post_norm, ffn_pre_norm, ffn_post_norm, mla_w_in, mla_q_norm, mla_w_qb, mla_kv_norm, mla_w_kvb, mla_w_o, ssm_w_in, ssm_conv_w, ssm_conv_b, ssm_a_log, ssm_dt_bias, ssm_d_skip, ssm_gate_norm, ssm_w_out, rwkv_mu, rwkv_w_rkv, rwkv_w0, rwkv_w1, rwkv_w2, rwkv_a0, rwkv_a1, rwkv_a2, rwkv_g1, rwkv_g2, rwkv_k_k, rwkv_k_a, rwkv_r_k, rwkv_ln_w, rwkv_ln_b, rwkv_w_o, ffn_w_in, ffn_w_out):
    depth = mix_pre_norm.shape[0]

    def trunk(x):
        bsz, length, d = x.shape
        h = x.reshape(bsz * length, d)
        for i in range(depth):
            kind, j = i % N_MIXERS, i // N_MIXERS
            h3 = h.reshape(bsz, length, d)
            if kind == 0:
                y = _mla_mixer(h3, mix_pre_norm[i], mla_w_in[j], mla_q_norm[j], mla_w_qb[j],
                               mla_kv_norm[j], mla_w_kvb[j])
                y, w_o = y.reshape(bsz * length, -1), mla_w_o[j]
            elif kind == 1:
                y = _ssd_mixer(h3, mix_pre_norm[i], ssm_w_in[j], ssm_conv_w[j], ssm_conv_b[j],
                               ssm_a_log[j], ssm_dt_bias[j], ssm_d_skip[j], ssm_gate_norm[j])
                w_o = ssm_w_out[j]
            else:
                y = _rwkv_mixer(h3, mix_pre_norm[i], rwkv_mu[j], rwkv_w_rkv[j], rwkv_w0[j], rwkv_w1[j],
                                rwkv_w2[j], rwkv_a0[j], rwkv_a1[j], rwkv_a2[j], rwkv_g1[j], rwkv_g2[j],
                                rwkv_k_k[j], rwkv_k_a[j], rwkv_r_k[j], rwkv_ln_w[j], rwkv_ln_b[j])
                w_o = rwkv_w_o[j]
            h = _layer_tail(h, y, w_o, mix_post_norm[i], ffn_pre_norm[i], ffn_w_in[i], ffn_w_out[i],
                            ffn_post_norm[i])
        return h.reshape(bsz, length, d)

    return (trunk(x_prompt), trunk(x_sample))
```

```python
import functools

import jax
import jax.numpy as jnp
import numpy as np
from jax import lax
from jax.experimental import pallas as pl
from jax.experimental.pallas import tpu as pltpu

F32 = jnp.float32
BF16 = jnp.bfloat16

NORM_EPS = 1e-6
N_MIXERS = 3

VMEM_LIMIT_BYTES = 56 * 1024 * 1024
LANES = 128
SUBLANES = 8

MLA_HEADS = 8
MLA_NOPE = 128
MLA_ROPE = 64
MLA_V = 128
MLA_QK = MLA_NOPE + 2 * MLA_ROPE
ROPE_THETA = 10000.0

SSM_HEAD_DIM = 64
SSM_GROUPS = 4
SSM_STATE = 128
SSM_CONV = 5
SSM_CHUNK = 128

RWKV_HEAD_DIM = 64
RWKV_LN_EPS = 64e-5
RWKV_CHUNK = 64


def _rms(x, g, eps=NORM_EPS):
    return x * lax.rsqrt(jnp.mean(x * x, axis=-1, keepdims=True) + eps) * g


def _dot(a, b):
    return jnp.dot(a, b, preferred_element_type=F32)


def _dot_nt(a, b):
    return lax.dot_general(a, b, (((1,), (1,)), ((), ())), preferred_element_type=F32)


def _dot_tn(a, b):
    return lax.dot_general(a, b, (((0,), (0,)), ((), ())), preferred_element_type=F32)


def _bf16_terms(x, terms):
    pieces = []
    for _ in range(terms):
        p = x.astype(BF16)
        pieces.append(p)
        x = x - p.astype(F32)
    return pieces


def _dot_split(m, x, terms=2):
    return sum(_dot(m, p) for p in _bf16_terms(x, terms))


def _split_dot(x, m, terms=2):
    return sum(_dot(p, m) for p in _bf16_terms(x, terms))


def _silu(x):
    return x * jax.nn.sigmoid(x)


def _softplus(x):
    return jnp.maximum(x, 0.0) + jnp.log(1.0 + jnp.exp(-jnp.abs(x)))


def _const_spec(shape):
    nd = len(shape)
    return pl.BlockSpec(shape, lambda *_: (0,) * nd, pipeline_mode=pl.Buffered(1))


def _params(semantics):
    return pltpu.CompilerParams(dimension_semantics=semantics, vmem_limit_bytes=VMEM_LIMIT_BYTES)


def _row(v):
    return v.reshape(1, -1).astype(F32)


def _tail_body(h_ref, y_ref, wo_ref, gpost_ref, gpre_ref, win_ref, wout_ref, gfpost_ref, o_ref, *,
               ff_chunk):
    d_ff = wout_ref.shape[0]
    h1 = h_ref[...] + _rms(_dot(y_ref[...], wo_ref[...]), gpost_ref[...])
    u = _rms(h1, gpre_ref[...]).astype(BF16)
    acc = jnp.zeros(h1.shape, F32)
    for c in range(d_ff // ff_chunk):
        lo = c * ff_chunk
        gate = _dot(u, win_ref[:, lo:lo + ff_chunk])
        up = _dot(u, win_ref[:, d_ff + lo:d_ff + lo + ff_chunk])
        act = (_silu(gate) * up).astype(BF16)
        acc = acc + _dot(act, wout_ref[lo:lo + ff_chunk, :])
    o_ref[...] = h1 + _rms(acc, gfpost_ref[...])


def _layer_tail(h, y, w_o, g_post, g_pre, w_in, w_out, g_fpost, *, tm=512, ff_chunk=256):
    t, d = h.shape
    ky = y.shape[1]
    d_ff = w_out.shape[0]
    assert t % tm == 0 and d_ff % ff_chunk == 0 and ff_chunk % LANES == 0
    row = lambda c: pl.BlockSpec((tm, c), lambda i: (i, 0))
    return pl.pallas_call(
        functools.partial(_tail_body, ff_chunk=ff_chunk),
        grid=(t // tm,),
        in_specs=[row(d), row(ky), _const_spec((ky, d)), _const_spec((1, d)), _const_spec((1, d)),
                  _const_spec((d, 2 * d_ff)), _const_spec((d_ff, d)), _const_spec((1, d))],
        out_specs=row(d),
        out_shape=jax.ShapeDtypeStruct((t, d), F32),
        compiler_params=_params(("parallel",)),
        name="layer_tail",
    )(h, y, w_o.astype(BF16), _row(g_post), _row(g_pre), w_in.astype(BF16), w_out.astype(BF16),
      _row(g_fpost))


def _mla_in_body(h_ref, fq_ref, fk_ref, gpre_ref, win_ref, qn_ref, wq_ref, kvn_ref, wkv_ref,
                 q_ref, k_ref, v_ref, *, q_rank, kv_rank):
    u = _rms(h_ref[0], gpre_ref[...]).astype(BF16)
    lat = _dot(u, win_ref[...])
    q_lat = _rms(lat[:, :q_rank], qn_ref[...]).astype(BF16)
    kv_lat = _rms(lat[:, q_rank:q_rank + kv_rank], kvn_ref[...]).astype(BF16)
    kr = lat[:, q_rank + kv_rank:q_rank + kv_rank + LANES]
    kr_rot = lat[:, q_rank + kv_rank + LANES:q_rank + kv_rank + 2 * LANES]
    fk = fk_ref[...]
    k_rope = (kr * fk[:, :LANES] + kr_rot * fk[:, LANES:]).astype(BF16)
    fq = fq_ref[...]
    q = _dot(q_lat, wq_ref[...])
    kv = _dot(kv_lat, wkv_ref[...])
    for hd in range(MLA_HEADS):
        q_ref[0, :, hd * MLA_QK:(hd + 1) * MLA_QK] = (
            q[:, hd * MLA_QK:(hd + 1) * MLA_QK] * fq).astype(BF16)
        k_ref[0, :, hd * MLA_QK:hd * MLA_QK + MLA_NOPE] = (
            kv[:, hd * MLA_NOPE:(hd + 1) * MLA_NOPE].astype(BF16))
        k_ref[0, :, hd * MLA_QK + MLA_NOPE:(hd + 1) * MLA_QK] = k_rope
    v_ref[0] = kv[:, MLA_HEADS * MLA_NOPE:].astype(BF16)


def _mla_attn_body(q_ref, k_ref, v_ref, o_ref):
    s = _dot_nt(q_ref[0], k_ref[0])
    m = jnp.max(s, axis=-1, keepdims=True)
    p = jnp.exp(s - m)
    l = jnp.sum(p, axis=-1, keepdims=True)
    o = _dot(p.astype(BF16), v_ref[0])
    o_ref[0] = (o / l).astype(o_ref.dtype)


def _rot_cols(w):
    half = w.shape[-1] // 2
    return jnp.concatenate([-w[..., half:], w[..., :half]], axis=-1)


def _mla_mixer(h3, g_pre, w_in, q_norm, w_qb, kv_norm, w_kvb, *, tm=512, tq=256):
    bsz, length, d = h3.shape
    q_rank, kv_rank = q_norm.shape[0], kv_norm.shape[0]
    hd_q = MLA_NOPE + MLA_ROPE
    inv = 1.0 / (ROPE_THETA ** (jnp.arange(0, MLA_ROPE, 2, dtype=F32) / MLA_ROPE))
    ang = jnp.arange(length, dtype=F32)[:, None] * inv[None, :]
    cc = jnp.concatenate([jnp.cos(ang), jnp.cos(ang)], axis=-1)
    ss = jnp.concatenate([jnp.sin(ang), jnp.sin(ang)], axis=-1)
    scale = hd_q ** -0.5
    fq = jnp.concatenate([jnp.full((length, MLA_NOPE), scale, F32), cc * scale, ss * scale], axis=-1)
    fk = jnp.concatenate([cc, cc, ss, ss], axis=-1)
    w_kr = w_in[:, q_rank + kv_rank:]
    w_in_ext = jnp.concatenate([w_in[:, :q_rank + kv_rank], w_kr, w_kr, _rot_cols(w_kr), _rot_cols(w_kr)],
                               axis=-1).astype(BF16)
    wq = w_qb.reshape(q_rank, MLA_HEADS, hd_q)
    wq_ext = jnp.concatenate([wq, _rot_cols(wq[..., MLA_NOPE:])], axis=-1).reshape(
        q_rank, MLA_HEADS * MLA_QK).astype(BF16)
    wkv = w_kvb.reshape(kv_rank, MLA_HEADS, MLA_NOPE + MLA_V)
    wkv_perm = jnp.concatenate([wkv[..., :MLA_NOPE].reshape(kv_rank, -1),
                                wkv[..., MLA_NOPE:].reshape(kv_rank, -1)], axis=-1).astype(BF16)
    n_in = w_in_ext.shape[1]
    assert length % tm == 0 and length % tq == 0
    q, k, v = pl.pallas_call(
        functools.partial(_mla_in_body, q_rank=q_rank, kv_rank=kv_rank),
        grid=(bsz, length // tm),
        in_specs=[pl.BlockSpec((1, tm, d), lambda b, i: (b, i, 0)),
                  pl.BlockSpec((tm, MLA_QK), lambda b, i: (i, 0)),
                  pl.BlockSpec((tm, MLA_QK), lambda b, i: (i, 0)),
                  _const_spec((1, d)), _const_spec((d, n_in)), _const_spec((1, q_rank)),
                  _const_spec((q_rank, MLA_HEADS * MLA_QK)), _const_spec((1, kv_rank)),
                  _const_spec((kv_rank, MLA_HEADS * (MLA_NOPE + MLA_V)))],
        out_specs=[pl.BlockSpec((1, tm, MLA_HEADS * MLA_QK), lambda b, i: (b, i, 0)),
                   pl.BlockSpec((1, tm, MLA_HEADS * MLA_QK), lambda b, i: (b, i, 0)),
                   pl.BlockSpec((1, tm, MLA_HEADS * MLA_V), lambda b, i: (b, i, 0))],
        out_shape=[jax.ShapeDtypeStruct((bsz, length, MLA_HEADS * MLA_QK), BF16),
                   jax.ShapeDtypeStruct((bsz, length, MLA_HEADS * MLA_QK), BF16),
                   jax.ShapeDtypeStruct((bsz, length, MLA_HEADS * MLA_V), BF16)],
        compiler_params=_params(("parallel", "parallel")),
        name="mla_in",
    )(h3, fq, fk, _row(g_pre), w_in_ext, _row(q_norm), wq_ext, _row(kv_norm), wkv_perm)
    return pl.pallas_call(
        _mla_attn_body,
        grid=(bsz, MLA_HEADS, length // tq),
        in_specs=[pl.BlockSpec((1, tq, MLA_QK), lambda b, hd, i: (b, i, hd)),
                  pl.BlockSpec((1, length, MLA_QK), lambda b, hd, i: (b, 0, hd)),
                  pl.BlockSpec((1, length, MLA_V), lambda b, hd, i: (b, 0, hd))],
        out_specs=pl.BlockSpec((1, tq, MLA_V), lambda b, hd, i: (b, i, hd)),
        out_shape=jax.ShapeDtypeStruct((bsz, length, MLA_HEADS * MLA_V), BF16),
        compiler_params=_params(("parallel", "parallel", "parallel")),
        name="mla_attn",
    )(q, k, v)


def _ssd_in_body(h_ref, gpre_ref, wz_ref, wx_ref, wdt_ref, z_ref, xbc_ref, dt_ref):
    u = _rms(h_ref[...], gpre_ref[...]).astype(BF16)
    z_ref[...] = _dot(u, wz_ref[...])
    xbc_ref[...] = _dot(u, wx_ref[...])
    dt_ref[...] = _dot(u, wdt_ref[...])


def _ssd_conv_body(x_ref, w_ref, b_ref, o_ref):
    x = x_ref[0]
    length = x.shape[0]
    pad = SSM_CONV // 2
    t = lax.broadcasted_iota(jnp.int32, x.shape, 0)
    acc = x * w_ref[pad:pad + 1, :] + b_ref[...]
    for kk in range(SSM_CONV):
        off = kk - pad
        if off == 0:
            continue
        shifted = pltpu.roll(x, (-off) % length, axis=0)
        valid = (t >= -off) if off < 0 else (t < length - off)
        acc = acc + jnp.where(valid, shifted, 0.0) * w_ref[kk:kk + 1, :]
    o_ref[0] = _silu(acc)


def _ssd_direction(xbc, dt_raw, bias, a_neg, expand, state_ref, reverse, d_inner):
    q = xbc.shape[0]
    gn = SSM_GROUPS * SSM_STATE
    hpg = d_inner // SSM_HEAD_DIM // SSM_GROUPS
    xs = xbc[:, :d_inner]
    b_in = xbc[:, d_inner:d_inner + gn]
    c_in = xbc[:, d_inner + gn:]
    dt = _softplus(dt_raw + bias)
    a_dt = dt * a_neg
    ti = lax.broadcasted_iota(jnp.int32, (q, q), 0)
    si = lax.broadcasted_iota(jnp.int32, (q, q), 1)
    keep = (si >= ti) if reverse else (si <= ti)
    tri = jnp.where(keep, 1.0, 0.0).astype(BF16)
    cum = _dot_split(tri, a_dt, terms=3)
    cum_t = jnp.transpose(cum)
    edge = cum[0:1, :] if reverse else cum[q - 1:q, :]
    x_dt = xs * _split_dot(dt, expand)
    x_end = (xs * _split_dot(dt * jnp.exp(edge - cum), expand)).astype(BF16)
    scale_in = _split_dot(jnp.exp(cum), expand)
    decay_state = _split_dot(jnp.exp(edge), expand)
    x_dt_b = x_dt.astype(BF16)
    outs = []
    for g in range(SSM_GROUPS):
        bg = b_in[:, g * SSM_STATE:(g + 1) * SSM_STATE].astype(BF16)
        cg = c_in[:, g * SSM_STATE:(g + 1) * SSM_STATE].astype(BF16)
        cb = _dot_nt(cg, bg)
        cols = slice(g * hpg * SSM_HEAD_DIM, (g + 1) * hpg * SSM_HEAD_DIM)
        st = state_ref[:, cols]
        y_off = _dot(cg, st.astype(BF16)) * scale_in[:, cols]
        for j in range(hpg):
            hd = g * hpg + j
            seg = cum[:, hd:hd + 1] - cum_t[hd:hd + 1, :]
            m = (cb * jnp.exp(jnp.where(keep, seg, -jnp.inf))).astype(BF16)
            hc = slice(hd * SSM_HEAD_DIM, (hd + 1) * SSM_HEAD_DIM)
            y_diag = _dot(m, x_dt_b[:, hc])
            outs.append(y_diag + y_off[:, j * SSM_HEAD_DIM:(j + 1) * SSM_HEAD_DIM])
        state_ref[:, cols] = st * decay_state[:, cols] + _dot_tn(bg, x_end[:, cols])
    return jnp.concatenate(outs, axis=-1)


def _ssd_scan_body(xf_ref, dtf_ref, xb_ref, dtb_ref, bias_ref, alog_ref, exp_ref,
                   yf_ref, yb_ref, sf_ref, sb_ref, *, d_inner):
    @pl.when(pl.program_id(1) == 0)
    def _():
        sf_ref[...] = jnp.zeros(sf_ref.shape, F32)
        sb_ref[...] = jnp.zeros(sb_ref.shape, F32)

    nh = d_inner // SSM_HEAD_DIM
    a_neg = -jnp.exp(alog_ref[...])
    expand = exp_ref[...]
    yf_ref[0] = _ssd_direction(xf_ref[0], dtf_ref[0][:, :nh], bias_ref[0:1, :], a_neg[0:1, :], expand,
                               sf_ref, False, d_inner)
    yb_ref[0] = _ssd_direction(xb_ref[0], dtb_ref[0][:, nh:], bias_ref[1:2, :], a_neg[1:2, :], expand,
                               sb_ref, True, d_inner)


def _ssd_out_body(yf_ref, yb_ref, xbc_ref, z_ref, dsk_ref, gn_ref, o_ref, *, d_inner):
    xs = xbc_ref[:, :d_inner]
    y = (yf_ref[...] + yb_ref[...] + xs * dsk_ref[...]) * _silu(z_ref[...])
    gw = d_inner // SSM_GROUPS
    for g in range(SSM_GROUPS):
        cols = slice(g * gw, (g + 1) * gw)
        o_ref[:, cols] = _rms(y[:, cols], gn_ref[:, cols]).astype(o_ref.dtype)


def _ssd_mixer(h3, g_pre, w_in, conv_w, conv_b, a_log, dt_bias, d_skip, gate_norm, *, tm=512, cb=256):
    bsz, length, d = h3.shape
    t = bsz * length
    nh = a_log.shape[-1]
    d_inner = nh * SSM_HEAD_DIM
    conv_dim = d_inner + 2 * SSM_GROUPS * SSM_STATE
    q = SSM_CHUNK
    nc = length // q
    assert t % tm == 0 and length % q == 0 and conv_dim % cb == 0
    row = lambda c: pl.BlockSpec((tm, c), lambda i: (i, 0))
    z, xbc, dt = pl.pallas_call(
        _ssd_in_body,
        grid=(t // tm,),
        in_specs=[row(d), _const_spec((1, d)), _const_spec((d, d_inner)), _const_spec((d, conv_dim)),
                  _const_spec((d, 2 * nh))],
        out_specs=[row(d_inner), row(conv_dim), row(2 * nh)],
        out_shape=[jax.ShapeDtypeStruct((t, d_inner), F32), jax.ShapeDtypeStruct((t, conv_dim), F32),
                   jax.ShapeDtypeStruct((t, 2 * nh), F32)],
        compiler_params=_params(("parallel",)),
        name="ssd_in",
    )(h3.reshape(t, d), _row(g_pre), w_in[:, :d_inner].astype(BF16),
      w_in[:, d_inner:d_inner + conv_dim].astype(BF16), w_in[:, d_inner + conv_dim:].astype(BF16))
    xbc = pl.pallas_call(
        _ssd_conv_body,
        grid=(bsz, conv_dim // cb),
        in_specs=[pl.BlockSpec((1, length, cb), lambda b, j: (b, 0, j)),
                  pl.BlockSpec((SSM_CONV, cb), lambda b, j: (0, j)),
                  pl.BlockSpec((1, cb), lambda b, j: (0, j))],
        out_specs=pl.BlockSpec((1, length, cb), lambda b, j: (b, 0, j)),
        out_shape=jax.ShapeDtypeStruct((bsz, length, conv_dim), F32),
        compiler_params=_params(("parallel", "parallel")),
        name="ssd_conv",
    )(xbc.reshape(bsz, length, conv_dim), conv_w.astype(F32), _row(conv_b))
    dt3 = dt.reshape(bsz, length, 2 * nh)
    expand = jnp.repeat(jnp.eye(nh, dtype=BF16), SSM_HEAD_DIM, axis=1)
    fwd = lambda c: pl.BlockSpec((1, q, c), lambda b, i: (b, i, 0))
    bwd = lambda c: pl.BlockSpec((1, q, c), lambda b, i: (b, nc - 1 - i, 0))
    y_f, y_b = pl.pallas_call(
        functools.partial(_ssd_scan_body, d_inner=d_inner),
        grid=(bsz, nc),
        in_specs=[fwd(conv_dim), fwd(2 * nh), bwd(conv_dim), bwd(2 * nh),
                  _const_spec((2, nh)), _const_spec((2, nh)), _const_spec((nh, d_inner))],
        out_specs=[fwd(d_inner), bwd(d_inner)],
        out_shape=[jax.ShapeDtypeStruct((bsz, length, d_inner), F32)] * 2,
        scratch_shapes=[pltpu.VMEM((SSM_STATE, d_inner), F32)] * 2,
        compiler_params=_params(("parallel", "arbitrary")),
        name="ssd_scan",
    )(xbc, dt3, xbc, dt3, dt_bias.astype(F32), a_log.astype(F32), expand)
    return pl.pallas_call(
        functools.partial(_ssd_out_body, d_inner=d_inner),
        grid=(t // tm,),
        in_specs=[row(d_inner), row(d_inner), row(conv_dim), row(d_inner),
                  _const_spec((1, d_inner)), _const_spec((1, d_inner))],
        out_specs=row(d_inner),
        out_shape=jax.ShapeDtypeStruct((t, d_inner), BF16),
        compiler_params=_params(("parallel",)),
        name="ssd_out",
    )(y_f.reshape(t, d_inner), y_b.reshape(t, d_inner), xbc.reshape(t, conv_dim), z,
      _row(jnp.repeat(d_skip.astype(F32), SSM_HEAD_DIM)), _row(gate_norm))


def _rwkv_in_body(h_ref, hp_ref, hn_ref, gpre_ref, mu_ref, wr_ref, wk_ref, wv_ref, w0_ref, w1_ref,
                  w2f_ref, w2b_ref, a0_ref, a1_ref, a2_ref, g1_ref, g2_ref, kk_ref, ka_ref,
                  rk_ref, seg_ref, segt_ref, tri_ref, up_ref,
                  v_out, bv_out, g_out, pendf_out, pendb_out, *dir_outs):
    i = pl.program_id(1)
    n_i = pl.num_programs(1)
    gpre = gpre_ref[...]
    x = _rms(h_ref[0], gpre)
    tm = x.shape[0]
    prev_row = _rms(hp_ref[0][SUBLANES - 1:SUBLANES, :], gpre) * jnp.where(i > 0, 1.0, 0.0)
    next_row = _rms(hn_ref[0][0:1, :], gpre) * jnp.where(i < n_i - 1, 1.0, 0.0)
    t = lax.broadcasted_iota(jnp.int32, x.shape, 0)
    x_prev = jnp.where(t == 0, prev_row, pltpu.roll(x, 1, axis=0))
    x_next = jnp.where(t == tm - 1, next_row, pltpu.roll(x, tm - 1, axis=0))
    xx = 0.5 * (x_prev + x_next) - x
    mix = lambda j: (x + xx * mu_ref[j:j + 1, :]).astype(BF16)
    xr, xw, xk, xv, xa, xg = (mix(j) for j in range(6))
    r = _dot(xr, wr_ref[...])
    k = _dot(xk, wk_ref[...])
    v = _dot(xv, wv_ref[...])
    lora_w = jnp.tanh(_dot(xw, w1_ref[...])).astype(BF16)
    log_decay = lambda w: -np.float32(np.exp(-0.5)) * jax.nn.sigmoid(w)
    lw_f = log_decay(w0_ref[0:1, :] + _dot(lora_w, w2f_ref[...]))
    lw_b = log_decay(w0_ref[1:2, :] + _dot(lora_w, w2b_ref[...]))
    a = jax.nn.sigmoid(a0_ref[...] + _dot(_dot(xa, a1_ref[...]).astype(BF16), a2_ref[...]))
    g_out[0] = _dot(jax.nn.sigmoid(_dot(xg, g1_ref[...])).astype(BF16), g2_ref[...])
    kk = k * kk_ref[...]
    norm = jnp.maximum(jnp.sqrt(_split_dot(kk * kk, seg_ref[...])), 1e-12)
    kk = kk * _split_dot(1.0 / norm, segt_ref[...])
    k = k * (1.0 + (a - 1.0) * ka_ref[...])
    rm = -kk
    add = kk * a
    v_out[0] = v.astype(v_out.dtype)
    bonus = _split_dot(_split_dot(r * k * rk_ref[...], seg_ref[...]), segt_ref[...])
    bv_out[0] = bonus * v
    tri, up = tri_ref[...], up_ref[...]
    c = RWKV_CHUNK
    for lw, cum, rest, pend_out, outs in (
            (lw_f, _dot_split(tri, lw_f), _dot_split(up, lw_f), pendf_out, dir_outs[:6]),
            (lw_b, lw_b + _dot_split(up, lw_b), _dot_split(tri, lw_b) - lw_b, pendb_out, dir_outs[6:])):
        e_pos = jnp.exp(cum)
        e_neg = jnp.exp(-cum)
        e_end = jnp.exp(rest)
        values = (r * e_pos, rm * jnp.exp(cum - lw), add * e_neg, k * e_neg, add * e_end, k * e_end)
        for o_ref, val in zip(outs, values):
            o_ref[0] = val.astype(o_ref.dtype)
        total = cum + rest
        for m in range(tm // c):
            pend_out[0, m] = jnp.exp(total[m * c:m * c + 1, :])


def _rwkv_direction(r_t, rm_t, add_t, k_t, add_h, k_h, v_b, p_end, s_all, reverse):
    c = r_t.shape[0]
    hd = RWKV_HEAD_DIM
    ti = lax.broadcasted_iota(jnp.int32, (c, c), 0)
    si = lax.broadcasted_iota(jnp.int32, (c, c), 1)
    incl = (si >= ti) if reverse else (si <= ti)
    strict = (si > ti) if reverse else (si < ti)
    eye = jnp.where(si == ti, 1.0, 0.0).astype(F32)
    heads = range(r_t.shape[1] // hd)
    col = lambda x, j: x[:, j * hd:(j + 1) * hd]
    lhs = [jnp.concatenate([col(rm_t, j), col(r_t, j)], axis=0) for j in heads]
    ab = [_dot_nt(lhs[j], col(add_t, j)) for j in heads]
    ak = [_dot_nt(lhs[j], col(k_t, j)) for j in heads]
    a_ab = [jnp.where(strict, ab[j][:c], 0.0).astype(BF16) for j in heads]
    a_ak = [jnp.where(strict, ak[j][:c], 0.0).astype(BF16) for j in heads]
    a_rb = [jnp.where(incl, ab[j][c:], 0.0).astype(BF16) for j in heads]
    a_rk = [jnp.where(incl, ak[j][c:], 0.0).astype(BF16) for j in heads]
    akv = [_dot(a_ak[j], col(v_b, j)).astype(BF16) for j in heads]
    rkv = [_dot(a_rk[j], col(v_b, j)) for j in heads]
    s0 = [col(s_all, j) for j in heads]
    s0_b = [s.astype(BF16) for s in s0]
    rs = [_dot_nt(col(r_t, j), s0_b[j]) for j in heads]
    kv = [_dot_tn(col(v_b, j), col(k_h, j)) for j in heads]
    inv = [eye + a_ab[j].astype(F32) for j in heads]
    pw = a_ab
    span = 1
    while 2 * span < c:
        pw = [_dot(pw[j], pw[j]).astype(BF16) for j in heads]
        inv = [inv[j] + _dot(inv[j].astype(BF16), pw[j]) for j in heads]
        span *= 2
    inv = [x.astype(BF16) for x in inv]
    w_mat = [_dot(inv[j], col(rm_t, j)).astype(BF16) for j in heads]
    u_mat = [_dot(inv[j], akv[j]) for j in heads]
    sa_b = [(_dot_nt(w_mat[j], s0_b[j]) + u_mat[j]).astype(BF16) for j in heads]
    outs = [rs[j] + _dot(a_rb[j], sa_b[j]) + rkv[j] for j in heads]
    states = [s0[j] * col(p_end, j) + _dot_tn(sa_b[j], col(add_h, j)) + kv[j] for j in heads]
    return jnp.concatenate(outs, axis=-1), jnp.concatenate(states, axis=-1)


def _rwkv_scan_body(*refs):
    f_in, b_in = refs[:8], refs[8:16]
    yf_ref, yb_ref, sf_ref, sb_ref = refs[16:]

    @pl.when(pl.program_id(2) == 0)
    def _():
        sf_ref[...] = jnp.zeros(sf_ref.shape, F32)
        sb_ref[...] = jnp.zeros(sb_ref.shape, F32)

    y_f, s_f = _rwkv_direction(*[ref[0] for ref in f_in[:7]], f_in[7][0, 0], sf_ref[...], False)
    y_b, s_b = _rwkv_direction(*[ref[0] for ref in b_in[:7]], b_in[7][0, 0], sb_ref[...], True)
    yf_ref[0] = y_f
    yb_ref[0] = y_b
    sf_ref[...] = s_f
    sb_ref[...] = s_b


def _rwkv_out_body(yf_ref, yb_ref, bv_ref, g_ref, lnw_ref, lnb_ref, seg_ref, segt_ref, o_ref):
    seg, segt = seg_ref[...], segt_ref[...]
    head_mean = lambda x: _split_dot(_split_dot(x, seg) * (1.0 / RWKV_HEAD_DIM), segt)
    y = yf_ref[...] + yb_ref[...]
    cen = y - head_mean(y)
    var = head_mean(cen * cen)
    y = cen * lax.rsqrt(var + RWKV_LN_EPS) * lnw_ref[...] + lnb_ref[...]
    o_ref[...] = ((y + bv_ref[...]) * g_ref[...]).astype(o_ref.dtype)


def _pad_to(w, axis, size):
    pad = [(0, 0)] * w.ndim
    pad[axis] = (0, size - w.shape[axis])
    return jnp.pad(w, pad)


def _rwkv_mixer(h3, g_pre, mu, w_rkv, w0, w1, w2, a0, a1, a2, g1, g2, k_k, k_a, r_k, ln_w, ln_b, *,
                tm=256, heads_per_step=16):
    bsz, length, d = h3.shape
    t = bsz * length
    n_heads = d // RWKV_HEAD_DIM
    c = RWKV_CHUNK
    nc = length // c
    hw = heads_per_step * RWKV_HEAD_DIM
    assert length % tm == 0 and length % c == 0 and n_heads % heads_per_step == 0
    assert n_heads <= LANES
    rank_w = w1.shape[-1]
    w1_cat = jnp.concatenate([w1[0], w1[1]], axis=-1).astype(BF16)
    w2f = jnp.concatenate([w2[0], jnp.zeros_like(w2[1])], axis=0).astype(BF16)
    w2b = jnp.concatenate([jnp.zeros_like(w2[0]), w2[1]], axis=0).astype(BF16)
    a1p = _pad_to(a1, 1, LANES).astype(BF16)
    a2p = _pad_to(a2, 0, LANES).astype(BF16)
    g_rank = -(-g1.shape[1] // LANES) * LANES
    g1p = _pad_to(g1, 1, g_rank).astype(BF16)
    g2p = _pad_to(g2, 0, g_rank).astype(BF16)
    seg = _pad_to(jnp.repeat(jnp.eye(n_heads, dtype=BF16), RWKV_HEAD_DIM, axis=0), 1, LANES)
    segt = seg.T
    hb = tm // SUBLANES
    nhb = length // SUBLANES
    pos = np.arange(tm)
    same_chunk = (pos[:, None] // c) == (pos[None, :] // c)
    tri = jnp.asarray(same_chunk & (pos[None, :] <= pos[:, None]), BF16)
    up = jnp.asarray(same_chunk & (pos[None, :] > pos[:, None]), BF16)
    cur = pl.BlockSpec((1, tm, d), lambda b, i: (b, i, 0))
    pend_spec = pl.BlockSpec((1, tm // c, 1, d), lambda b, i: (b, i, 0, 0))
    pend_shape = jax.ShapeDtypeStruct((bsz, nc, 1, d), F32)
    act = jax.ShapeDtypeStruct((bsz, length, d), BF16)
    outs = pl.pallas_call(
        _rwkv_in_body,
        grid=(bsz, length // tm),
        in_specs=[cur,
                  pl.BlockSpec((1, SUBLANES, d), lambda b, i: (b, jnp.maximum(i * hb - 1, 0), 0)),
                  pl.BlockSpec((1, SUBLANES, d), lambda b, i: (b, jnp.minimum((i + 1) * hb, nhb - 1), 0)),
                  _const_spec((1, d)), _const_spec((6, d)),
                  _const_spec((d, d)), _const_spec((d, d)), _const_spec((d, d)),
                  _const_spec((2, d)), _const_spec((d, 2 * rank_w)),
                  _const_spec((2 * rank_w, d)), _const_spec((2 * rank_w, d)),
                  _const_spec((1, d)), _const_spec((d, LANES)), _const_spec((LANES, d)),
                  _const_spec((d, g_rank)), _const_spec((g_rank, d)),
                  _const_spec((1, d)), _const_spec((1, d)), _const_spec((1, d)),
                  _const_spec((d, LANES)), _const_spec((LANES, d)),
                  _const_spec((tm, tm)), _const_spec((tm, tm))],
        out_specs=[cur, cur, cur, pend_spec, pend_spec] + [cur] * 12,
        out_shape=[act, jax.ShapeDtypeStruct((bsz, length, d), F32),
                   jax.ShapeDtypeStruct((bsz, length, d), F32), pend_shape, pend_shape] + [act] * 12,
        compiler_params=_params(("parallel", "parallel")),
        name="rwkv_in",
    )(h3, h3, h3, _row(g_pre), mu.astype(F32), w_rkv[0].astype(BF16), w_rkv[1].astype(BF16),
      w_rkv[2].astype(BF16), w0.astype(F32), w1_cat, w2f, w2b, _row(a0), a1p, a2p, g1p, g2p,
      _row(k_k), _row(k_a), _row(r_k), seg, segt, tri, up)
    v, bv, g, pend_f, pend_b = outs[:5]
    ops_f, ops_b = outs[5:11], outs[11:]
    fwd = pl.BlockSpec((1, c, hw), lambda b, hg, i: (b, i, hg))
    bwd = pl.BlockSpec((1, c, hw), lambda b, hg, i: (b, nc - 1 - i, hg))
    fwd_p = pl.BlockSpec((1, 1, 1, hw), lambda b, hg, i: (b, i, 0, hg))
    bwd_p = pl.BlockSpec((1, 1, 1, hw), lambda b, hg, i: (b, nc - 1 - i, 0, hg))
    y_f, y_b = pl.pallas_call(
        _rwkv_scan_body,
        grid=(bsz, n_heads // heads_per_step, nc),
        in_specs=[fwd] * 7 + [fwd_p] + [bwd] * 7 + [bwd_p],
        out_specs=[fwd, bwd],
        out_shape=[jax.ShapeDtypeStruct((bsz, length, d), F32)] * 2,
        scratch_shapes=[pltpu.VMEM((RWKV_HEAD_DIM, hw), F32)] * 2,
        compiler_params=_params(("parallel", "parallel", "arbitrary")),
        name="rwkv_scan",
    )(*ops_f, v, pend_f, *ops_b, v, pend_b)
    tmo = 512
    row = pl.BlockSpec((tmo, d), lambda i: (i, 0))
    flat = lambda x: x.reshape(t, d)
    return pl.pallas_call(
        _rwkv_out_body,
        grid=(t // tmo,),
        in_specs=[row] * 4 + [_const_spec((1, d))] * 2 + [_const_spec((d, LANES)), _const_spec((LANES, d))],
        out_specs=row,
        out_shape=jax.ShapeDtypeStruct((t, d), BF16),
        compiler_params=_params(("parallel",)),
        name="rwkv_out",
    )(flat(y_f), flat(y_b), flat(bv), flat(g), _row(ln_w), _row(ln_b), seg, segt)


def kernel(x_prompt, x_sample, mix_pre_norm, mix_post_norm, ffn_pre_norm, ffn_post_norm, mla_w_in, mla_q_norm, mla_w_qb, mla_kv_norm, mla_w_kvb, mla_w_o, ssm_w_in, ssm_conv_w, ssm_conv_b, ssm_a_log, ssm_dt_bias, ssm_d_skip, ssm_gate_norm, ssm_w_out, rwkv_mu, rwkv_w_rkv, rwkv_w0, rwkv_w1, rwkv_w2, rwkv_a0, rwkv_a1, rwkv_a2, rwkv_g1, rwkv_g2, rwkv_k_k, rwkv_k_a, rwkv_r_k, rwkv_ln_w, rwkv_ln_b, rwkv_w_o, ffn_w_in, ffn_w_out):
    depth = mix_pre_norm.shape[0]

    def trunk(x):
        bsz, length, d = x.shape
        h = x.reshape(bsz * length, d)
        for i in range(depth):
            kind, j = i % N_MIXERS, i // N_MIXERS
            h3 = h.reshape(bsz, length, d)
            if kind == 0:
                y = _mla_mixer(h3, mix_pre_norm[i], mla_w_in[j], mla_q_norm[j], mla_w_qb[j],
                               mla_kv_norm[j], mla_w_kvb[j])
                y, w_o = y.reshape(bsz * length, -1), mla_w_o[j]
            elif kind == 1:
                y = _ssd_mixer(h3, mix_pre_norm[i], ssm_w_in[j], ssm_conv_w[j], ssm_conv_b[j],
                               ssm_a_log[j], ssm_dt_bias[j], ssm_d_skip[j], ssm_gate_norm[j])
                w_o = ssm_w_out[j]
            else:
                y = _rwkv_mixer(h3, mix_pre_norm[i], rwkv_mu[j], rwkv_w_rkv[j], rwkv_w0[j], rwkv_w1[j],
                                rwkv_w2[j], rwkv_a0[j], rwkv_a1[j], rwkv_a2[j], rwkv_g1[j], rwkv_g2[j],
                                rwkv_k_k[j], rwkv_k_a[j], rwkv_r_k[j], rwkv_ln_w[j], rwkv_ln_b[j])
                w_o = rwkv_w_o[j]
            h = _layer_tail(h, y, w_o, mix_post_norm[i], ffn_pre_norm[i], ffn_w_in[i], ffn_w_out[i],
                            ffn_post_norm[i])
        return h.reshape(bsz, length, d)

    return (trunk(x_prompt), trunk(x_sample))
```

```python
import functools

import jax
import jax.numpy as jnp
import numpy as np
from jax import lax
from jax.experimental import pallas as pl
from jax.experimental.pallas import tpu as pltpu

F32 = jnp.float32
BF16 = jnp.bfloat16

NORM_EPS = 1e-6
N_MIXERS = 3

VMEM_LIMIT_BYTES = 56 * 1024 * 1024
LANES = 128
SUBLANES = 8

MLA_HEADS = 8
MLA_NOPE = 128
MLA_ROPE = 64
MLA_V = 128
MLA_QK = MLA_NOPE + 2 * MLA_ROPE
ROPE_THETA = 10000.0

SSM_HEAD_DIM = 64
SSM_GROUPS = 4
SSM_STATE = 128
SSM_CONV = 5
SSM_CHUNK = 128

RWKV_HEAD_DIM = 64
RWKV_LN_EPS = 64e-5
RWKV_CHUNK = 64


def _rms(x, g, eps=NORM_EPS):
    return x * lax.rsqrt(jnp.mean(x * x, axis=-1, keepdims=True) + eps) * g


def _dot(a, b):
    return jnp.dot(a, b, preferred_element_type=F32)


def _dot_nt(a, b):
    return lax.dot_general(a, b, (((1,), (1,)), ((), ())), preferred_element_type=F32)


def _dot_tn(a, b):
    return lax.dot_general(a, b, (((0,), (0,)), ((), ())), preferred_element_type=F32)


def _bf16_terms(x, terms):
    pieces = []
    for _ in range(terms):
        p = x.astype(BF16)
        pieces.append(p)
        x = x - p.astype(F32)
    return pieces


def _dot_split(m, x, terms=2):
    return sum(_dot(m, p) for p in _bf16_terms(x, terms))


def _split_dot(x, m, terms=2):
    return sum(_dot(p, m) for p in _bf16_terms(x, terms))


def _silu(x):
    return x * jax.nn.sigmoid(x)


def _softplus(x):
    return jnp.maximum(x, 0.0) + jnp.log(1.0 + jnp.exp(-jnp.abs(x)))


def _const_spec(shape):
    nd = len(shape)
    return pl.BlockSpec(shape, lambda *_: (0,) * nd, pipeline_mode=pl.Buffered(1))


def _params(semantics):
    return pltpu.CompilerParams(dimension_semantics=semantics, vmem_limit_bytes=VMEM_LIMIT_BYTES)


def _row(v):
    return v.reshape(1, -1).astype(F32)


def _tail_body(*refs, n_mixer_refs, mixer_out, ff_chunk):
    h_ref = refs[0]
    y = mixer_out(*refs[1:1 + n_mixer_refs])
    wo_ref, gpost_ref, gpre_ref, win_ref, wout_ref, gfpost_ref, o_ref = refs[1 + n_mixer_refs:]
    d_ff = wout_ref.shape[0]
    h1 = h_ref[...] + _rms(_dot(y, wo_ref[...]), gpost_ref[...])
    u = _rms(h1, gpre_ref[...]).astype(BF16)
    acc = jnp.zeros(h1.shape, F32)
    for c in range(d_ff // ff_chunk):
        lo = c * ff_chunk
        gate = _dot(u, win_ref[:, lo:lo + ff_chunk])
        up = _dot(u, win_ref[:, d_ff + lo:d_ff + lo + ff_chunk])
        act = (_silu(gate) * up).astype(BF16)
        acc = acc + _dot(act, wout_ref[lo:lo + ff_chunk, :])
    o_ref[...] = h1 + _rms(acc, gfpost_ref[...])


def _layer_tail(h, mixer, w_o, g_post, g_pre, w_in, w_out, g_fpost, *, ff_chunk=256):
    mixer_out, row_inputs, const_inputs, tm = mixer
    t, d = h.shape
    ky = w_o.shape[0]
    d_ff = w_out.shape[0]
    assert t % tm == 0 and d_ff % ff_chunk == 0 and ff_chunk % LANES == 0
    row = lambda c: pl.BlockSpec((tm, c), lambda i: (i, 0))
    return pl.pallas_call(
        functools.partial(_tail_body, n_mixer_refs=len(row_inputs) + len(const_inputs),
                          mixer_out=mixer_out, ff_chunk=ff_chunk),
        grid=(t // tm,),
        in_specs=[row(d)] + [row(width) for _, width in row_inputs]
        + [_const_spec(c.shape) for c in const_inputs]
        + [_const_spec((ky, d)), _const_spec((1, d)), _const_spec((1, d)),
           _const_spec((d, 2 * d_ff)), _const_spec((d_ff, d)), _const_spec((1, d))],
        out_specs=row(d),
        out_shape=jax.ShapeDtypeStruct((t, d), F32),
        compiler_params=_params(("parallel",)),
        name="layer_tail",
    )(h, *[a for a, _ in row_inputs], *const_inputs, w_o.astype(BF16), _row(g_post), _row(g_pre),
      w_in.astype(BF16), w_out.astype(BF16), _row(g_fpost))


def _mla_in_body(h_ref, fq_ref, fk_ref, gpre_ref, win_ref, qn_ref, wq_ref, kvn_ref, wkv_ref,
                 q_ref, k_ref, v_ref, *, q_rank, kv_rank):
    u = _rms(h_ref[0], gpre_ref[...]).astype(BF16)
    lat = _dot(u, win_ref[...])
    q_lat = _rms(lat[:, :q_rank], qn_ref[...]).astype(BF16)
    kv_lat = _rms(lat[:, q_rank:q_rank + kv_rank], kvn_ref[...]).astype(BF16)
    kr = lat[:, q_rank + kv_rank:q_rank + kv_rank + LANES]
    kr_rot = lat[:, q_rank + kv_rank + LANES:q_rank + kv_rank + 2 * LANES]
    fk = fk_ref[...]
    k_rope = (kr * fk[:, :LANES] + kr_rot * fk[:, LANES:]).astype(BF16)
    fq = fq_ref[...]
    q = _dot(q_lat, wq_ref[...])
    kv = _dot(kv_lat, wkv_ref[...])
    for hd in range(MLA_HEADS):
        q_ref[0, :, hd * MLA_QK:(hd + 1) * MLA_QK] = (
            q[:, hd * MLA_QK:(hd + 1) * MLA_QK] * fq).astype(BF16)
        k_ref[0, :, hd * MLA_QK:hd * MLA_QK + MLA_NOPE] = (
            kv[:, hd * MLA_NOPE:(hd + 1) * MLA_NOPE].astype(BF16))
        k_ref[0, :, hd * MLA_QK + MLA_NOPE:(hd + 1) * MLA_QK] = k_rope
    v_ref[0] = kv[:, MLA_HEADS * MLA_NOPE:].astype(BF16)


def _mla_attn_body(q_ref, k_ref, v_ref, o_ref, *, q_sub, kv_chunk):
    n_chunks = k_ref.shape[1] // kv_chunk
    n_sub = q_ref.shape[1] // q_sub
    lane_tiles = lambda x: [x[:, t * LANES:(t + 1) * LANES] for t in range(x.shape[1] // LANES)]
    kv_rows = lambda c: slice(c * kv_chunk, (c + 1) * kv_chunk)
    score = lambda t, c: _dot_nt(q_ref[0, t * q_sub:(t + 1) * q_sub, :], k_ref[0, kv_rows(c), :])
    s_cur = [score(0, c) for c in range(n_chunks)]
    for t in range(n_sub):
        m = jnp.max(functools.reduce(jnp.maximum, [x for s in s_cur for x in lane_tiles(s)]),
                    axis=-1, keepdims=True)
        l_acc = jnp.zeros((q_sub, LANES), F32)
        acc = jnp.zeros((q_sub, v_ref.shape[2]), F32)
        s_next = []
        for c in range(n_chunks):
            if t + 1 < n_sub:
                s_next.append(score(t + 1, c))
            p = jnp.exp2(s_cur[c] - m)
            l_acc = l_acc + sum(lane_tiles(p))
            acc = acc + _dot(p.astype(BF16), v_ref[0, kv_rows(c), :])
        l = jnp.sum(l_acc, axis=-1, keepdims=True)
        o_ref[0, t * q_sub:(t + 1) * q_sub, :] = (acc / l).astype(o_ref.dtype)
        s_cur = s_next


def _rot_cols(w):
    half = w.shape[-1] // 2
    return jnp.concatenate([-w[..., half:], w[..., :half]], axis=-1)


def _mla_mixer(h3, g_pre, w_in, q_norm, w_qb, kv_norm, w_kvb, *, tm=512, tq=1024, q_sub=256,
               kv_chunk=256, tail_tm=512):
    bsz, length, d = h3.shape
    q_rank, kv_rank = q_norm.shape[0], kv_norm.shape[0]
    hd_q = MLA_NOPE + MLA_ROPE
    inv = 1.0 / (ROPE_THETA ** (jnp.arange(0, MLA_ROPE, 2, dtype=F32) / MLA_ROPE))
    ang = jnp.arange(length, dtype=F32)[:, None] * inv[None, :]
    cc = jnp.concatenate([jnp.cos(ang), jnp.cos(ang)], axis=-1)
    ss = jnp.concatenate([jnp.sin(ang), jnp.sin(ang)], axis=-1)
    scale = hd_q ** -0.5 * np.log2(np.e)
    fq = jnp.concatenate([jnp.full((length, MLA_NOPE), scale, F32), cc * scale, ss * scale], axis=-1)
    fk = jnp.concatenate([cc, cc, ss, ss], axis=-1)
    w_kr = w_in[:, q_rank + kv_rank:]
    w_in_ext = jnp.concatenate([w_in[:, :q_rank + kv_rank], w_kr, w_kr, _rot_cols(w_kr), _rot_cols(w_kr)],
                               axis=-1).astype(BF16)
    wq = w_qb.reshape(q_rank, MLA_HEADS, hd_q)
    wq_ext = jnp.concatenate([wq, _rot_cols(wq[..., MLA_NOPE:])], axis=-1).reshape(
        q_rank, MLA_HEADS * MLA_QK).astype(BF16)
    wkv = w_kvb.reshape(kv_rank, MLA_HEADS, MLA_NOPE + MLA_V)
    wkv_perm = jnp.concatenate([wkv[..., :MLA_NOPE].reshape(kv_rank, -1),
                                wkv[..., MLA_NOPE:].reshape(kv_rank, -1)], axis=-1).astype(BF16)
    n_in = w_in_ext.shape[1]
    assert length % tm == 0 and length % tq == 0
    q, k, v = pl.pallas_call(
        functools.partial(_mla_in_body, q_rank=q_rank, kv_rank=kv_rank),
        grid=(bsz, length // tm),
        in_specs=[pl.BlockSpec((1, tm, d), lambda b, i: (b, i, 0)),
                  pl.BlockSpec((tm, MLA_QK), lambda b, i: (i, 0)),
                  pl.BlockSpec((tm, MLA_QK), lambda b, i: (i, 0)),
                  _const_spec((1, d)), _const_spec((d, n_in)), _const_spec((1, q_rank)),
                  _const_spec((q_rank, MLA_HEADS * MLA_QK)), _const_spec((1, kv_rank)),
                  _const_spec((kv_rank, MLA_HEADS * (MLA_NOPE + MLA_V)))],
        out_specs=[pl.BlockSpec((1, tm, MLA_HEADS * MLA_QK), lambda b, i: (b, i, 0)),
                   pl.BlockSpec((1, tm, MLA_HEADS * MLA_QK), lambda b, i: (b, i, 0)),
                   pl.BlockSpec((1, tm, MLA_HEADS * MLA_V), lambda b, i: (b, i, 0))],
        out_shape=[jax.ShapeDtypeStruct((bsz, length, MLA_HEADS * MLA_QK), BF16),
                   jax.ShapeDtypeStruct((bsz, length, MLA_HEADS * MLA_QK), BF16),
                   jax.ShapeDtypeStruct((bsz, length, MLA_HEADS * MLA_V), BF16)],
        compiler_params=_params(("parallel", "parallel")),
        name="mla_in",
    )(h3, fq, fk, _row(g_pre), w_in_ext, _row(q_norm), wq_ext, _row(kv_norm), wkv_perm)
    o = pl.pallas_call(
        functools.partial(_mla_attn_body, q_sub=q_sub, kv_chunk=kv_chunk),
        grid=(bsz, MLA_HEADS, length // tq),
        in_specs=[pl.BlockSpec((1, tq, MLA_QK), lambda b, hd, i: (b, i, hd)),
                  pl.BlockSpec((1, length, MLA_QK), lambda b, hd, i: (b, 0, hd)),
                  pl.BlockSpec((1, length, MLA_V), lambda b, hd, i: (b, 0, hd))],
        out_specs=pl.BlockSpec((1, tq, MLA_V), lambda b, hd, i: (b, i, hd)),
        out_shape=jax.ShapeDtypeStruct((bsz, length, MLA_HEADS * MLA_V), BF16),
        compiler_params=_params(("parallel", "parallel", "parallel")),
        name="mla_attn",
    )(q, k, v)
    width = MLA_HEADS * MLA_V
    return (lambda o_ref: o_ref[...]), [(o.reshape(bsz * length, width), width)], [], tail_tm


def _ssd_in_body(h_ref, gpre_ref, wz_ref, wx_ref, wdt_ref, z_ref, xbc_ref, dt_ref):
    u = _rms(h_ref[...], gpre_ref[...]).astype(BF16)
    z_ref[...] = _dot(u, wz_ref[...])
    xbc_ref[...] = _dot(u, wx_ref[...])
    dt_ref[...] = _dot(u, wdt_ref[...])


def _ssd_conv_body(x_ref, w_ref, b_ref, o_ref):
    x = x_ref[0]
    length = x.shape[0]
    pad = SSM_CONV // 2
    t = lax.broadcasted_iota(jnp.int32, x.shape, 0)
    acc = x * w_ref[pad:pad + 1, :] + b_ref[...]
    for kk in range(SSM_CONV):
        off = kk - pad
        if off == 0:
            continue
        shifted = pltpu.roll(x, (-off) % length, axis=0)
        valid = (t >= -off) if off < 0 else (t < length - off)
        acc = acc + jnp.where(valid, shifted, 0.0) * w_ref[kk:kk + 1, :]
    o_ref[0] = _silu(acc)


def _ssd_direction(xbc, dt_raw, bias, a_neg, expand, state_ref, reverse, d_inner):
    q = xbc.shape[0]
    gn = SSM_GROUPS * SSM_STATE
    hpg = d_inner // SSM_HEAD_DIM // SSM_GROUPS
    xs = xbc[:, :d_inner]
    b_in = xbc[:, d_inner:d_inner + gn]
    c_in = xbc[:, d_inner + gn:]
    dt = _softplus(dt_raw + bias)
    a_dt = dt * a_neg
    ti = lax.broadcasted_iota(jnp.int32, (q, q), 0)
    si = lax.broadcasted_iota(jnp.int32, (q, q), 1)
    keep = (si >= ti) if reverse else (si <= ti)
    tri = jnp.where(keep, 1.0, 0.0).astype(BF16)
    cum = _dot_split(tri, a_dt, terms=3)
    cum_t = jnp.transpose(cum)
    edge = cum[0:1, :] if reverse else cum[q - 1:q, :]
    x_dt = xs * _split_dot(dt, expand)
    x_end = (xs * _split_dot(dt * jnp.exp(edge - cum), expand)).astype(BF16)
    scale_in = _split_dot(jnp.exp(cum), expand)
    decay_state = _split_dot(jnp.exp(edge), expand)
    x_dt_b = x_dt.astype(BF16)
    outs = []
    for g in range(SSM_GROUPS):
        bg = b_in[:, g * SSM_STATE:(g + 1) * SSM_STATE].astype(BF16)
        cg = c_in[:, g * SSM_STATE:(g + 1) * SSM_STATE].astype(BF16)
        cb = _dot_nt(cg, bg)
        cols = slice(g * hpg * SSM_HEAD_DIM, (g + 1) * hpg * SSM_HEAD_DIM)
        st = state_ref[:, cols]
        y_off = _dot(cg, st.astype(BF16)) * scale_in[:, cols]
        for j in range(hpg):
            hd = g * hpg + j
            seg = cum[:, hd:hd + 1] - cum_t[hd:hd + 1, :]
            m = (cb * jnp.exp(jnp.where(keep, seg, -jnp.inf))).astype(BF16)
            hc = slice(hd * SSM_HEAD_DIM, (hd + 1) * SSM_HEAD_DIM)
            y_diag = _dot(m, x_dt_b[:, hc])
            outs.append(y_diag + y_off[:, j * SSM_HEAD_DIM:(j + 1) * SSM_HEAD_DIM])
        state_ref[:, cols] = st * decay_state[:, cols] + _dot_tn(bg, x_end[:, cols])
    return jnp.concatenate(outs, axis=-1)


def _ssd_scan_body(xf_ref, dtf_ref, xb_ref, dtb_ref, bias_ref, alog_ref, exp_ref,
                   yf_ref, yb_ref, sf_ref, sb_ref, *, d_inner):
    @pl.when(pl.program_id(1) == 0)
    def _():
        sf_ref[...] = jnp.zeros(sf_ref.shape, F32)
        sb_ref[...] = jnp.zeros(sb_ref.shape, F32)

    nh = d_inner // SSM_HEAD_DIM
    a_neg = -jnp.exp(alog_ref[...])
    expand = exp_ref[...]
    yf_ref[0] = _ssd_direction(xf_ref[0], dtf_ref[0][:, :nh], bias_ref[0:1, :], a_neg[0:1, :], expand,
                               sf_ref, False, d_inner)
    yb_ref[0] = _ssd_direction(xb_ref[0], dtb_ref[0][:, nh:], bias_ref[1:2, :], a_neg[1:2, :], expand,
                               sb_ref, True, d_inner)


def _ssd_out(yf_ref, yb_ref, xs_ref, z_ref, dsk_ref, gn_ref):
    y = (yf_ref[...] + yb_ref[...] + xs_ref[...] * dsk_ref[...]) * _silu(z_ref[...])
    gw = y.shape[1] // SSM_GROUPS
    return jnp.concatenate(
        [_rms(y[:, g * gw:(g + 1) * gw], gn_ref[:, g * gw:(g + 1) * gw]).astype(BF16)
         for g in range(SSM_GROUPS)], axis=-1)


def _ssd_mixer(h3, g_pre, w_in, conv_w, conv_b, a_log, dt_bias, d_skip, gate_norm, *, tm=512, cb=256,
               tail_tm=256):
    bsz, length, d = h3.shape
    t = bsz * length
    nh = a_log.shape[-1]
    d_inner = nh * SSM_HEAD_DIM
    conv_dim = d_inner + 2 * SSM_GROUPS * SSM_STATE
    q = SSM_CHUNK
    nc = length // q
    assert t % tm == 0 and length % q == 0 and conv_dim % cb == 0
    row = lambda c: pl.BlockSpec((tm, c), lambda i: (i, 0))
    z, xbc, dt = pl.pallas_call(
        _ssd_in_body,
        grid=(t // tm,),
        in_specs=[row(d), _const_spec((1, d)), _const_spec((d, d_inner)), _const_spec((d, conv_dim)),
                  _const_spec((d, 2 * nh))],
        out_specs=[row(d_inner), row(conv_dim), row(2 * nh)],
        out_shape=[jax.ShapeDtypeStruct((t, d_inner), F32), jax.ShapeDtypeStruct((t, conv_dim), F32),
                   jax.ShapeDtypeStruct((t, 2 * nh), F32)],
        compiler_params=_params(("parallel",)),
        name="ssd_in",
    )(h3.reshape(t, d), _row(g_pre), w_in[:, :d_inner].astype(BF16),
      w_in[:, d_inner:d_inner + conv_dim].astype(BF16), w_in[:, d_inner + conv_dim:].astype(BF16))
    xbc = pl.pallas_call(
        _ssd_conv_body,
        grid=(bsz, conv_dim // cb),
        in_specs=[pl.BlockSpec((1, length, cb), lambda b, j: (b, 0, j)),
                  pl.BlockSpec((SSM_CONV, cb), lambda b, j: (0, j)),
                  pl.BlockSpec((1, cb), lambda b, j: (0, j))],
        out_specs=pl.BlockSpec((1, length, cb), lambda b, j: (b, 0, j)),
        out_shape=jax.ShapeDtypeStruct((bsz, length, conv_dim), F32),
        compiler_params=_params(("parallel", "parallel")),
        name="ssd_conv",
    )(xbc.reshape(bsz, length, conv_dim), conv_w.astype(F32), _row(conv_b))
    dt3 = dt.reshape(bsz, length, 2 * nh)
    expand = jnp.repeat(jnp.eye(nh, dtype=BF16), SSM_HEAD_DIM, axis=1)
    fwd = lambda c: pl.BlockSpec((1, q, c), lambda b, i: (b, i, 0))
    bwd = lambda c: pl.BlockSpec((1, q, c), lambda b, i: (b, nc - 1 - i, 0))
    y_f, y_b = pl.pallas_call(
        functools.partial(_ssd_scan_body, d_inner=d_inner),
        grid=(bsz, nc),
        in_specs=[fwd(conv_dim), fwd(2 * nh), bwd(conv_dim), bwd(2 * nh),
                  _const_spec((2, nh)), _const_spec((2, nh)), _const_spec((nh, d_inner))],
        out_specs=[fwd(d_inner), bwd(d_inner)],
        out_shape=[jax.ShapeDtypeStruct((bsz, length, d_inner), F32)] * 2,
        scratch_shapes=[pltpu.VMEM((SSM_STATE, d_inner), F32)] * 2,
        compiler_params=_params(("parallel", "arbitrary")),
        name="ssd_scan",
    )(xbc, dt3, xbc, dt3, dt_bias.astype(F32), a_log.astype(F32), expand)
    row_inputs = [(y_f.reshape(t, d_inner), d_inner), (y_b.reshape(t, d_inner), d_inner),
                  (xbc.reshape(t, conv_dim), d_inner), (z, d_inner)]
    consts = [_row(jnp.repeat(d_skip.astype(F32), SSM_HEAD_DIM)), _row(gate_norm)]
    return _ssd_out, row_inputs, consts, tail_tm


def _rwkv_in_body(h_ref, hp_ref, hn_ref, gpre_ref, mu_ref, wr_ref, wk_ref, wv_ref, w0_ref, w1_ref,
                  w2f_ref, w2b_ref, a0_ref, a1_ref, a2_ref, g1_ref, g2_ref, kk_ref, ka_ref,
                  rk_ref, seg_ref, segt_ref, tri_ref, up_ref,
                  v_out, bv_out, g_out, pendf_out, pendb_out, *dir_outs):
    i = pl.program_id(1)
    n_i = pl.num_programs(1)
    gpre = gpre_ref[...]
    x = _rms(h_ref[0], gpre)
    tm = x.shape[0]
    prev_row = _rms(hp_ref[0][SUBLANES - 1:SUBLANES, :], gpre) * jnp.where(i > 0, 1.0, 0.0)
    next_row = _rms(hn_ref[0][0:1, :], gpre) * jnp.where(i < n_i - 1, 1.0, 0.0)
    t = lax.broadcasted_iota(jnp.int32, x.shape, 0)
    x_prev = jnp.where(t == 0, prev_row, pltpu.roll(x, 1, axis=0))
    x_next = jnp.where(t == tm - 1, next_row, pltpu.roll(x, tm - 1, axis=0))
    xx = 0.5 * (x_prev + x_next) - x
    mix = lambda j: (x + xx * mu_ref[j:j + 1, :]).astype(BF16)
    xr, xw, xk, xv, xa, xg = (mix(j) for j in range(6))
    r = _dot(xr, wr_ref[...])
    k = _dot(xk, wk_ref[...])
    v = _dot(xv, wv_ref[...])
    lora_w = jnp.tanh(_dot(xw, w1_ref[...])).astype(BF16)
    log_decay = lambda w: -np.float32(np.exp(-0.5)) * jax.nn.sigmoid(w)
    lw_f = log_decay(w0_ref[0:1, :] + _dot(lora_w, w2f_ref[...]))
    lw_b = log_decay(w0_ref[1:2, :] + _dot(lora_w, w2b_ref[...]))
    a = jax.nn.sigmoid(a0_ref[...] + _dot(_dot(xa, a1_ref[...]).astype(BF16), a2_ref[...]))
    g_out[0] = _dot(jax.nn.sigmoid(_dot(xg, g1_ref[...])).astype(BF16), g2_ref[...])
    kk = k * kk_ref[...]
    norm = jnp.maximum(jnp.sqrt(_split_dot(kk * kk, seg_ref[...])), 1e-12)
    kk = kk * _split_dot(1.0 / norm, segt_ref[...])
    k = k * (1.0 + (a - 1.0) * ka_ref[...])
    rm = -kk
    add = kk * a
    v_out[0] = v.astype(v_out.dtype)
    bonus = _split_dot(_split_dot(r * k * rk_ref[...], seg_ref[...]), segt_ref[...])
    bv_out[0] = bonus * v
    tri, up = tri_ref[...], up_ref[...]
    c = RWKV_CHUNK
    for lw, cum, rest, pend_out, outs in (
            (lw_f, _dot_split(tri, lw_f), _dot_split(up, lw_f), pendf_out, dir_outs[:6]),
            (lw_b, lw_b + _dot_split(up, lw_b), _dot_split(tri, lw_b) - lw_b, pendb_out, dir_outs[6:])):
        e_pos = jnp.exp(cum)
        e_neg = jnp.exp(-cum)
        e_end = jnp.exp(rest)
        values = (r * e_pos, rm * jnp.exp(cum - lw), add * e_neg, k * e_neg, add * e_end, k * e_end)
        for o_ref, val in zip(outs, values):
            o_ref[0] = val.astype(o_ref.dtype)
        total = cum + rest
        for m in range(tm // c):
            pend_out[0, m] = jnp.exp(total[m * c:m * c + 1, :])


def _rwkv_chunk(r_t, rm_t, add_t, k_t, add_h, k_h, v_b, p_end, s_all, n_forward):
    c = r_t.shape[0]
    hd = RWKV_HEAD_DIM
    pair = 2 * hd
    assert c == hd and pair == LANES
    lane = lax.broadcasted_iota(jnp.int32, (c, pair), 1)
    ti = lax.broadcasted_iota(jnp.int32, (c, pair), 0)
    si = lane & (hd - 1)
    left = lane < hd
    keep_l = jnp.where(left, 1.0, 0.0).astype(BF16)
    keep_r = jnp.where(left, 0.0, 1.0).astype(BF16)
    bd = lambda x: jnp.concatenate([x * keep_l, x * keep_r], axis=0)
    eye = jnp.where(si == ti, 1.0, 0.0).astype(F32)
    pairs = range(r_t.shape[1] // pair)
    incl_of = lambda j: (si <= ti) if j < n_forward else (si >= ti)
    strict_of = lambda j: (si < ti) if j < n_forward else (si > ti)
    col = lambda x, j: x[:, j * pair:(j + 1) * pair]
    rows = lambda *xs: jnp.concatenate(xs, axis=0)
    v_bd = [bd(col(v_b, j)) for j in pairs]
    s0 = [col(s_all, j) for j in pairs]
    s0_bd = [bd(s.astype(BF16)) for s in s0]
    sc = [_dot_nt(rows(col(rm_t, j), col(r_t, j)), rows(bd(col(add_t, j)), bd(col(k_t, j)))) for j in pairs]
    n_mat = [jnp.where(strict_of(j), sc[j][:c, :pair], 0.0) for j in pairs]
    a_ak = [jnp.where(strict_of(j), sc[j][:c, pair:], 0.0).astype(BF16) for j in pairs]
    a_rb = [jnp.where(incl_of(j), sc[j][c:, :pair], 0.0).astype(BF16) for j in pairs]
    a_rk = [jnp.where(incl_of(j), sc[j][c:, pair:], 0.0).astype(BF16) for j in pairs]
    xkv = [_dot(rows(a_ak[j], a_rk[j]), v_bd[j]) for j in pairs]
    rs = [_dot_nt(col(r_t, j), s0_bd[j]) for j in pairs]
    inv = [eye + n_mat[j] for j in pairs]
    pw = [n_mat[j].astype(BF16) for j in pairs]
    pw = [_dot(pw[j], bd(pw[j])).astype(BF16) for j in pairs]
    span = 2
    while span < c:
        if 2 * span < c:
            both = [_dot(rows(pw[j], inv[j].astype(BF16)), bd(pw[j])) for j in pairs]
            pw = [both[j][:c].astype(BF16) for j in pairs]
            inv = [inv[j] + both[j][c:] for j in pairs]
        else:
            inv = [inv[j] + _dot(inv[j].astype(BF16), bd(pw[j])) for j in pairs]
        span *= 2
    wu = [_dot(inv[j].astype(BF16),
               jnp.concatenate([bd(col(rm_t, j)), bd(xkv[j][:c].astype(BF16))], axis=1)) for j in pairs]
    sa_b = [(_dot_nt(wu[j][:, :pair].astype(BF16), s0_bd[j]) + wu[j][:, pair:]).astype(BF16)
            for j in pairs]
    outs = [rs[j] + _dot(a_rb[j], bd(sa_b[j])) + xkv[j][c:] for j in pairs]
    full = [_dot_tn(rows(sa_b[j], col(v_b, j)), rows(col(add_h, j), col(k_h, j))) for j in pairs]
    states = [s0[j] * col(p_end, j) + jnp.where(left, full[j][:hd], full[j][hd:]) for j in pairs]
    return jnp.concatenate(outs, axis=-1), jnp.concatenate(states, axis=-1)


def _rwkv_scan_body(*refs):
    f_in, b_in = refs[:8], refs[8:16]
    yf_ref, yb_ref, sf_ref, sb_ref = refs[16:]

    @pl.when(pl.program_id(2) == 0)
    def _():
        sf_ref[...] = jnp.zeros(sf_ref.shape, F32)
        sb_ref[...] = jnp.zeros(sb_ref.shape, F32)

    width = sf_ref.shape[1]
    both = lambda f, b: jnp.concatenate([f, b], axis=-1)
    ops = [both(f[0], b[0]) for f, b in zip(f_in[:7], b_in[:7])]
    y, s_new = _rwkv_chunk(*ops, both(f_in[7][0, 0], b_in[7][0, 0]), both(sf_ref[...], sb_ref[...]),
                           width // LANES)
    yf_ref[0] = y[:, :width]
    yb_ref[0] = y[:, width:]
    sf_ref[...] = s_new[:, :width]
    sb_ref[...] = s_new[:, width:]


def _rwkv_out(yf_ref, yb_ref, bv_ref, g_ref, lnw_ref, lnb_ref, seg_ref, segt_ref):
    seg, segt = seg_ref[...], segt_ref[...]
    head_mean = lambda x: _split_dot(_split_dot(x, seg) * (1.0 / RWKV_HEAD_DIM), segt)
    y = yf_ref[...] + yb_ref[...]
    cen = y - head_mean(y)
    var = head_mean(cen * cen)
    y = cen * lax.rsqrt(var + RWKV_LN_EPS) * lnw_ref[...] + lnb_ref[...]
    return ((y + bv_ref[...]) * g_ref[...]).astype(BF16)


def _pad_to(w, axis, size):
    pad = [(0, 0)] * w.ndim
    pad[axis] = (0, size - w.shape[axis])
    return jnp.pad(w, pad)


def _rwkv_mixer(h3, g_pre, mu, w_rkv, w0, w1, w2, a0, a1, a2, g1, g2, k_k, k_a, r_k, ln_w, ln_b, *,
                tm=256, heads_per_step=16, tail_tm=512):
    bsz, length, d = h3.shape
    t = bsz * length
    n_heads = d // RWKV_HEAD_DIM
    c = RWKV_CHUNK
    nc = length // c
    hw = heads_per_step * RWKV_HEAD_DIM
    assert length % tm == 0 and length % c == 0 and n_heads % heads_per_step == 0
    assert n_heads <= LANES
    rank_w = w1.shape[-1]
    w1_cat = jnp.concatenate([w1[0], w1[1]], axis=-1).astype(BF16)
    w2f = jnp.concatenate([w2[0], jnp.zeros_like(w2[1])], axis=0).astype(BF16)
    w2b = jnp.concatenate([jnp.zeros_like(w2[0]), w2[1]], axis=0).astype(BF16)
    a1p = _pad_to(a1, 1, LANES).astype(BF16)
    a2p = _pad_to(a2, 0, LANES).astype(BF16)
    g_rank = -(-g1.shape[1] // LANES) * LANES
    g1p = _pad_to(g1, 1, g_rank).astype(BF16)
    g2p = _pad_to(g2, 0, g_rank).astype(BF16)
    seg = _pad_to(jnp.repeat(jnp.eye(n_heads, dtype=BF16), RWKV_HEAD_DIM, axis=0), 1, LANES)
    segt = seg.T
    hb = tm // SUBLANES
    nhb = length // SUBLANES
    pos = np.arange(tm)
    same_chunk = (pos[:, None] // c) == (pos[None, :] // c)
    tri = jnp.asarray(same_chunk & (pos[None, :] <= pos[:, None]), BF16)
    up = jnp.asarray(same_chunk & (pos[None, :] > pos[:, None]), BF16)
    cur = pl.BlockSpec((1, tm, d), lambda b, i: (b, i, 0))
    pend_spec = pl.BlockSpec((1, tm // c, 1, d), lambda b, i: (b, i, 0, 0))
    pend_shape = jax.ShapeDtypeStruct((bsz, nc, 1, d), F32)
    act = jax.ShapeDtypeStruct((bsz, length, d), BF16)
    outs = pl.pallas_call(
        _rwkv_in_body,
        grid=(bsz, length // tm),
        in_specs=[cur,
                  pl.BlockSpec((1, SUBLANES, d), lambda b, i: (b, jnp.maximum(i * hb - 1, 0), 0)),
                  pl.BlockSpec((1, SUBLANES, d), lambda b, i: (b, jnp.minimum((i + 1) * hb, nhb - 1), 0)),
                  _const_spec((1, d)), _const_spec((6, d)),
                  _const_spec((d, d)), _const_spec((d, d)), _const_spec((d, d)),
                  _const_spec((2, d)), _const_spec((d, 2 * rank_w)),
                  _const_spec((2 * rank_w, d)), _const_spec((2 * rank_w, d)),
                  _const_spec((1, d)), _const_spec((d, LANES)), _const_spec((LANES, d)),
                  _const_spec((d, g_rank)), _const_spec((g_rank, d)),
                  _const_spec((1, d)), _const_spec((1, d)), _const_spec((1, d)),
                  _const_spec((d, LANES)), _const_spec((LANES, d)),
                  _const_spec((tm, tm)), _const_spec((tm, tm))],
        out_specs=[cur, cur, cur, pend_spec, pend_spec] + [cur] * 12,
        out_shape=[act, jax.ShapeDtypeStruct((bsz, length, d), F32),
                   jax.ShapeDtypeStruct((bsz, length, d), F32), pend_shape, pend_shape] + [act] * 12,
        compiler_params=_params(("parallel", "parallel")),
        name="rwkv_in",
    )(h3, h3, h3, _row(g_pre), mu.astype(F32), w_rkv[0].astype(BF16), w_rkv[1].astype(BF16),
      w_rkv[2].astype(BF16), w0.astype(F32), w1_cat, w2f, w2b, _row(a0), a1p, a2p, g1p, g2p,
      _row(k_k), _row(k_a), _row(r_k), seg, segt, tri, up)
    v, bv, g, pend_f, pend_b = outs[:5]
    ops_f, ops_b = outs[5:11], outs[11:]
    fwd = pl.BlockSpec((1, c, hw), lambda b, hg, i: (b, i, hg))
    bwd = pl.BlockSpec((1, c, hw), lambda b, hg, i: (b, nc - 1 - i, hg))
    fwd_p = pl.BlockSpec((1, 1, 1, hw), lambda b, hg, i: (b, i, 0, hg))
    bwd_p = pl.BlockSpec((1, 1, 1, hw), lambda b, hg, i: (b, nc - 1 - i, 0, hg))
    y_f, y_b = pl.pallas_call(
        _rwkv_scan_body,
        grid=(bsz, n_heads // heads_per_step, nc),
        in_specs=[fwd] * 7 + [fwd_p] + [bwd] * 7 + [bwd_p],
        out_specs=[fwd, bwd],
        out_shape=[jax.ShapeDtypeStruct((bsz, length, d), F32)] * 2,
        scratch_shapes=[pltpu.VMEM((RWKV_HEAD_DIM, hw), F32)] * 2,
        compiler_params=_params(("parallel", "parallel", "arbitrary")),
        name="rwkv_scan",
    )(*ops_f, v, pend_f, *ops_b, v, pend_b)
    row_inputs = [(x.reshape(t, d), d) for x in (y_f, y_b, bv, g)]
    return _rwkv_out, row_inputs, [_row(ln_w), _row(ln_b), seg, segt], tail_tm


def kernel(x_prompt, x_sample, mix_pre_norm, mix_post_norm, ffn_pre_norm, ffn_post_norm, mla_w_in, mla_q_norm, mla_w_qb, mla_kv_norm, mla_w_kvb, mla_w_o, ssm_w_in, ssm_conv_w, ssm_conv_b, ssm_a_log, ssm_dt_bias, ssm_d_skip, ssm_gate_norm, ssm_w_out, rwkv_mu, rwkv_w_rkv, rwkv_w0, rwkv_w1, rwkv_w2, rwkv_a0, rwkv_a1, rwkv_a2, rwkv_g1, rwkv_g2, rwkv_k_k, rwkv_k_a, rwkv_r_k, rwkv_ln_w, rwkv_ln_b, rwkv_w_o, ffn_w_in, ffn_w_out):
    depth = mix_pre_norm.shape[0]

    def trunk(x):
        bsz, length, d = x.shape
        h = x.reshape(bsz * length, d)
        for i in range(depth):
            kind, j = i % N_MIXERS, i // N_MIXERS
            h3 = h.reshape(bsz, length, d)
            if kind == 0:
                mixer = _mla_mixer(h3, mix_pre_norm[i], mla_w_in[j], mla_q_norm[j], mla_w_qb[j],
                                   mla_kv_norm[j], mla_w_kvb[j])
                w_o = mla_w_o[j]
            elif kind == 1:
                mixer = _ssd_mixer(h3, mix_pre_norm[i], ssm_w_in[j], ssm_conv_w[j], ssm_conv_b[j],
                                   ssm_a_log[j], ssm_dt_bias[j], ssm_d_skip[j], ssm_gate_norm[j])
                w_o = ssm_w_out[j]
            else:
                mixer = _rwkv_mixer(h3, mix_pre_norm[i], rwkv_mu[j], rwkv_w_rkv[j], rwkv_w0[j],
                                    rwkv_w1[j], rwkv_w2[j], rwkv_a0[j], rwkv_a1[j], rwkv_a2[j],
                                    rwkv_g1[j], rwkv_g2[j], rwkv_k_k[j], rwkv_k_a[j], rwkv_r_k[j],
                                    rwkv_ln_w[j], rwkv_ln_b[j])
                w_o = rwkv_w_o[j]
            h = _layer_tail(h, mixer, w_o, mix_post_norm[i], ffn_pre_norm[i], ffn_w_in[i],
                            ffn_w_out[i], ffn_post_norm[i])
        return h.reshape(bsz, length, d)

    return (trunk(x_prompt), trunk(x_sample))
```

```python
import functools

import jax
import jax.numpy as jnp
import numpy as np
from jax import lax
from jax.experimental import pallas as pl
from jax.experimental.pallas import tpu as pltpu

F32 = jnp.float32
BF16 = jnp.bfloat16

NORM_EPS = 1e-6
N_MIXERS = 3

VMEM_LIMIT_BYTES = 56 * 1024 * 1024
LANES = 128
SUBLANES = 8

MLA_HEADS = 8
MLA_NOPE = 128
MLA_ROPE = 64
MLA_V = 128
MLA_QK = MLA_NOPE + 2 * MLA_ROPE
ROPE_THETA = 10000.0

SSM_HEAD_DIM = 64
SSM_GROUPS = 4
SSM_STATE = 128
SSM_CONV = 5
SSM_CHUNK = 128

RWKV_HEAD_DIM = 64
RWKV_LN_EPS = 64e-5
RWKV_CHUNK = 64


def _rms(x, g, eps=NORM_EPS):
    return x * lax.rsqrt(jnp.mean(x * x, axis=-1, keepdims=True) + eps) * g


def _dot(a, b):
    return jnp.dot(a, b, preferred_element_type=F32)


def _dot_nt(a, b):
    return lax.dot_general(a, b, (((1,), (1,)), ((), ())), preferred_element_type=F32)


def _dot_tn(a, b):
    return lax.dot_general(a, b, (((0,), (0,)), ((), ())), preferred_element_type=F32)


def _bf16_terms(x, terms):
    pieces = []
    for _ in range(terms):
        p = x.astype(BF16)
        pieces.append(p)
        x = x - p.astype(F32)
    return pieces


def _dot_split(m, x, terms=2):
    return sum(_dot(m, p) for p in _bf16_terms(x, terms))


def _split_dot(x, m, terms=2):
    return sum(_dot(p, m) for p in _bf16_terms(x, terms))


def _silu(x):
    return x * jax.nn.sigmoid(x)


def _softplus(x):
    return jnp.maximum(x, 0.0) + jnp.log(1.0 + jnp.exp(-jnp.abs(x)))


def _const_spec(shape):
    nd = len(shape)
    return pl.BlockSpec(shape, lambda *_: (0,) * nd, pipeline_mode=pl.Buffered(1))


def _params(semantics):
    return pltpu.CompilerParams(dimension_semantics=semantics, vmem_limit_bytes=VMEM_LIMIT_BYTES)


def _row(v):
    return v.reshape(1, -1).astype(F32)


def _tail_body(*refs, n_mixer_refs, mixer_out, ff_chunk):
    h_ref = refs[0]
    y = mixer_out(*refs[1:1 + n_mixer_refs])
    wo_ref, gpost_ref, gpre_ref, win_ref, wout_ref, gfpost_ref, o_ref = refs[1 + n_mixer_refs:]
    d_ff = wout_ref.shape[0]
    h1 = h_ref[...] + _rms(_dot(y, wo_ref[...]), gpost_ref[...])
    u = _rms(h1, gpre_ref[...]).astype(BF16)
    acc = jnp.zeros(h1.shape, F32)
    for c in range(d_ff // ff_chunk):
        lo = c * ff_chunk
        gate = _dot(u, win_ref[:, lo:lo + ff_chunk])
        up = _dot(u, win_ref[:, d_ff + lo:d_ff + lo + ff_chunk])
        act = (_silu(gate) * up).astype(BF16)
        acc = acc + _dot(act, wout_ref[lo:lo + ff_chunk, :])
    o_ref[...] = h1 + _rms(acc, gfpost_ref[...])


def _layer_tail(h, mixer, w_o, g_post, g_pre, w_in, w_out, g_fpost, *, ff_chunk=256):
    mixer_out, row_inputs, const_inputs, tm = mixer
    t, d = h.shape
    ky = w_o.shape[0]
    d_ff = w_out.shape[0]
    assert t % tm == 0 and d_ff % ff_chunk == 0 and ff_chunk % LANES == 0
    row = lambda c: pl.BlockSpec((tm, c), lambda i: (i, 0))
    return pl.pallas_call(
        functools.partial(_tail_body, n_mixer_refs=len(row_inputs) + len(const_inputs),
                          mixer_out=mixer_out, ff_chunk=ff_chunk),
        grid=(t // tm,),
        in_specs=[row(d)] + [row(width) for _, width in row_inputs]
        + [_const_spec(c.shape) for c in const_inputs]
        + [_const_spec((ky, d)), _const_spec((1, d)), _const_spec((1, d)),
           _const_spec((d, 2 * d_ff)), _const_spec((d_ff, d)), _const_spec((1, d))],
        out_specs=row(d),
        out_shape=jax.ShapeDtypeStruct((t, d), F32),
        compiler_params=_params(("parallel",)),
        name="layer_tail",
    )(h, *[a for a, _ in row_inputs], *const_inputs, w_o.astype(BF16), _row(g_post), _row(g_pre),
      w_in.astype(BF16), w_out.astype(BF16), _row(g_fpost))


def _mla_in_body(h_ref, fq_ref, fk_ref, gpre_ref, win_ref, qn_ref, wq_ref, kvn_ref, wkv_ref,
                 q_ref, k_ref, v_ref, *, q_rank, kv_rank):
    u = _rms(h_ref[0], gpre_ref[...]).astype(BF16)
    lat = _dot(u, win_ref[...])
    q_lat = _rms(lat[:, :q_rank], qn_ref[...]).astype(BF16)
    kv_lat = _rms(lat[:, q_rank:q_rank + kv_rank], kvn_ref[...]).astype(BF16)
    kr = lat[:, q_rank + kv_rank:q_rank + kv_rank + LANES]
    kr_rot = lat[:, q_rank + kv_rank + LANES:q_rank + kv_rank + 2 * LANES]
    fk = fk_ref[...]
    k_rope = (kr * fk[:, :LANES] + kr_rot * fk[:, LANES:]).astype(BF16)
    fq = fq_ref[...]
    q = _dot(q_lat, wq_ref[...])
    kv = _dot(kv_lat, wkv_ref[...])
    for hd in range(MLA_HEADS):
        q_ref[0, :, hd * MLA_QK:(hd + 1) * MLA_QK] = (
            q[:, hd * MLA_QK:(hd + 1) * MLA_QK] * fq).astype(BF16)
        k_ref[0, :, hd * MLA_QK:hd * MLA_QK + MLA_NOPE] = (
            kv[:, hd * MLA_NOPE:(hd + 1) * MLA_NOPE].astype(BF16))
        k_ref[0, :, hd * MLA_QK + MLA_NOPE:(hd + 1) * MLA_QK] = k_rope
    v_ref[0] = jnp.transpose(kv[:, MLA_HEADS * MLA_NOPE:]).astype(BF16)


def _mla_attn_body(q_ref, k_ref, vt_ref, o_ref, *, q_sub, kv_chunk):
    n_chunks = k_ref.shape[1] // kv_chunk
    n_sub = q_ref.shape[1] // q_sub
    kv_rows = lambda c: slice(c * kv_chunk, (c + 1) * kv_chunk)
    score = lambda t: _dot_nt(k_ref[0], q_ref[0, t * q_sub:(t + 1) * q_sub, :])
    s_cur = score(0)
    for t in range(n_sub):
        s_next = score(t + 1) if t + 1 < n_sub else None
        m = jnp.max(s_cur, axis=0, keepdims=True)
        l = jnp.zeros((1, q_sub), F32)
        acc = jnp.zeros((vt_ref.shape[1], q_sub), F32)
        for c in range(n_chunks):
            p = jnp.exp2(s_cur[kv_rows(c), :] - m)
            l = l + jnp.sum(p, axis=0, keepdims=True)
            acc = acc + _dot(vt_ref[0, :, kv_rows(c)], p.astype(BF16))
        o_ref[0, t * q_sub:(t + 1) * q_sub, :] = jnp.transpose(acc / l).astype(o_ref.dtype)
        s_cur = s_next


def _rot_cols(w):
    half = w.shape[-1] // 2
    return jnp.concatenate([-w[..., half:], w[..., :half]], axis=-1)


def _mla_mixer(h3, g_pre, w_in, q_norm, w_qb, kv_norm, w_kvb, *, tm=512, tq=2048, q_sub=256,
               kv_chunk=256, tail_tm=512):
    bsz, length, d = h3.shape
    q_rank, kv_rank = q_norm.shape[0], kv_norm.shape[0]
    hd_q = MLA_NOPE + MLA_ROPE
    inv = 1.0 / (ROPE_THETA ** (jnp.arange(0, MLA_ROPE, 2, dtype=F32) / MLA_ROPE))
    ang = jnp.arange(length, dtype=F32)[:, None] * inv[None, :]
    cc = jnp.concatenate([jnp.cos(ang), jnp.cos(ang)], axis=-1)
    ss = jnp.concatenate([jnp.sin(ang), jnp.sin(ang)], axis=-1)
    scale = hd_q ** -0.5 * np.log2(np.e)
    fq = jnp.concatenate([jnp.full((length, MLA_NOPE), scale, F32), cc * scale, ss * scale], axis=-1)
    fk = jnp.concatenate([cc, cc, ss, ss], axis=-1)
    w_kr = w_in[:, q_rank + kv_rank:]
    w_in_ext = jnp.concatenate([w_in[:, :q_rank + kv_rank], w_kr, w_kr, _rot_cols(w_kr), _rot_cols(w_kr)],
                               axis=-1).astype(BF16)
    wq = w_qb.reshape(q_rank, MLA_HEADS, hd_q)
    wq_ext = jnp.concatenate([wq, _rot_cols(wq[..., MLA_NOPE:])], axis=-1).reshape(
        q_rank, MLA_HEADS * MLA_QK).astype(BF16)
    wkv = w_kvb.reshape(kv_rank, MLA_HEADS, MLA_NOPE + MLA_V)
    wkv_perm = jnp.concatenate([wkv[..., :MLA_NOPE].reshape(kv_rank, -1),
                                wkv[..., MLA_NOPE:].reshape(kv_rank, -1)], axis=-1).astype(BF16)
    n_in = w_in_ext.shape[1]
    assert length % tm == 0 and length % tq == 0
    q, k, v = pl.pallas_call(
        functools.partial(_mla_in_body, q_rank=q_rank, kv_rank=kv_rank),
        grid=(bsz, length // tm),
        in_specs=[pl.BlockSpec((1, tm, d), lambda b, i: (b, i, 0)),
                  pl.BlockSpec((tm, MLA_QK), lambda b, i: (i, 0)),
                  pl.BlockSpec((tm, MLA_QK), lambda b, i: (i, 0)),
                  _const_spec((1, d)), _const_spec((d, n_in)), _const_spec((1, q_rank)),
                  _const_spec((q_rank, MLA_HEADS * MLA_QK)), _const_spec((1, kv_rank)),
                  _const_spec((kv_rank, MLA_HEADS * (MLA_NOPE + MLA_V)))],
        out_specs=[pl.BlockSpec((1, tm, MLA_HEADS * MLA_QK), lambda b, i: (b, i, 0)),
                   pl.BlockSpec((1, tm, MLA_HEADS * MLA_QK), lambda b, i: (b, i, 0)),
                   pl.BlockSpec((1, MLA_HEADS * MLA_V, tm), lambda b, i: (b, 0, i))],
        out_shape=[jax.ShapeDtypeStruct((bsz, length, MLA_HEADS * MLA_QK), BF16),
                   jax.ShapeDtypeStruct((bsz, length, MLA_HEADS * MLA_QK), BF16),
                   jax.ShapeDtypeStruct((bsz, MLA_HEADS * MLA_V, length), BF16)],
        compiler_params=_params(("parallel", "parallel")),
        name="mla_in",
    )(h3, fq, fk, _row(g_pre), w_in_ext, _row(q_norm), wq_ext, _row(kv_norm), wkv_perm)
    o = pl.pallas_call(
        functools.partial(_mla_attn_body, q_sub=q_sub, kv_chunk=kv_chunk),
        grid=(bsz, MLA_HEADS, length // tq),
        in_specs=[pl.BlockSpec((1, tq, MLA_QK), lambda b, hd, i: (b, i, hd)),
                  pl.BlockSpec((1, length, MLA_QK), lambda b, hd, i: (b, 0, hd)),
                  pl.BlockSpec((1, MLA_V, length), lambda b, hd, i: (b, hd, 0))],
        out_specs=pl.BlockSpec((1, tq, MLA_V), lambda b, hd, i: (b, i, hd)),
        out_shape=jax.ShapeDtypeStruct((bsz, length, MLA_HEADS * MLA_V), BF16),
        compiler_params=_params(("parallel", "parallel", "parallel")),
        name="mla_attn",
    )(q, k, v)
    width = MLA_HEADS * MLA_V
    return (lambda o_ref: o_ref[...]), [(o.reshape(bsz * length, width), width)], [], tail_tm


def _ssd_in_body(h_ref, gpre_ref, wz_ref, wx_ref, wdt_ref, z_ref, xbc_ref, dt_ref):
    u = _rms(h_ref[...], gpre_ref[...]).astype(BF16)
    z_ref[...] = _dot(u, wz_ref[...])
    xbc_ref[...] = _dot(u, wx_ref[...])
    dt_ref[...] = _dot(u, wdt_ref[...])


def _ssd_conv_body(x_ref, w_ref, b_ref, o_ref):
    x = x_ref[0]
    length = x.shape[0]
    pad = SSM_CONV // 2
    t = lax.broadcasted_iota(jnp.int32, x.shape, 0)
    acc = x * w_ref[pad:pad + 1, :] + b_ref[...]
    for kk in range(SSM_CONV):
        off = kk - pad
        if off == 0:
            continue
        shifted = pltpu.roll(x, (-off) % length, axis=0)
        valid = (t >= -off) if off < 0 else (t < length - off)
        acc = acc + jnp.where(valid, shifted, 0.0) * w_ref[kk:kk + 1, :]
    o_ref[0] = _silu(acc)


def _ssd_direction(xbc, dt_raw, bias, a_neg, expand, state_ref, reverse, d_inner):
    q = xbc.shape[0]
    gn = SSM_GROUPS * SSM_STATE
    hpg = d_inner // SSM_HEAD_DIM // SSM_GROUPS
    xs = xbc[:, :d_inner]
    b_in = xbc[:, d_inner:d_inner + gn]
    c_in = xbc[:, d_inner + gn:]
    dt = _softplus(dt_raw + bias)
    a_dt = dt * a_neg
    ti = lax.broadcasted_iota(jnp.int32, (q, q), 0)
    si = lax.broadcasted_iota(jnp.int32, (q, q), 1)
    keep = (si >= ti) if reverse else (si <= ti)
    tri = jnp.where(keep, 1.0, 0.0).astype(BF16)
    cum = _dot_split(tri, a_dt, terms=3)
    cum_t = jnp.transpose(cum)
    dt_t = jnp.transpose(dt)
    edge = cum[0:1, :] if reverse else cum[q - 1:q, :]
    x_end = (xs * _split_dot(dt * jnp.exp(edge - cum), expand)).astype(BF16)
    scale_in = _split_dot(jnp.exp(cum), expand)
    decay_state = _split_dot(jnp.exp(edge), expand)
    xs_b = xs.astype(BF16)
    gw = hpg * SSM_HEAD_DIM
    groups = range(SSM_GROUPS)
    gcols = lambda g: slice(g * gw, (g + 1) * gw)
    bg = [b_in[:, g * SSM_STATE:(g + 1) * SSM_STATE].astype(BF16) for g in groups]
    cg = [c_in[:, g * SSM_STATE:(g + 1) * SSM_STATE].astype(BF16) for g in groups]
    st = [state_ref[:, gcols(g)] for g in groups]
    cb = [_dot_nt(cg[g], bg[g]) for g in groups]
    y_off = [_dot(cg[g], st[g].astype(BF16)) * scale_in[:, gcols(g)] for g in groups]
    lane = lax.broadcasted_iota(jnp.int32, (q, 2 * SSM_HEAD_DIM), 1)
    keep_l = jnp.where(lane < SSM_HEAD_DIM, 1.0, 0.0).astype(BF16)
    keep_r = jnp.where(lane < SSM_HEAD_DIM, 0.0, 1.0).astype(BF16)
    outs = []
    for pr in range(d_inner // (2 * SSM_HEAD_DIM)):
        m_pair = []
        for hd in (2 * pr, 2 * pr + 1):
            seg = cum[:, hd:hd + 1] - cum_t[hd:hd + 1, :]
            decay = jnp.exp(jnp.where(keep, seg, -jnp.inf))
            m_pair.append((cb[hd // hpg] * decay * dt_t[hd:hd + 1, :]).astype(BF16))
        x_pair = xs_b[:, pr * 2 * SSM_HEAD_DIM:(pr + 1) * 2 * SSM_HEAD_DIM]
        x_bd = jnp.concatenate([x_pair * keep_l, x_pair * keep_r], axis=0)
        outs.append(_dot(jnp.concatenate(m_pair, axis=1), x_bd))
    y = jnp.concatenate(outs, axis=-1) + jnp.concatenate(y_off, axis=-1)
    new_state = [st[g] * decay_state[:, gcols(g)] + _dot_tn(bg[g], x_end[:, gcols(g)]) for g in groups]
    for g in groups:
        state_ref[:, gcols(g)] = new_state[g]
    return y


def _ssd_scan_body(xf_ref, dtf_ref, xb_ref, dtb_ref, bias_ref, alog_ref, exp_ref,
                   yf_ref, yb_ref, sf_ref, sb_ref, *, d_inner):
    @pl.when(pl.program_id(1) == 0)
    def _():
        sf_ref[...] = jnp.zeros(sf_ref.shape, F32)
        sb_ref[...] = jnp.zeros(sb_ref.shape, F32)

    nh = d_inner // SSM_HEAD_DIM
    a_neg = -jnp.exp(alog_ref[...])
    expand = exp_ref[...]
    yf_ref[0] = _ssd_direction(xf_ref[0], dtf_ref[0][:, :nh], bias_ref[0:1, :], a_neg[0:1, :], expand,
                               sf_ref, False, d_inner)
    yb_ref[0] = _ssd_direction(xb_ref[0], dtb_ref[0][:, nh:], bias_ref[1:2, :], a_neg[1:2, :], expand,
                               sb_ref, True, d_inner)


def _ssd_out(yf_ref, yb_ref, xs_ref, z_ref, dsk_ref, gn_ref):
    y = (yf_ref[...] + yb_ref[...] + xs_ref[...] * dsk_ref[...]) * _silu(z_ref[...])
    gw = y.shape[1] // SSM_GROUPS
    return jnp.concatenate(
        [_rms(y[:, g * gw:(g + 1) * gw], gn_ref[:, g * gw:(g + 1) * gw]).astype(BF16)
         for g in range(SSM_GROUPS)], axis=-1)


def _ssd_mixer(h3, g_pre, w_in, conv_w, conv_b, a_log, dt_bias, d_skip, gate_norm, *, tm=512, cb=256,
               tail_tm=256):
    bsz, length, d = h3.shape
    t = bsz * length
    nh = a_log.shape[-1]
    d_inner = nh * SSM_HEAD_DIM
    conv_dim = d_inner + 2 * SSM_GROUPS * SSM_STATE
    q = SSM_CHUNK
    nc = length // q
    assert t % tm == 0 and length % q == 0 and conv_dim % cb == 0
    row = lambda c: pl.BlockSpec((tm, c), lambda i: (i, 0))
    z, xbc, dt = pl.pallas_call(
        _ssd_in_body,
        grid=(t // tm,),
        in_specs=[row(d), _const_spec((1, d)), _const_spec((d, d_inner)), _const_spec((d, conv_dim)),
                  _const_spec((d, 2 * nh))],
        out_specs=[row(d_inner), row(conv_dim), row(2 * nh)],
        out_shape=[jax.ShapeDtypeStruct((t, d_inner), F32), jax.ShapeDtypeStruct((t, conv_dim), F32),
                   jax.ShapeDtypeStruct((t, 2 * nh), F32)],
        compiler_params=_params(("parallel",)),
        name="ssd_in",
    )(h3.reshape(t, d), _row(g_pre), w_in[:, :d_inner].astype(BF16),
      w_in[:, d_inner:d_inner + conv_dim].astype(BF16), w_in[:, d_inner + conv_dim:].astype(BF16))
    xbc = pl.pallas_call(
        _ssd_conv_body,
        grid=(bsz, conv_dim // cb),
        in_specs=[pl.BlockSpec((1, length, cb), lambda b, j: (b, 0, j)),
                  pl.BlockSpec((SSM_CONV, cb), lambda b, j: (0, j)),
                  pl.BlockSpec((1, cb), lambda b, j: (0, j))],
        out_specs=pl.BlockSpec((1, length, cb), lambda b, j: (b, 0, j)),
        out_shape=jax.ShapeDtypeStruct((bsz, length, conv_dim), F32),
        compiler_params=_params(("parallel", "parallel")),
        name="ssd_conv",
    )(xbc.reshape(bsz, length, conv_dim), conv_w.astype(F32), _row(conv_b))
    dt3 = dt.reshape(bsz, length, 2 * nh)
    expand = jnp.repeat(jnp.eye(nh, dtype=BF16), SSM_HEAD_DIM, axis=1)
    fwd = lambda c: pl.BlockSpec((1, q, c), lambda b, i: (b, i, 0))
    bwd = lambda c: pl.BlockSpec((1, q, c), lambda b, i: (b, nc - 1 - i, 0))
    y_f, y_b = pl.pallas_call(
        functools.partial(_ssd_scan_body, d_inner=d_inner),
        grid=(bsz, nc),
        in_specs=[fwd(conv_dim), fwd(2 * nh), bwd(conv_dim), bwd(2 * nh),
                  _const_spec((2, nh)), _const_spec((2, nh)), _const_spec((nh, d_inner))],
        out_specs=[fwd(d_inner), bwd(d_inner)],
        out_shape=[jax.ShapeDtypeStruct((bsz, length, d_inner), F32)] * 2,
        scratch_shapes=[pltpu.VMEM((SSM_STATE, d_inner), F32)] * 2,
        compiler_params=_params(("parallel", "arbitrary")),
        name="ssd_scan",
    )(xbc, dt3, xbc, dt3, dt_bias.astype(F32), a_log.astype(F32), expand)
    row_inputs = [(y_f.reshape(t, d_inner), d_inner), (y_b.reshape(t, d_inner), d_inner),
                  (xbc.reshape(t, conv_dim), d_inner), (z, d_inner)]
    consts = [_row(jnp.repeat(d_skip.astype(F32), SSM_HEAD_DIM)), _row(gate_norm)]
    return _ssd_out, row_inputs, consts, tail_tm


def _rwkv_in_body(h_ref, hp_ref, hn_ref, gpre_ref, mu_ref, wr_ref, wk_ref, wv_ref, w0_ref, w1_ref,
                  w2f_ref, w2b_ref, a0_ref, a1_ref, a2_ref, g1_ref, g2_ref, kk_ref, ka_ref,
                  rk_ref, seg_ref, segt_ref, tri_ref, up_ref,
                  v_out, bv_out, g_out, pendf_out, pendb_out, *dir_outs):
    i = pl.program_id(1)
    n_i = pl.num_programs(1)
    gpre = gpre_ref[...]
    x = _rms(h_ref[0], gpre)
    tm = x.shape[0]
    prev_row = _rms(hp_ref[0][SUBLANES - 1:SUBLANES, :], gpre) * jnp.where(i > 0, 1.0, 0.0)
    next_row = _rms(hn_ref[0][0:1, :], gpre) * jnp.where(i < n_i - 1, 1.0, 0.0)
    t = lax.broadcasted_iota(jnp.int32, x.shape, 0)
    x_prev = jnp.where(t == 0, prev_row, pltpu.roll(x, 1, axis=0))
    x_next = jnp.where(t == tm - 1, next_row, pltpu.roll(x, tm - 1, axis=0))
    xx = 0.5 * (x_prev + x_next) - x
    mix = lambda j: (x + xx * mu_ref[j:j + 1, :]).astype(BF16)
    xr, xw, xk, xv, xa, xg = (mix(j) for j in range(6))
    r = _dot(xr, wr_ref[...])
    k = _dot(xk, wk_ref[...])
    v = _dot(xv, wv_ref[...])
    lora_w = jnp.tanh(_dot(xw, w1_ref[...])).astype(BF16)
    log_decay = lambda w: -np.float32(np.exp(-0.5)) * jax.nn.sigmoid(w)
    lw_f = log_decay(w0_ref[0:1, :] + _dot(lora_w, w2f_ref[...]))
    lw_b = log_decay(w0_ref[1:2, :] + _dot(lora_w, w2b_ref[...]))
    a = jax.nn.sigmoid(a0_ref[...] + _dot(_dot(xa, a1_ref[...]).astype(BF16), a2_ref[...]))
    g_out[0] = _dot(jax.nn.sigmoid(_dot(xg, g1_ref[...])).astype(BF16), g2_ref[...])
    kk = k * kk_ref[...]
    norm = jnp.maximum(jnp.sqrt(_split_dot(kk * kk, seg_ref[...])), 1e-12)
    kk = kk * _split_dot(1.0 / norm, segt_ref[...])
    k = k * (1.0 + (a - 1.0) * ka_ref[...])
    rm = -kk
    add = kk * a
    v_out[0] = v.astype(v_out.dtype)
    bonus = _split_dot(_split_dot(r * k * rk_ref[...], seg_ref[...]), segt_ref[...])
    bv_out[0] = bonus * v
    c = RWKV_CHUNK
    for lw, cum, last, pend_out, outs in (
            (lw_f, _dot_split(tri_ref[...], lw_f), c - 1, pendf_out, dir_outs[:4]),
            (lw_b, lw_b + _dot_split(up_ref[...], lw_b), 0, pendb_out, dir_outs[4:])):
        e_neg = jnp.exp(-cum)
        values = (r * jnp.exp(cum), rm * jnp.exp(cum - lw), add * e_neg, k * e_neg)
        for o_ref, val in zip(outs, values):
            o_ref[0] = val.astype(o_ref.dtype)
        for m in range(tm // c):
            pend_out[0, m] = jnp.exp(cum[m * c + last:m * c + last + 1, :])


def _rwkv_chunk(r_t, rm_t, add_t, k_t, v_b, p_end, s_all, n_forward):
    c = r_t.shape[0]
    hd = RWKV_HEAD_DIM
    pair = 2 * hd
    assert c == hd and pair == LANES
    lane = lax.broadcasted_iota(jnp.int32, (c, pair), 1)
    ti = lax.broadcasted_iota(jnp.int32, (c, pair), 0)
    si = lane & (hd - 1)
    left = lane < hd
    keep_l = jnp.where(left, 1.0, 0.0).astype(BF16)
    keep_r = jnp.where(left, 0.0, 1.0).astype(BF16)
    bd = lambda x: jnp.concatenate([x * keep_l, x * keep_r], axis=0)
    eye = jnp.where(si == ti, 1.0, 0.0).astype(F32)
    pairs = range(r_t.shape[1] // pair)
    incl_of = lambda j: (si <= ti) if j < n_forward else (si >= ti)
    strict_of = lambda j: (si < ti) if j < n_forward else (si > ti)
    col = lambda x, j: x[:, j * pair:(j + 1) * pair]
    rows = lambda *xs: jnp.concatenate(xs, axis=0)
    v_bd = [bd(col(v_b, j)) for j in pairs]
    s0 = [col(s_all, j) for j in pairs]
    s0_bd = [bd(s.astype(BF16)) for s in s0]
    sc = [_dot_nt(rows(col(rm_t, j), col(r_t, j)), rows(bd(col(add_t, j)), bd(col(k_t, j)))) for j in pairs]
    n_mat = [jnp.where(strict_of(j), sc[j][:c, :pair], 0.0) for j in pairs]
    a_ak = [jnp.where(strict_of(j), sc[j][:c, pair:], 0.0).astype(BF16) for j in pairs]
    a_rb = [jnp.where(incl_of(j), sc[j][c:, :pair], 0.0).astype(BF16) for j in pairs]
    a_rk = [jnp.where(incl_of(j), sc[j][c:, pair:], 0.0).astype(BF16) for j in pairs]
    xkv = [_dot(rows(a_ak[j], a_rk[j]), v_bd[j]) for j in pairs]
    rs = [_dot_nt(col(r_t, j), s0_bd[j]) for j in pairs]
    inv = [eye + n_mat[j] for j in pairs]
    pw = [n_mat[j].astype(BF16) for j in pairs]
    pw = [_dot(pw[j], bd(pw[j])).astype(BF16) for j in pairs]
    span = 2
    while span < c:
        if 2 * span < c:
            both = [_dot(rows(pw[j], inv[j].astype(BF16)), bd(pw[j])) for j in pairs]
            pw = [both[j][:c].astype(BF16) for j in pairs]
            inv = [inv[j] + both[j][c:] for j in pairs]
        else:
            inv = [inv[j] + _dot(inv[j].astype(BF16), bd(pw[j])) for j in pairs]
        span *= 2
    wu = [_dot(inv[j].astype(BF16),
               jnp.concatenate([bd(col(rm_t, j)), bd(xkv[j][:c].astype(BF16))], axis=1)) for j in pairs]
    sa_b = [(_dot_nt(wu[j][:, :pair].astype(BF16), s0_bd[j]) + wu[j][:, pair:]).astype(BF16)
            for j in pairs]
    outs = [rs[j] + _dot(a_rb[j], bd(sa_b[j])) + xkv[j][c:] for j in pairs]
    full = [_dot_tn(rows(sa_b[j], col(v_b, j)), rows(col(add_t, j), col(k_t, j))) for j in pairs]
    states = [(s0[j] + jnp.where(left, full[j][:hd], full[j][hd:])) * col(p_end, j) for j in pairs]
    return jnp.concatenate(outs, axis=-1), jnp.concatenate(states, axis=-1)


def _rwkv_scan_body(*refs):
    f_in, b_in = refs[:6], refs[6:12]
    yf_ref, yb_ref, sf_ref, sb_ref = refs[12:]

    @pl.when(pl.program_id(2) == 0)
    def _():
        sf_ref[...] = jnp.zeros(sf_ref.shape, F32)
        sb_ref[...] = jnp.zeros(sb_ref.shape, F32)

    width = sf_ref.shape[1]
    both = lambda f, b: jnp.concatenate([f, b], axis=-1)
    ops = [both(f[0], b[0]) for f, b in zip(f_in[:5], b_in[:5])]
    y, s_new = _rwkv_chunk(*ops, both(f_in[5][0, 0], b_in[5][0, 0]), both(sf_ref[...], sb_ref[...]),
                           width // LANES)
    yf_ref[0] = y[:, :width]
    yb_ref[0] = y[:, width:]
    sf_ref[...] = s_new[:, :width]
    sb_ref[...] = s_new[:, width:]


def _rwkv_out(yf_ref, yb_ref, bv_ref, g_ref, lnw_ref, lnb_ref, seg_ref, segt_ref):
    seg, segt = seg_ref[...], segt_ref[...]
    head_mean = lambda x: _split_dot(_split_dot(x, seg) * (1.0 / RWKV_HEAD_DIM), segt)
    y = yf_ref[...] + yb_ref[...]
    cen = y - head_mean(y)
    var = head_mean(cen * cen)
    y = cen * lax.rsqrt(var + RWKV_LN_EPS) * lnw_ref[...] + lnb_ref[...]
    return ((y + bv_ref[...]) * g_ref[...]).astype(BF16)


def _pad_to(w, axis, size):
    pad = [(0, 0)] * w.ndim
    pad[axis] = (0, size - w.shape[axis])
    return jnp.pad(w, pad)


def _rwkv_mixer(h3, g_pre, mu, w_rkv, w0, w1, w2, a0, a1, a2, g1, g2, k_k, k_a, r_k, ln_w, ln_b, *,
                tm=256, heads_per_step=16, tail_tm=512):
    bsz, length, d = h3.shape
    t = bsz * length
    n_heads = d // RWKV_HEAD_DIM
    c = RWKV_CHUNK
    nc = length // c
    hw = heads_per_step * RWKV_HEAD_DIM
    assert length % tm == 0 and length % c == 0 and n_heads % heads_per_step == 0
    assert n_heads <= LANES
    rank_w = w1.shape[-1]
    w1_cat = jnp.concatenate([w1[0], w1[1]], axis=-1).astype(BF16)
    w2f = jnp.concatenate([w2[0], jnp.zeros_like(w2[1])], axis=0).astype(BF16)
    w2b = jnp.concatenate([jnp.zeros_like(w2[0]), w2[1]], axis=0).astype(BF16)
    a1p = _pad_to(a1, 1, LANES).astype(BF16)
    a2p = _pad_to(a2, 0, LANES).astype(BF16)
    g_rank = -(-g1.shape[1] // LANES) * LANES
    g1p = _pad_to(g1, 1, g_rank).astype(BF16)
    g2p = _pad_to(g2, 0, g_rank).astype(BF16)
    seg = _pad_to(jnp.repeat(jnp.eye(n_heads, dtype=BF16), RWKV_HEAD_DIM, axis=0), 1, LANES)
    segt = seg.T
    hb = tm // SUBLANES
    nhb = length // SUBLANES
    pos = np.arange(tm)
    same_chunk = (pos[:, None] // c) == (pos[None, :] // c)
    tri = jnp.asarray(same_chunk & (pos[None, :] <= pos[:, None]), BF16)
    up = jnp.asarray(same_chunk & (pos[None, :] > pos[:, None]), BF16)
    cur = pl.BlockSpec((1, tm, d), lambda b, i: (b, i, 0))
    pend_spec = pl.BlockSpec((1, tm // c, 1, d), lambda b, i: (b, i, 0, 0))
    pend_shape = jax.ShapeDtypeStruct((bsz, nc, 1, d), F32)
    act = jax.ShapeDtypeStruct((bsz, length, d), BF16)
    outs = pl.pallas_call(
        _rwkv_in_body,
        grid=(bsz, length // tm),
        in_specs=[cur,
                  pl.BlockSpec((1, SUBLANES, d), lambda b, i: (b, jnp.maximum(i * hb - 1, 0), 0)),
                  pl.BlockSpec((1, SUBLANES, d), lambda b, i: (b, jnp.minimum((i + 1) * hb, nhb - 1), 0)),
                  _const_spec((1, d)), _const_spec((6, d)),
                  _const_spec((d, d)), _const_spec((d, d)), _const_spec((d, d)),
                  _const_spec((2, d)), _const_spec((d, 2 * rank_w)),
                  _const_spec((2 * rank_w, d)), _const_spec((2 * rank_w, d)),
                  _const_spec((1, d)), _const_spec((d, LANES)), _const_spec((LANES, d)),
                  _const_spec((d, g_rank)), _const_spec((g_rank, d)),
                  _const_spec((1, d)), _const_spec((1, d)), _const_spec((1, d)),
                  _const_spec((d, LANES)), _const_spec((LANES, d)),
                  _const_spec((tm, tm)), _const_spec((tm, tm))],
        out_specs=[cur, cur, cur, pend_spec, pend_spec] + [cur] * 8,
        out_shape=[act, jax.ShapeDtypeStruct((bsz, length, d), F32),
                   jax.ShapeDtypeStruct((bsz, length, d), F32), pend_shape, pend_shape] + [act] * 8,
        compiler_params=_params(("parallel", "parallel")),
        name="rwkv_in",
    )(h3, h3, h3, _row(g_pre), mu.astype(F32), w_rkv[0].astype(BF16), w_rkv[1].astype(BF16),
      w_rkv[2].astype(BF16), w0.astype(F32), w1_cat, w2f, w2b, _row(a0), a1p, a2p, g1p, g2p,
      _row(k_k), _row(k_a), _row(r_k), seg, segt, tri, up)
    v, bv, g, pend_f, pend_b = outs[:5]
    ops_f, ops_b = outs[5:9], outs[9:]
    fwd = pl.BlockSpec((1, c, hw), lambda b, hg, i: (b, i, hg))
    bwd = pl.BlockSpec((1, c, hw), lambda b, hg, i: (b, nc - 1 - i, hg))
    fwd_p = pl.BlockSpec((1, 1, 1, hw), lambda b, hg, i: (b, i, 0, hg))
    bwd_p = pl.BlockSpec((1, 1, 1, hw), lambda b, hg, i: (b, nc - 1 - i, 0, hg))
    y_f, y_b = pl.pallas_call(
        _rwkv_scan_body,
        grid=(bsz, n_heads // heads_per_step, nc),
        in_specs=[fwd] * 5 + [fwd_p] + [bwd] * 5 + [bwd_p],
        out_specs=[fwd, bwd],
        out_shape=[jax.ShapeDtypeStruct((bsz, length, d), F32)] * 2,
        scratch_shapes=[pltpu.VMEM((RWKV_HEAD_DIM, hw), F32)] * 2,
        compiler_params=_params(("parallel", "parallel", "arbitrary")),
        name="rwkv_scan",
    )(*ops_f, v, pend_f, *ops_b, v, pend_b)
    row_inputs = [(x.reshape(t, d), d) for x in (y_f, y_b, bv, g)]
    return _rwkv_out, row_inputs, [_row(ln_w), _row(ln_b), seg, segt], tail_tm


def kernel(x_prompt, x_sample, mix_pre_norm, mix_post_norm, ffn_pre_norm, ffn_post_norm, mla_w_in, mla_q_norm, mla_w_qb, mla_kv_norm, mla_w_kvb, mla_w_o, ssm_w_in, ssm_conv_w, ssm_conv_b, ssm_a_log, ssm_dt_bias, ssm_d_skip, ssm_gate_norm, ssm_w_out, rwkv_mu, rwkv_w_rkv, rwkv_w0, rwkv_w1, rwkv_w2, rwkv_a0, rwkv_a1, rwkv_a2, rwkv_g1, rwkv_g2, rwkv_k_k, rwkv_k_a, rwkv_r_k, rwkv_ln_w, rwkv_ln_b, rwkv_w_o, ffn_w_in, ffn_w_out):
    depth = mix_pre_norm.shape[0]

    def trunk(x):
        bsz, length, d = x.shape
        h = x.reshape(bsz * length, d)
        for i in range(depth):
            kind, j = i % N_MIXERS, i // N_MIXERS
            h3 = h.reshape(bsz, length, d)
            if kind == 0:
                mixer = _mla_mixer(h3, mix_pre_norm[i], mla_w_in[j], mla_q_norm[j], mla_w_qb[j],
                                   mla_kv_norm[j], mla_w_kvb[j])
                w_o = mla_w_o[j]
            elif kind == 1:
                mixer = _ssd_mixer(h3, mix_pre_norm[i], ssm_w_in[j], ssm_conv_w[j], ssm_conv_b[j],
                                   ssm_a_log[j], ssm_dt_bias[j], ssm_d_skip[j], ssm_gate_norm[j])
                w_o = ssm_w_out[j]
            else:
                mixer = _rwkv_mixer(h3, mix_pre_norm[i], rwkv_mu[j], rwkv_w_rkv[j], rwkv_w0[j],
                                    rwkv_w1[j], rwkv_w2[j], rwkv_a0[j], rwkv_a1[j], rwkv_a2[j],
                                    rwkv_g1[j], rwkv_g2[j], rwkv_k_k[j], rwkv_k_a[j], rwkv_r_k[j],
                                    rwkv_ln_w[j], rwkv_ln_b[j])
                w_o = rwkv_w_o[j]
            h = _layer_tail(h, mixer, w_o, mix_post_norm[i], ffn_pre_norm[i], ffn_w_in[i],
                            ffn_w_out[i], ffn_post_norm[i])
        return h.reshape(bsz, length, d)

    return (trunk(x_prompt), trunk(x_sample))
```

```python
import functools

import jax
import jax.numpy as jnp
import numpy as np
from jax import lax
from jax.experimental import pallas as pl
from jax.experimental.pallas import tpu as pltpu

F32 = jnp.float32
BF16 = jnp.bfloat16

NORM_EPS = 1e-6
N_MIXERS = 3

VMEM_LIMIT_BYTES = 56 * 1024 * 1024
LANES = 128
SUBLANES = 8

MLA_HEADS = 8
MLA_NOPE = 128
MLA_ROPE = 64
MLA_V = 128
MLA_QK = MLA_NOPE + 2 * MLA_ROPE
ROPE_THETA = 10000.0

SSM_HEAD_DIM = 64
SSM_GROUPS = 4
SSM_STATE = 128
SSM_CONV = 5
SSM_CHUNK = 128

RWKV_HEAD_DIM = 64
RWKV_LN_EPS = 64e-5
RWKV_CHUNK = 64


def _rms(x, g, eps=NORM_EPS):
    return x * lax.rsqrt(jnp.mean(x * x, axis=-1, keepdims=True) + eps) * g


def _dot(a, b):
    return jnp.dot(a, b, preferred_element_type=F32)


def _dot_nt(a, b):
    return lax.dot_general(a, b, (((1,), (1,)), ((), ())), preferred_element_type=F32)


def _dot_tn(a, b):
    return lax.dot_general(a, b, (((0,), (0,)), ((), ())), preferred_element_type=F32)


def _bf16_terms(x, terms):
    pieces = []
    for _ in range(terms):
        p = x.astype(BF16)
        pieces.append(p)
        x = x - p.astype(F32)
    return pieces


def _dot_split(m, x, terms=2):
    return sum(_dot(m, p) for p in _bf16_terms(x, terms))


def _split_dot(x, m, terms=2):
    return sum(_dot(p, m) for p in _bf16_terms(x, terms))


def _silu(x):
    return x * jax.nn.sigmoid(x)


def _softplus(x):
    return jnp.maximum(x, 0.0) + jnp.log(1.0 + jnp.exp(-jnp.abs(x)))


def _const_spec(shape):
    nd = len(shape)
    return pl.BlockSpec(shape, lambda *_: (0,) * nd, pipeline_mode=pl.Buffered(1))


def _params(semantics):
    return pltpu.CompilerParams(dimension_semantics=semantics, vmem_limit_bytes=VMEM_LIMIT_BYTES)


def _row(v):
    return v.reshape(1, -1).astype(F32)


def _tail_body(*refs, n_mixer_refs, mixer_out, ff_chunk):
    h_ref = refs[0]
    y = mixer_out(*refs[1:1 + n_mixer_refs])
    wo_ref, gpost_ref, gpre_ref, win_ref, wout_ref, gfpost_ref, o_ref = refs[1 + n_mixer_refs:]
    d_ff = wout_ref.shape[0]
    h1 = h_ref[...] + _rms(_dot(y, wo_ref[...]), gpost_ref[...])
    u = _rms(h1, gpre_ref[...]).astype(BF16)
    acc = jnp.zeros(h1.shape, F32)
    for c in range(d_ff // ff_chunk):
        lo = c * ff_chunk
        gate = _dot(u, win_ref[:, lo:lo + ff_chunk])
        up = _dot(u, win_ref[:, d_ff + lo:d_ff + lo + ff_chunk])
        act = (_silu(gate) * up).astype(BF16)
        acc = acc + _dot(act, wout_ref[lo:lo + ff_chunk, :])
    o_ref[...] = h1 + _rms(acc, gfpost_ref[...])


def _layer_tail(h, mixer, w_o, g_post, g_pre, w_in, w_out, g_fpost, *, ff_chunk=256):
    mixer_out, row_inputs, const_inputs, tm = mixer
    t, d = h.shape
    ky = w_o.shape[0]
    d_ff = w_out.shape[0]
    assert t % tm == 0 and d_ff % ff_chunk == 0 and ff_chunk % LANES == 0
    row = lambda c: pl.BlockSpec((tm, c), lambda i: (i, 0))
    return pl.pallas_call(
        functools.partial(_tail_body, n_mixer_refs=len(row_inputs) + len(const_inputs),
                          mixer_out=mixer_out, ff_chunk=ff_chunk),
        grid=(t // tm,),
        in_specs=[row(d)] + [row(width) for _, width in row_inputs]
        + [_const_spec(c.shape) for c in const_inputs]
        + [_const_spec((ky, d)), _const_spec((1, d)), _const_spec((1, d)),
           _const_spec((d, 2 * d_ff)), _const_spec((d_ff, d)), _const_spec((1, d))],
        out_specs=row(d),
        out_shape=jax.ShapeDtypeStruct((t, d), F32),
        compiler_params=_params(("parallel",)),
        name="layer_tail",
    )(h, *[a for a, _ in row_inputs], *const_inputs, w_o.astype(BF16), _row(g_post), _row(g_pre),
      w_in.astype(BF16), w_out.astype(BF16), _row(g_fpost))


def _mla_in_body(h_ref, fq_ref, fk_ref, gpre_ref, win_ref, qn_ref, wq_ref, kvn_ref, wkv_ref,
                 q_ref, k_ref, v_ref, *, q_rank, kv_rank):
    u = _rms(h_ref[0], gpre_ref[...]).astype(BF16)
    lat = _dot(u, win_ref[...])
    q_lat = _rms(lat[:, :q_rank], qn_ref[...]).astype(BF16)
    kv_lat = _rms(lat[:, q_rank:q_rank + kv_rank], kvn_ref[...]).astype(BF16)
    kr = lat[:, q_rank + kv_rank:q_rank + kv_rank + LANES]
    kr_rot = lat[:, q_rank + kv_rank + LANES:q_rank + kv_rank + 2 * LANES]
    fk = fk_ref[...]
    k_rope = (kr * fk[:, :LANES] + kr_rot * fk[:, LANES:]).astype(BF16)
    fq = fq_ref[...]
    q = _dot(q_lat, wq_ref[...])
    kv = _dot(kv_lat, wkv_ref[...])
    for hd in range(MLA_HEADS):
        q_ref[0, :, hd * MLA_QK:(hd + 1) * MLA_QK] = (
            q[:, hd * MLA_QK:(hd + 1) * MLA_QK] * fq).astype(BF16)
        k_ref[0, :, hd * MLA_QK:hd * MLA_QK + MLA_NOPE] = (
            kv[:, hd * MLA_NOPE:(hd + 1) * MLA_NOPE].astype(BF16))
        k_ref[0, :, hd * MLA_QK + MLA_NOPE:(hd + 1) * MLA_QK] = k_rope
    v_ref[0] = jnp.transpose(kv[:, MLA_HEADS * MLA_NOPE:]).astype(BF16)


def _mla_attn_body(q_ref, k_ref, vt_ref, o_ref, *, q_sub, kv_chunk):
    n_chunks = k_ref.shape[1] // kv_chunk
    n_sub = q_ref.shape[1] // q_sub
    kv_rows = lambda c: slice(c * kv_chunk, (c + 1) * kv_chunk)
    score = lambda t: _dot_nt(k_ref[0], q_ref[0, t * q_sub:(t + 1) * q_sub, :])
    s_cur = score(0)
    for t in range(n_sub):
        s_next = score(t + 1) if t + 1 < n_sub else None
        m = jnp.max(s_cur, axis=0, keepdims=True)
        l = jnp.zeros((1, q_sub), F32)
        acc = jnp.zeros((vt_ref.shape[1], q_sub), F32)
        for c in range(n_chunks):
            p = jnp.exp2(s_cur[kv_rows(c), :] - m)
            l = l + jnp.sum(p, axis=0, keepdims=True)
            acc = acc + _dot(vt_ref[0, :, kv_rows(c)], p.astype(BF16))
        o_ref[0, t * q_sub:(t + 1) * q_sub, :] = jnp.transpose(acc / l).astype(o_ref.dtype)
        s_cur = s_next


def _rot_cols(w):
    half = w.shape[-1] // 2
    return jnp.concatenate([-w[..., half:], w[..., :half]], axis=-1)


def _mla_mixer(h3, g_pre, w_in, q_norm, w_qb, kv_norm, w_kvb, *, tm=512, tq=2048, q_sub=256,
               kv_chunk=512, tail_tm=512):
    bsz, length, d = h3.shape
    q_rank, kv_rank = q_norm.shape[0], kv_norm.shape[0]
    hd_q = MLA_NOPE + MLA_ROPE
    inv = 1.0 / (ROPE_THETA ** (jnp.arange(0, MLA_ROPE, 2, dtype=F32) / MLA_ROPE))
    ang = jnp.arange(length, dtype=F32)[:, None] * inv[None, :]
    cc = jnp.concatenate([jnp.cos(ang), jnp.cos(ang)], axis=-1)
    ss = jnp.concatenate([jnp.sin(ang), jnp.sin(ang)], axis=-1)
    scale = hd_q ** -0.5 * np.log2(np.e)
    fq = jnp.concatenate([jnp.full((length, MLA_NOPE), scale, F32), cc * scale, ss * scale], axis=-1)
    fk = jnp.concatenate([cc, cc, ss, ss], axis=-1)
    w_kr = w_in[:, q_rank + kv_rank:]
    w_in_ext = jnp.concatenate([w_in[:, :q_rank + kv_rank], w_kr, w_kr, _rot_cols(w_kr), _rot_cols(w_kr)],
                               axis=-1).astype(BF16)
    wq = w_qb.reshape(q_rank, MLA_HEADS, hd_q)
    wq_ext = jnp.concatenate([wq, _rot_cols(wq[..., MLA_NOPE:])], axis=-1).reshape(
        q_rank, MLA_HEADS * MLA_QK).astype(BF16)
    wkv = w_kvb.reshape(kv_rank, MLA_HEADS, MLA_NOPE + MLA_V)
    wkv_perm = jnp.concatenate([wkv[..., :MLA_NOPE].reshape(kv_rank, -1),
                                wkv[..., MLA_NOPE:].reshape(kv_rank, -1)], axis=-1).astype(BF16)
    n_in = w_in_ext.shape[1]
    assert length % tm == 0 and length % tq == 0
    q, k, v = pl.pallas_call(
        functools.partial(_mla_in_body, q_rank=q_rank, kv_rank=kv_rank),
        grid=(bsz, length // tm),
        in_specs=[pl.BlockSpec((1, tm, d), lambda b, i: (b, i, 0)),
                  pl.BlockSpec((tm, MLA_QK), lambda b, i: (i, 0)),
                  pl.BlockSpec((tm, MLA_QK), lambda b, i: (i, 0)),
                  _const_spec((1, d)), _const_spec((d, n_in)), _const_spec((1, q_rank)),
                  _const_spec((q_rank, MLA_HEADS * MLA_QK)), _const_spec((1, kv_rank)),
                  _const_spec((kv_rank, MLA_HEADS * (MLA_NOPE + MLA_V)))],
        out_specs=[pl.BlockSpec((1, tm, MLA_HEADS * MLA_QK), lambda b, i: (b, i, 0)),
                   pl.BlockSpec((1, tm, MLA_HEADS * MLA_QK), lambda b, i: (b, i, 0)),
                   pl.BlockSpec((1, MLA_HEADS * MLA_V, tm), lambda b, i: (b, 0, i))],
        out_shape=[jax.ShapeDtypeStruct((bsz, length, MLA_HEADS * MLA_QK), BF16),
                   jax.ShapeDtypeStruct((bsz, length, MLA_HEADS * MLA_QK), BF16),
                   jax.ShapeDtypeStruct((bsz, MLA_HEADS * MLA_V, length), BF16)],
        compiler_params=_params(("parallel", "parallel")),
        name="mla_in",
    )(h3, fq, fk, _row(g_pre), w_in_ext, _row(q_norm), wq_ext, _row(kv_norm), wkv_perm)
    o = pl.pallas_call(
        functools.partial(_mla_attn_body, q_sub=q_sub, kv_chunk=kv_chunk),
        grid=(bsz, MLA_HEADS, length // tq),
        in_specs=[pl.BlockSpec((1, tq, MLA_QK), lambda b, hd, i: (b, i, hd)),
                  pl.BlockSpec((1, length, MLA_QK), lambda b, hd, i: (b, 0, hd)),
                  pl.BlockSpec((1, MLA_V, length), lambda b, hd, i: (b, hd, 0))],
        out_specs=pl.BlockSpec((1, tq, MLA_V), lambda b, hd, i: (b, i, hd)),
        out_shape=jax.ShapeDtypeStruct((bsz, length, MLA_HEADS * MLA_V), BF16),
        compiler_params=_params(("parallel", "parallel", "parallel")),
        name="mla_attn",
    )(q, k, v)
    width = MLA_HEADS * MLA_V
    return (lambda o_ref: o_ref[...]), [(o.reshape(bsz * length, width), width)], [], tail_tm


def _ssd_in_body(h_ref, hp_ref, hn_ref, gpre_ref, wz_ref, wx_ref, wdt_ref, cw_ref, cb_ref,
                 z_ref, xbc_ref, dt_ref, *, col_block):
    i = pl.program_id(1)
    n_i = pl.num_programs(1)
    gpre = gpre_ref[...]
    u = _rms(h_ref[0], gpre).astype(BF16)
    halo = _rms(jnp.concatenate([hp_ref[0], hn_ref[0]], axis=0), gpre).astype(BF16)
    tm = u.shape[0]
    pad = SSM_CONV // 2
    keep_prev = jnp.where(i > 0, 1.0, 0.0)
    keep_next = jnp.where(i < n_i - 1, 1.0, 0.0)
    n_blocks = xbc_ref.shape[2] // col_block
    z_blocks = z_ref.shape[2] // col_block
    cols = lambda j: slice(j * col_block, (j + 1) * col_block)
    proj = lambda j: (_dot(u, wx_ref[:, cols(j)]), _dot(halo, wx_ref[:, cols(j)]))

    def conv(j, x, x_halo):
        x_prev = x_halo[:SUBLANES] * keep_prev
        x_next = x_halo[SUBLANES:] * keep_next
        head = jnp.concatenate([x_prev, x[:SUBLANES]], axis=0)
        tail = jnp.concatenate([x[tm - SUBLANES:], x_next], axis=0)
        acc = x * cw_ref[pad:pad + 1, cols(j)] + cb_ref[:, cols(j)]
        for kk in range(SSM_CONV):
            off = kk - pad
            if off == 0:
                continue
            bulk = pltpu.roll(x, (-off) % tm, axis=0)
            if off < 0:
                first = pltpu.roll(head, (-off) % (2 * SUBLANES), axis=0)[SUBLANES:]
                shifted = jnp.concatenate([first, bulk[SUBLANES:]], axis=0)
            else:
                last = pltpu.roll(tail, (-off) % (2 * SUBLANES), axis=0)[:SUBLANES]
                shifted = jnp.concatenate([bulk[:tm - SUBLANES], last], axis=0)
            acc = acc + shifted * cw_ref[kk:kk + 1, cols(j)]
        xbc_ref[0, :, cols(j)] = _silu(acc)

    nxt = proj(0)
    for j in range(n_blocks):
        cur = nxt
        if j + 1 < n_blocks:
            nxt = proj(j + 1)
        if j < z_blocks:
            z_ref[0, :, cols(j)] = _dot(u, wz_ref[:, cols(j)])
        conv(j, *cur)
    for j in range(n_blocks, z_blocks):
        z_ref[0, :, cols(j)] = _dot(u, wz_ref[:, cols(j)])
    dt_ref[0] = _dot(u, wdt_ref[...])


def _ssd_direction(xbc, dt_raw, bias, a_neg, expand, state_ref, reverse, d_inner):
    q = xbc.shape[0]
    gn = SSM_GROUPS * SSM_STATE
    hpg = d_inner // SSM_HEAD_DIM // SSM_GROUPS
    xs = xbc[:, :d_inner]
    b_in = xbc[:, d_inner:d_inner + gn]
    c_in = xbc[:, d_inner + gn:]
    dt = _softplus(dt_raw + bias)
    a_dt = dt * a_neg
    ti = lax.broadcasted_iota(jnp.int32, (q, q), 0)
    si = lax.broadcasted_iota(jnp.int32, (q, q), 1)
    keep = (si >= ti) if reverse else (si <= ti)
    tri = jnp.where(keep, 1.0, 0.0).astype(BF16)
    cum = _dot_split(tri, a_dt, terms=3)
    cum_t = jnp.transpose(cum)
    dt_t = jnp.transpose(dt)
    edge = cum[0:1, :] if reverse else cum[q - 1:q, :]
    x_end = (xs * _split_dot(dt * jnp.exp(edge - cum), expand)).astype(BF16)
    scale_in = _split_dot(jnp.exp(cum), expand)
    decay_state = _split_dot(jnp.exp(edge), expand)
    xs_b = xs.astype(BF16)
    gw = hpg * SSM_HEAD_DIM
    groups = range(SSM_GROUPS)
    gcols = lambda g: slice(g * gw, (g + 1) * gw)
    bg = [b_in[:, g * SSM_STATE:(g + 1) * SSM_STATE].astype(BF16) for g in groups]
    cg = [c_in[:, g * SSM_STATE:(g + 1) * SSM_STATE].astype(BF16) for g in groups]
    st = [state_ref[:, gcols(g)] for g in groups]
    cb = [_dot_nt(cg[g], bg[g]) for g in groups]
    y_off = [_dot(cg[g], st[g].astype(BF16)) * scale_in[:, gcols(g)] for g in groups]
    lane = lax.broadcasted_iota(jnp.int32, (q, 2 * SSM_HEAD_DIM), 1)
    keep_l = jnp.where(lane < SSM_HEAD_DIM, 1.0, 0.0).astype(BF16)
    keep_r = jnp.where(lane < SSM_HEAD_DIM, 0.0, 1.0).astype(BF16)
    outs = []
    for pr in range(d_inner // (2 * SSM_HEAD_DIM)):
        m_pair = []
        for hd in (2 * pr, 2 * pr + 1):
            seg = cum[:, hd:hd + 1] - cum_t[hd:hd + 1, :]
            decay = jnp.exp(jnp.where(keep, seg, -jnp.inf))
            m_pair.append((cb[hd // hpg] * decay * dt_t[hd:hd + 1, :]).astype(BF16))
        x_pair = xs_b[:, pr * 2 * SSM_HEAD_DIM:(pr + 1) * 2 * SSM_HEAD_DIM]
        x_bd = jnp.concatenate([x_pair * keep_l, x_pair * keep_r], axis=0)
        outs.append(_dot(jnp.concatenate(m_pair, axis=1), x_bd))
    y = jnp.concatenate(outs, axis=-1) + jnp.concatenate(y_off, axis=-1)
    new_state = [st[g] * decay_state[:, gcols(g)] + _dot_tn(bg[g], x_end[:, gcols(g)]) for g in groups]
    for g in groups:
        state_ref[:, gcols(g)] = new_state[g]
    return y


def _ssd_scan_body(xf_ref, dtf_ref, xb_ref, dtb_ref, bias_ref, alog_ref, exp_ref,
                   yf_ref, yb_ref, sf_ref, sb_ref, *, d_inner):
    @pl.when(pl.program_id(1) == 0)
    def _():
        sf_ref[...] = jnp.zeros(sf_ref.shape, F32)
        sb_ref[...] = jnp.zeros(sb_ref.shape, F32)

    nh = d_inner // SSM_HEAD_DIM
    a_neg = -jnp.exp(alog_ref[...])
    expand = exp_ref[...]
    yf_ref[0] = _ssd_direction(xf_ref[0], dtf_ref[0][:, :nh], bias_ref[0:1, :], a_neg[0:1, :], expand,
                               sf_ref, False, d_inner)
    yb_ref[0] = _ssd_direction(xb_ref[0], dtb_ref[0][:, nh:], bias_ref[1:2, :], a_neg[1:2, :], expand,
                               sb_ref, True, d_inner)


def _ssd_out(yf_ref, yb_ref, xs_ref, z_ref, dsk_ref, gn_ref):
    y = (yf_ref[...] + yb_ref[...] + xs_ref[...] * dsk_ref[...]) * _silu(z_ref[...])
    gw = y.shape[1] // SSM_GROUPS
    return jnp.concatenate(
        [_rms(y[:, g * gw:(g + 1) * gw], gn_ref[:, g * gw:(g + 1) * gw]).astype(BF16)
         for g in range(SSM_GROUPS)], axis=-1)


def _ssd_mixer(h3, g_pre, w_in, conv_w, conv_b, a_log, dt_bias, d_skip, gate_norm, *, tm=512,
               col_block=512, tail_tm=256):
    bsz, length, d = h3.shape
    t = bsz * length
    nh = a_log.shape[-1]
    d_inner = nh * SSM_HEAD_DIM
    conv_dim = d_inner + 2 * SSM_GROUPS * SSM_STATE
    q = SSM_CHUNK
    nc = length // q
    assert length % tm == 0 and length % q == 0
    hb = tm // SUBLANES
    nhb = length // SUBLANES
    cur = lambda c: pl.BlockSpec((1, tm, c), lambda b, i: (b, i, 0))
    z, xbc, dt3 = pl.pallas_call(
        functools.partial(_ssd_in_body, col_block=col_block),
        grid=(bsz, length // tm),
        in_specs=[cur(d),
                  pl.BlockSpec((1, SUBLANES, d), lambda b, i: (b, jnp.maximum(i * hb - 1, 0), 0)),
                  pl.BlockSpec((1, SUBLANES, d), lambda b, i: (b, jnp.minimum((i + 1) * hb, nhb - 1), 0)),
                  _const_spec((1, d)), _const_spec((d, d_inner)), _const_spec((d, conv_dim)),
                  _const_spec((d, 2 * nh)), _const_spec((SSM_CONV, conv_dim)), _const_spec((1, conv_dim))],
        out_specs=[cur(d_inner), cur(conv_dim), cur(2 * nh)],
        out_shape=[jax.ShapeDtypeStruct((bsz, length, d_inner), F32),
                   jax.ShapeDtypeStruct((bsz, length, conv_dim), F32),
                   jax.ShapeDtypeStruct((bsz, length, 2 * nh), F32)],
        compiler_params=_params(("parallel", "parallel")),
        name="ssd_in",
    )(h3, h3, h3, _row(g_pre), w_in[:, :d_inner].astype(BF16),
      w_in[:, d_inner:d_inner + conv_dim].astype(BF16), w_in[:, d_inner + conv_dim:].astype(BF16),
      conv_w.astype(F32), _row(conv_b))
    z = z.reshape(t, d_inner)
    expand = jnp.repeat(jnp.eye(nh, dtype=BF16), SSM_HEAD_DIM, axis=1)
    fwd = lambda c: pl.BlockSpec((1, q, c), lambda b, i: (b, i, 0))
    bwd = lambda c: pl.BlockSpec((1, q, c), lambda b, i: (b, nc - 1 - i, 0))
    y_f, y_b = pl.pallas_call(
        functools.partial(_ssd_scan_body, d_inner=d_inner),
        grid=(bsz, nc),
        in_specs=[fwd(conv_dim), fwd(2 * nh), bwd(conv_dim), bwd(2 * nh),
                  _const_spec((2, nh)), _const_spec((2, nh)), _const_spec((nh, d_inner))],
        out_specs=[fwd(d_inner), bwd(d_inner)],
        out_shape=[jax.ShapeDtypeStruct((bsz, length, d_inner), F32)] * 2,
        scratch_shapes=[pltpu.VMEM((SSM_STATE, d_inner), F32)] * 2,
        compiler_params=_params(("parallel", "arbitrary")),
        name="ssd_scan",
    )(xbc, dt3, xbc, dt3, dt_bias.astype(F32), a_log.astype(F32), expand)
    row_inputs = [(y_f.reshape(t, d_inner), d_inner), (y_b.reshape(t, d_inner), d_inner),
                  (xbc.reshape(t, conv_dim), d_inner), (z, d_inner)]
    consts = [_row(jnp.repeat(d_skip.astype(F32), SSM_HEAD_DIM)), _row(gate_norm)]
    return _ssd_out, row_inputs, consts, tail_tm


def _rwkv_in_body(h_ref, hp_ref, hn_ref, gpre_ref, mu_ref, wr_ref, wk_ref, wv_ref, w0_ref, w1_ref,
                  w2f_ref, w2b_ref, a0_ref, a1_ref, a2_ref, g1_ref, g2_ref, kk_ref, ka_ref,
                  rk_ref, seg_ref, segt_ref, tri_ref, up_ref,
                  v_out, bv_out, g_out, pendf_out, pendb_out, *dir_outs):
    i = pl.program_id(1)
    n_i = pl.num_programs(1)
    gpre = gpre_ref[...]
    x = _rms(h_ref[0], gpre)
    tm = x.shape[0]
    prev_row = _rms(hp_ref[0][SUBLANES - 1:SUBLANES, :], gpre) * jnp.where(i > 0, 1.0, 0.0)
    next_row = _rms(hn_ref[0][0:1, :], gpre) * jnp.where(i < n_i - 1, 1.0, 0.0)
    t = lax.broadcasted_iota(jnp.int32, x.shape, 0)
    x_prev = jnp.where(t == 0, prev_row, pltpu.roll(x, 1, axis=0))
    x_next = jnp.where(t == tm - 1, next_row, pltpu.roll(x, tm - 1, axis=0))
    xx = 0.5 * (x_prev + x_next) - x
    mix = lambda j: (x + xx * mu_ref[j:j + 1, :]).astype(BF16)
    xr, xw, xk, xv, xa, xg = (mix(j) for j in range(6))
    r = _dot(xr, wr_ref[...])
    k = _dot(xk, wk_ref[...])
    v = _dot(xv, wv_ref[...])
    lora_w = jnp.tanh(_dot(xw, w1_ref[...])).astype(BF16)
    log_decay = lambda w: -np.float32(np.exp(-0.5)) * jax.nn.sigmoid(w)
    lw_f = log_decay(w0_ref[0:1, :] + _dot(lora_w, w2f_ref[...]))
    lw_b = log_decay(w0_ref[1:2, :] + _dot(lora_w, w2b_ref[...]))
    a = jax.nn.sigmoid(a0_ref[...] + _dot(_dot(xa, a1_ref[...]).astype(BF16), a2_ref[...]))
    g_out[0] = _dot(jax.nn.sigmoid(_dot(xg, g1_ref[...])).astype(BF16), g2_ref[...])
    kk = k * kk_ref[...]
    norm = jnp.maximum(jnp.sqrt(_split_dot(kk * kk, seg_ref[...])), 1e-12)
    kk = kk * _split_dot(1.0 / norm, segt_ref[...])
    k = k * (1.0 + (a - 1.0) * ka_ref[...])
    rm = -kk
    add = kk * a
    v_out[0] = v.astype(v_out.dtype)
    bonus = _split_dot(_split_dot(r * k * rk_ref[...], seg_ref[...]), segt_ref[...])
    bv_out[0] = bonus * v
    c = RWKV_CHUNK
    for lw, cum, last, pend_out, outs in (
            (lw_f, _dot_split(tri_ref[...], lw_f), c - 1, pendf_out, dir_outs[:4]),
            (lw_b, lw_b + _dot_split(up_ref[...], lw_b), 0, pendb_out, dir_outs[4:])):
        e_neg = jnp.exp(-cum)
        values = (r * jnp.exp(cum), rm * jnp.exp(cum - lw), add * e_neg, k * e_neg)
        for o_ref, val in zip(outs, values):
            o_ref[0] = val.astype(o_ref.dtype)
        for m in range(tm // c):
            pend_out[0, m] = jnp.exp(cum[m * c + last:m * c + last + 1, :])


def _rwkv_chunk(r_t, rm_t, add_t, k_t, v_b, p_end, s_all, n_forward):
    c = r_t.shape[0]
    hd = RWKV_HEAD_DIM
    pair = 2 * hd
    assert c == hd and pair == LANES
    lane = lax.broadcasted_iota(jnp.int32, (c, pair), 1)
    ti = lax.broadcasted_iota(jnp.int32, (c, pair), 0)
    si = lane & (hd - 1)
    left = lane < hd
    keep_l = jnp.where(left, 1.0, 0.0).astype(BF16)
    keep_r = jnp.where(left, 0.0, 1.0).astype(BF16)
    bd = lambda x: jnp.concatenate([x * keep_l, x * keep_r], axis=0)
    eye = jnp.where(si == ti, 1.0, 0.0).astype(F32)
    pairs = range(r_t.shape[1] // pair)
    incl_of = lambda j: (si <= ti) if j < n_forward else (si >= ti)
    strict_of = lambda j: (si < ti) if j < n_forward else (si > ti)
    col = lambda x, j: x[:, j * pair:(j + 1) * pair]
    rows = lambda *xs: jnp.concatenate(xs, axis=0)
    v_bd = [bd(col(v_b, j)) for j in pairs]
    s0 = [col(s_all, j) for j in pairs]
    s0_bd = [bd(s.astype(BF16)) for s in s0]
    sc = [_dot_nt(rows(col(rm_t, j), col(r_t, j)), rows(bd(col(add_t, j)), bd(col(k_t, j)))) for j in pairs]
    n_mat = [jnp.where(strict_of(j), sc[j][:c, :pair], 0.0) for j in pairs]
    a_ak = [jnp.where(strict_of(j), sc[j][:c, pair:], 0.0).astype(BF16) for j in pairs]
    a_rb = [jnp.where(incl_of(j), sc[j][c:, :pair], 0.0).astype(BF16) for j in pairs]
    a_rk = [jnp.where(incl_of(j), sc[j][c:, pair:], 0.0).astype(BF16) for j in pairs]
    xkv = [_dot(rows(a_ak[j], a_rk[j]), v_bd[j]) for j in pairs]
    rs = [_dot_nt(col(r_t, j), s0_bd[j]) for j in pairs]
    inv = [eye + n_mat[j] for j in pairs]
    pw = [n_mat[j].astype(BF16) for j in pairs]
    pw = [_dot(pw[j], bd(pw[j])).astype(BF16) for j in pairs]
    span = 2
    while span < c:
        if 2 * span < c:
            both = [_dot(rows(pw[j], inv[j].astype(BF16)), bd(pw[j])) for j in pairs]
            pw = [both[j][:c].astype(BF16) for j in pairs]
            inv = [inv[j] + both[j][c:] for j in pairs]
        else:
            inv = [inv[j] + _dot(inv[j].astype(BF16), bd(pw[j])) for j in pairs]
        span *= 2
    wu = [_dot(inv[j].astype(BF16),
               jnp.concatenate([bd(col(rm_t, j)), bd(xkv[j][:c].astype(BF16))], axis=1)) for j in pairs]
    sa_b = [(_dot_nt(wu[j][:, :pair].astype(BF16), s0_bd[j]) + wu[j][:, pair:]).astype(BF16)
            for j in pairs]
    outs = [rs[j] + _dot(a_rb[j], bd(sa_b[j])) + xkv[j][c:] for j in pairs]
    full = [_dot_tn(rows(sa_b[j], col(v_b, j)), rows(col(add_t, j), col(k_t, j))) for j in pairs]
    states = [(s0[j] + jnp.where(left, full[j][:hd], full[j][hd:])) * col(p_end, j) for j in pairs]
    return jnp.concatenate(outs, axis=-1), jnp.concatenate(states, axis=-1)


def _rwkv_scan_body(*refs):
    f_in, b_in = refs[:6], refs[6:12]
    yf_ref, yb_ref, sf_ref, sb_ref = refs[12:]

    @pl.when(pl.program_id(2) == 0)
    def _():
        sf_ref[...] = jnp.zeros(sf_ref.shape, F32)
        sb_ref[...] = jnp.zeros(sb_ref.shape, F32)

    width = sf_ref.shape[1]
    both = lambda f, b: jnp.concatenate([f, b], axis=-1)
    ops = [both(f[0], b[0]) for f, b in zip(f_in[:5], b_in[:5])]
    y, s_new = _rwkv_chunk(*ops, both(f_in[5][0, 0], b_in[5][0, 0]), both(sf_ref[...], sb_ref[...]),
                           width // LANES)
    yf_ref[0] = y[:, :width]
    yb_ref[0] = y[:, width:]
    sf_ref[...] = s_new[:, :width]
    sb_ref[...] = s_new[:, width:]


def _rwkv_out(yf_ref, yb_ref, bv_ref, g_ref, lnw_ref, lnb_ref, seg_ref, segt_ref):
    seg, segt = seg_ref[...], segt_ref[...]
    head_mean = lambda x: _split_dot(_split_dot(x, seg) * (1.0 / RWKV_HEAD_DIM), segt)
    y = yf_ref[...] + yb_ref[...]
    cen = y - head_mean(y)
    var = head_mean(cen * cen)
    y = cen * lax.rsqrt(var + RWKV_LN_EPS) * lnw_ref[...] + lnb_ref[...]
    return ((y + bv_ref[...]) * g_ref[...]).astype(BF16)


def _pad_to(w, axis, size):
    pad = [(0, 0)] * w.ndim
    pad[axis] = (0, size - w.shape[axis])
    return jnp.pad(w, pad)


def _rwkv_mixer(h3, g_pre, mu, w_rkv, w0, w1, w2, a0, a1, a2, g1, g2, k_k, k_a, r_k, ln_w, ln_b, *,
                tm=256, heads_per_step=16, tail_tm=512):
    bsz, length, d = h3.shape
    t = bsz * length
    n_heads = d // RWKV_HEAD_DIM
    c = RWKV_CHUNK
    nc = length // c
    hw = heads_per_step * RWKV_HEAD_DIM
    assert length % tm == 0 and length % c == 0 and n_heads % heads_per_step == 0
    assert n_heads <= LANES
    rank_w = w1.shape[-1]
    w1_cat = jnp.concatenate([w1[0], w1[1]], axis=-1).astype(BF16)
    w2f = jnp.concatenate([w2[0], jnp.zeros_like(w2[1])], axis=0).astype(BF16)
    w2b = jnp.concatenate([jnp.zeros_like(w2[0]), w2[1]], axis=0).astype(BF16)
    a1p = _pad_to(a1, 1, LANES).astype(BF16)
    a2p = _pad_to(a2, 0, LANES).astype(BF16)
    g_rank = -(-g1.shape[1] // LANES) * LANES
    g1p = _pad_to(g1, 1, g_rank).astype(BF16)
    g2p = _pad_to(g2, 0, g_rank).astype(BF16)
    seg = _pad_to(jnp.repeat(jnp.eye(n_heads, dtype=BF16), RWKV_HEAD_DIM, axis=0), 1, LANES)
    segt = seg.T
    hb = tm // SUBLANES
    nhb = length // SUBLANES
    pos = np.arange(tm)
    same_chunk = (pos[:, None] // c) == (pos[None, :] // c)
    tri = jnp.asarray(same_chunk & (pos[None, :] <= pos[:, None]), BF16)
    up = jnp.asarray(same_chunk & (pos[None, :] > pos[:, None]), BF16)
    cur = pl.BlockSpec((1, tm, d), lambda b, i: (b, i, 0))
    pend_spec = pl.BlockSpec((1, tm // c, 1, d), lambda b, i: (b, i, 0, 0))
    pend_shape = jax.ShapeDtypeStruct((bsz, nc, 1, d), F32)
    act = jax.ShapeDtypeStruct((bsz, length, d), BF16)
    outs = pl.pallas_call(
        _rwkv_in_body,
        grid=(bsz, length // tm),
        in_specs=[cur,
                  pl.BlockSpec((1, SUBLANES, d), lambda b, i: (b, jnp.maximum(i * hb - 1, 0), 0)),
                  pl.BlockSpec((1, SUBLANES, d), lambda b, i: (b, jnp.minimum((i + 1) * hb, nhb - 1), 0)),
                  _const_spec((1, d)), _const_spec((6, d)),
                  _const_spec((d, d)), _const_spec((d, d)), _const_spec((d, d)),
                  _const_spec((2, d)), _const_spec((d, 2 * rank_w)),
                  _const_spec((2 * rank_w, d)), _const_spec((2 * rank_w, d)),
                  _const_spec((1, d)), _const_spec((d, LANES)), _const_spec((LANES, d)),
                  _const_spec((d, g_rank)), _const_spec((g_rank, d)),
                  _const_spec((1, d)), _const_spec((1, d)), _const_spec((1, d)),
                  _const_spec((d, LANES)), _const_spec((LANES, d)),
                  _const_spec((tm, tm)), _const_spec((tm, tm))],
        out_specs=[cur, cur, cur, pend_spec, pend_spec] + [cur] * 8,
        out_shape=[act, jax.ShapeDtypeStruct((bsz, length, d), F32),
                   jax.ShapeDtypeStruct((bsz, length, d), F32), pend_shape, pend_shape] + [act] * 8,
        compiler_params=_params(("parallel", "parallel")),
        name="rwkv_in",
    )(h3, h3, h3, _row(g_pre), mu.astype(F32), w_rkv[0].astype(BF16), w_rkv[1].astype(BF16),
      w_rkv[2].astype(BF16), w0.astype(F32), w1_cat, w2f, w2b, _row(a0), a1p, a2p, g1p, g2p,
      _row(k_k), _row(k_a), _row(r_k), seg, segt, tri, up)
    v, bv, g, pend_f, pend_b = outs[:5]
    ops_f, ops_b = outs[5:9], outs[9:]
    fwd = pl.BlockSpec((1, c, hw), lambda b, hg, i: (b, i, hg))
    bwd = pl.BlockSpec((1, c, hw), lambda b, hg, i: (b, nc - 1 - i, hg))
    fwd_p = pl.BlockSpec((1, 1, 1, hw), lambda b, hg, i: (b, i, 0, hg))
    bwd_p = pl.BlockSpec((1, 1, 1, hw), lambda b, hg, i: (b, nc - 1 - i, 0, hg))
    y_f, y_b = pl.pallas_call(
        _rwkv_scan_body,
        grid=(bsz, n_heads // heads_per_step, nc),
        in_specs=[fwd] * 5 + [fwd_p] + [bwd] * 5 + [bwd_p],
        out_specs=[fwd, bwd],
        out_shape=[jax.ShapeDtypeStruct((bsz, length, d), F32)] * 2,
        scratch_shapes=[pltpu.VMEM((RWKV_HEAD_DIM, hw), F32)] * 2,
        compiler_params=_params(("parallel", "parallel", "arbitrary")),
        name="rwkv_scan",
    )(*ops_f, v, pend_f, *ops_b, v, pend_b)
    row_inputs = [(x.reshape(t, d), d) for x in (y_f, y_b, bv, g)]
    return _rwkv_out, row_inputs, [_row(ln_w), _row(ln_b), seg, segt], tail_tm


def kernel(x_prompt, x_sample, mix_pre_norm, mix_post_norm, ffn_pre_norm, ffn_post_norm, mla_w_in, mla_q_norm, mla_w_qb, mla_kv_norm, mla_w_kvb, mla_w_o, ssm_w_in, ssm_conv_w, ssm_conv_b, ssm_a_log, ssm_dt_bias, ssm_d_skip, ssm_gate_norm, ssm_w_out, rwkv_mu, rwkv_w_rkv, rwkv_w0, rwkv_w1, rwkv_w2, rwkv_a0, rwkv_a1, rwkv_a2, rwkv_g1, rwkv_g2, rwkv_k_k, rwkv_k_a, rwkv_r_k, rwkv_ln_w, rwkv_ln_b, rwkv_w_o, ffn_w_in, ffn_w_out):
    depth = mix_pre_norm.shape[0]

    def trunk(x):
        bsz, length, d = x.shape
        h = x.reshape(bsz * length, d)
        for i in range(depth):
            kind, j = i % N_MIXERS, i // N_MIXERS
            h3 = h.reshape(bsz, length, d)
            if kind == 0:
                mixer = _mla_mixer(h3, mix_pre_norm[i], mla_w_in[j], mla_q_norm[j], mla_w_qb[j],
                                   mla_kv_norm[j], mla_w_kvb[j])
                w_o = mla_w_o[j]
            elif kind == 1:
                mixer = _ssd_mixer(h3, mix_pre_norm[i], ssm_w_in[j], ssm_conv_w[j], ssm_conv_b[j],
                                   ssm_a_log[j], ssm_dt_bias[j], ssm_d_skip[j], ssm_gate_norm[j])
                w_o = ssm_w_out[j]
            else:
                mixer = _rwkv_mixer(h3, mix_pre_norm[i], rwkv_mu[j], rwkv_w_rkv[j], rwkv_w0[j],
                                    rwkv_w1[j], rwkv_w2[j], rwkv_a0[j], rwkv_a1[j], rwkv_a2[j],
                                    rwkv_g1[j], rwkv_g2[j], rwkv_k_k[j], rwkv_k_a[j], rwkv_r_k[j],
                                    rwkv_ln_w[j], rwkv_ln_b[j])
                w_o = rwkv_w_o[j]
            h = _layer_tail(h, mixer, w_o, mix_post_norm[i], ffn_pre_norm[i], ffn_w_in[i],
                            ffn_w_out[i], ffn_post_norm[i])
        return h.reshape(bsz, length, d)

    return (trunk(x_prompt), trunk(x_sample))
```

```python
import functools

import jax
import jax.numpy as jnp
import numpy as np
from jax import lax
from jax.experimental import pallas as pl
from jax.experimental.pallas import tpu as pltpu

F32 = jnp.float32
BF16 = jnp.bfloat16

NORM_EPS = 1e-6
N_MIXERS = 3

VMEM_LIMIT_BYTES = 56 * 1024 * 1024
LANES = 128
SUBLANES = 8

MLA_HEADS = 8
MLA_NOPE = 128
MLA_ROPE = 64
MLA_V = 128
MLA_QK = MLA_NOPE + 2 * MLA_ROPE
ROPE_THETA = 10000.0

SSM_HEAD_DIM = 64
SSM_GROUPS = 4
SSM_STATE = 128
SSM_CONV = 5
SSM_CHUNK = 128

RWKV_HEAD_DIM = 64
RWKV_LN_EPS = 64e-5
RWKV_CHUNK = 64


def _rms(x, g, eps=NORM_EPS):
    return x * lax.rsqrt(jnp.mean(x * x, axis=-1, keepdims=True) + eps) * g


def _dot(a, b):
    return jnp.dot(a, b, preferred_element_type=F32)


def _dot_nt(a, b):
    return lax.dot_general(a, b, (((1,), (1,)), ((), ())), preferred_element_type=F32)


def _dot_tn(a, b):
    return lax.dot_general(a, b, (((0,), (0,)), ((), ())), preferred_element_type=F32)


def _bf16_terms(x, terms):
    pieces = []
    for _ in range(terms):
        p = x.astype(BF16)
        pieces.append(p)
        x = x - p.astype(F32)
    return pieces


def _dot_split(m, x, terms=2):
    return sum(_dot(m, p) for p in _bf16_terms(x, terms))


def _split_dot(x, m, terms=2):
    return sum(_dot(p, m) for p in _bf16_terms(x, terms))


def _silu(x):
    return x * jax.nn.sigmoid(x)


def _softplus(x):
    return jnp.maximum(x, 0.0) + jnp.log(1.0 + jnp.exp(-jnp.abs(x)))


def _const_spec(shape):
    nd = len(shape)
    return pl.BlockSpec(shape, lambda *_: (0,) * nd, pipeline_mode=pl.Buffered(1))


def _params(semantics):
    return pltpu.CompilerParams(dimension_semantics=semantics, vmem_limit_bytes=VMEM_LIMIT_BYTES)


def _row(v):
    return v.reshape(1, -1).astype(F32)


def _tail_body(*refs, n_mixer_refs, mixer_out, ff_chunk):
    h_ref = refs[0]
    wo_ref, gpost_ref, gpre_ref, win_ref, wout_ref, gfpost_ref, o_ref = refs[1 + n_mixer_refs:]
    d_ff = wout_ref.shape[0]
    m = jnp.zeros(h_ref.shape, F32)
    k0 = 0
    for blk in mixer_out(*refs[1:1 + n_mixer_refs]):
        m = m + _dot(blk, wo_ref[k0:k0 + blk.shape[1], :])
        k0 += blk.shape[1]
    h1 = h_ref[...] + _rms(m, gpost_ref[...])
    u = _rms(h1, gpre_ref[...]).astype(BF16)
    n_chunks = d_ff // ff_chunk
    gate_up = lambda c: (_dot(u, win_ref[:, c * ff_chunk:(c + 1) * ff_chunk]),
                         _dot(u, win_ref[:, d_ff + c * ff_chunk:d_ff + (c + 1) * ff_chunk]))
    acc = jnp.zeros(h1.shape, F32)
    nxt = gate_up(0)
    for c in range(n_chunks):
        gate, up = nxt
        if c + 1 < n_chunks:
            nxt = gate_up(c + 1)
        act = (_silu(gate) * up).astype(BF16)
        acc = acc + _dot(act, wout_ref[c * ff_chunk:(c + 1) * ff_chunk, :])
    o_ref[...] = h1 + _rms(acc, gfpost_ref[...])


def _layer_tail(h, mixer, w_o, g_post, g_pre, w_in, w_out, g_fpost, *, ff_chunk=256):
    mixer_out, row_inputs, const_inputs, tm = mixer
    t, d = h.shape
    ky = w_o.shape[0]
    d_ff = w_out.shape[0]
    assert t % tm == 0 and d_ff % ff_chunk == 0 and ff_chunk % LANES == 0
    row = lambda c: pl.BlockSpec((tm, c), lambda i: (i, 0))
    return pl.pallas_call(
        functools.partial(_tail_body, n_mixer_refs=len(row_inputs) + len(const_inputs),
                          mixer_out=mixer_out, ff_chunk=ff_chunk),
        grid=(t // tm,),
        in_specs=[row(d)] + [row(width) for _, width in row_inputs]
        + [_const_spec(c.shape) for c in const_inputs]
        + [_const_spec((ky, d)), _const_spec((1, d)), _const_spec((1, d)),
           _const_spec((d, 2 * d_ff)), _const_spec((d_ff, d)), _const_spec((1, d))],
        out_specs=row(d),
        out_shape=jax.ShapeDtypeStruct((t, d), F32),
        compiler_params=_params(("parallel",)),
        name="layer_tail",
    )(h, *[a for a, _ in row_inputs], *const_inputs, w_o.astype(BF16), _row(g_post), _row(g_pre),
      w_in.astype(BF16), w_out.astype(BF16), _row(g_fpost))


def _mla_in_body(h_ref, fq_ref, fk_ref, gpre_ref, win_ref, qn_ref, wq_ref, kvn_ref, wkv_ref,
                 q_ref, k_ref, v_ref, *, q_rank, kv_rank):
    u = _rms(h_ref[0], gpre_ref[...]).astype(BF16)
    lat = _dot(u, win_ref[...])
    q_lat = _rms(lat[:, :q_rank], qn_ref[...]).astype(BF16)
    kv_lat = _rms(lat[:, q_rank:q_rank + kv_rank], kvn_ref[...]).astype(BF16)
    kr = lat[:, q_rank + kv_rank:q_rank + kv_rank + LANES]
    kr_rot = lat[:, q_rank + kv_rank + LANES:q_rank + kv_rank + 2 * LANES]
    fk = fk_ref[...]
    k_rope = (kr * fk[:, :LANES] + kr_rot * fk[:, LANES:]).astype(BF16)
    fq = fq_ref[...]
    q = _dot(q_lat, wq_ref[...])
    kv = _dot(kv_lat, wkv_ref[...])
    for hd in range(MLA_HEADS):
        q_ref[0, :, hd * MLA_QK:(hd + 1) * MLA_QK] = (
            q[:, hd * MLA_QK:(hd + 1) * MLA_QK] * fq).astype(BF16)
        k_ref[0, :, hd * MLA_QK:hd * MLA_QK + MLA_NOPE] = (
            kv[:, hd * MLA_NOPE:(hd + 1) * MLA_NOPE].astype(BF16))
        k_ref[0, :, hd * MLA_QK + MLA_NOPE:(hd + 1) * MLA_QK] = k_rope
    v_ref[0] = jnp.transpose(kv[:, MLA_HEADS * MLA_NOPE:]).astype(BF16)


def _mla_attn_body(q_ref, k_ref, vt_ref, o_ref, *, q_sub, kv_chunk):
    n_chunks = k_ref.shape[1] // kv_chunk
    n_sub = q_ref.shape[1] // q_sub
    kv_rows = lambda c: slice(c * kv_chunk, (c + 1) * kv_chunk)
    score = lambda t: _dot_nt(k_ref[0], q_ref[0, t * q_sub:(t + 1) * q_sub, :])
    s_cur = score(0)
    for t in range(n_sub):
        s_next = score(t + 1) if t + 1 < n_sub else None
        m = jnp.max(s_cur, axis=0, keepdims=True)
        l = jnp.zeros((1, q_sub), F32)
        acc = jnp.zeros((vt_ref.shape[1], q_sub), F32)
        for c in range(n_chunks):
            p = jnp.exp2(s_cur[kv_rows(c), :] - m)
            l = l + jnp.sum(p, axis=0, keepdims=True)
            acc = acc + _dot(vt_ref[0, :, kv_rows(c)], p.astype(BF16))
        o_ref[0, t * q_sub:(t + 1) * q_sub, :] = jnp.transpose(acc / l).astype(o_ref.dtype)
        s_cur = s_next


def _rot_cols(w):
    half = w.shape[-1] // 2
    return jnp.concatenate([-w[..., half:], w[..., :half]], axis=-1)


def _mla_mixer(h3, g_pre, w_in, q_norm, w_qb, kv_norm, w_kvb, *, tm=512, tq=2048, q_sub=256,
               kv_chunk=512, tail_tm=512):
    bsz, length, d = h3.shape
    q_rank, kv_rank = q_norm.shape[0], kv_norm.shape[0]
    hd_q = MLA_NOPE + MLA_ROPE
    inv = 1.0 / (ROPE_THETA ** (jnp.arange(0, MLA_ROPE, 2, dtype=F32) / MLA_ROPE))
    ang = jnp.arange(length, dtype=F32)[:, None] * inv[None, :]
    cc = jnp.concatenate([jnp.cos(ang), jnp.cos(ang)], axis=-1)
    ss = jnp.concatenate([jnp.sin(ang), jnp.sin(ang)], axis=-1)
    scale = hd_q ** -0.5 * np.log2(np.e)
    fq = jnp.concatenate([jnp.full((length, MLA_NOPE), scale, F32), cc * scale, ss * scale], axis=-1)
    fk = jnp.concatenate([cc, cc, ss, ss], axis=-1)
    w_kr = w_in[:, q_rank + kv_rank:]
    w_in_ext = jnp.concatenate([w_in[:, :q_rank + kv_rank], w_kr, w_kr, _rot_cols(w_kr), _rot_cols(w_kr)],
                               axis=-1).astype(BF16)
    wq = w_qb.reshape(q_rank, MLA_HEADS, hd_q)
    wq_ext = jnp.concatenate([wq, _rot_cols(wq[..., MLA_NOPE:])], axis=-1).reshape(
        q_rank, MLA_HEADS * MLA_QK).astype(BF16)
    wkv = w_kvb.reshape(kv_rank, MLA_HEADS, MLA_NOPE + MLA_V)
    wkv_perm = jnp.concatenate([wkv[..., :MLA_NOPE].reshape(kv_rank, -1),
                                wkv[..., MLA_NOPE:].reshape(kv_rank, -1)], axis=-1).astype(BF16)
    n_in = w_in_ext.shape[1]
    assert length % tm == 0 and length % tq == 0
    q, k, v = pl.pallas_call(
        functools.partial(_mla_in_body, q_rank=q_rank, kv_rank=kv_rank),
        grid=(bsz, length // tm),
        in_specs=[pl.BlockSpec((1, tm, d), lambda b, i: (b, i, 0)),
                  pl.BlockSpec((tm, MLA_QK), lambda b, i: (i, 0)),
                  pl.BlockSpec((tm, MLA_QK), lambda b, i: (i, 0)),
                  _const_spec((1, d)), _const_spec((d, n_in)), _const_spec((1, q_rank)),
                  _const_spec((q_rank, MLA_HEADS * MLA_QK)), _const_spec((1, kv_rank)),
                  _const_spec((kv_rank, MLA_HEADS * (MLA_NOPE + MLA_V)))],
        out_specs=[pl.BlockSpec((1, tm, MLA_HEADS * MLA_QK), lambda b, i: (b, i, 0)),
                   pl.BlockSpec((1, tm, MLA_HEADS * MLA_QK), lambda b, i: (b, i, 0)),
                   pl.BlockSpec((1, MLA_HEADS * MLA_V, tm), lambda b, i: (b, 0, i))],
        out_shape=[jax.ShapeDtypeStruct((bsz, length, MLA_HEADS * MLA_QK), BF16),
                   jax.ShapeDtypeStruct((bsz, length, MLA_HEADS * MLA_QK), BF16),
                   jax.ShapeDtypeStruct((bsz, MLA_HEADS * MLA_V, length), BF16)],
        compiler_params=_params(("parallel", "parallel")),
        name="mla_in",
    )(h3, fq, fk, _row(g_pre), w_in_ext, _row(q_norm), wq_ext, _row(kv_norm), wkv_perm)
    o = pl.pallas_call(
        functools.partial(_mla_attn_body, q_sub=q_sub, kv_chunk=kv_chunk),
        grid=(bsz, MLA_HEADS, length // tq),
        in_specs=[pl.BlockSpec((1, tq, MLA_QK), lambda b, hd, i: (b, i, hd)),
                  pl.BlockSpec((1, length, MLA_QK), lambda b, hd, i: (b, 0, hd)),
                  pl.BlockSpec((1, MLA_V, length), lambda b, hd, i: (b, hd, 0))],
        out_specs=pl.BlockSpec((1, tq, MLA_V), lambda b, hd, i: (b, i, hd)),
        out_shape=jax.ShapeDtypeStruct((bsz, length, MLA_HEADS * MLA_V), BF16),
        compiler_params=_params(("parallel", "parallel", "parallel")),
        name="mla_attn",
    )(q, k, v)
    width = MLA_HEADS * MLA_V
    return (lambda o_ref: [o_ref[...]]), [(o.reshape(bsz * length, width), width)], [], tail_tm


def _ssd_in_body(h_ref, hp_ref, hn_ref, gpre_ref, wz_ref, wx_ref, wdt_ref, cw_ref, cb_ref,
                 z_ref, xbc_ref, dt_ref, *, col_block):
    i = pl.program_id(1)
    n_i = pl.num_programs(1)
    gpre = gpre_ref[...]
    u = _rms(h_ref[0], gpre).astype(BF16)
    halo = _rms(jnp.concatenate([hp_ref[0], hn_ref[0]], axis=0), gpre).astype(BF16)
    tm = u.shape[0]
    pad = SSM_CONV // 2
    keep_prev = jnp.where(i > 0, 1.0, 0.0)
    keep_next = jnp.where(i < n_i - 1, 1.0, 0.0)
    n_blocks = xbc_ref.shape[2] // col_block
    z_blocks = z_ref.shape[2] // col_block
    cols = lambda j: slice(j * col_block, (j + 1) * col_block)
    proj = lambda j: (_dot(u, wx_ref[:, cols(j)]), _dot(halo, wx_ref[:, cols(j)]))

    def conv(j, x, x_halo):
        x_prev = x_halo[:SUBLANES] * keep_prev
        x_next = x_halo[SUBLANES:] * keep_next
        head = jnp.concatenate([x_prev, x[:SUBLANES]], axis=0)
        tail = jnp.concatenate([x[tm - SUBLANES:], x_next], axis=0)
        acc = x * cw_ref[pad:pad + 1, cols(j)] + cb_ref[:, cols(j)]
        for kk in range(SSM_CONV):
            off = kk - pad
            if off == 0:
                continue
            bulk = pltpu.roll(x, (-off) % tm, axis=0)
            if off < 0:
                first = pltpu.roll(head, (-off) % (2 * SUBLANES), axis=0)[SUBLANES:]
                shifted = jnp.concatenate([first, bulk[SUBLANES:]], axis=0)
            else:
                last = pltpu.roll(tail, (-off) % (2 * SUBLANES), axis=0)[:SUBLANES]
                shifted = jnp.concatenate([bulk[:tm - SUBLANES], last], axis=0)
            acc = acc + shifted * cw_ref[kk:kk + 1, cols(j)]
        xbc_ref[0, :, cols(j)] = _silu(acc)

    nxt = proj(0)
    for j in range(n_blocks):
        cur = nxt
        if j + 1 < n_blocks:
            nxt = proj(j + 1)
        if j < z_blocks:
            z_ref[0, :, cols(j)] = _dot(u, wz_ref[:, cols(j)])
        conv(j, *cur)
    for j in range(n_blocks, z_blocks):
        z_ref[0, :, cols(j)] = _dot(u, wz_ref[:, cols(j)])
    dt_ref[0] = _dot(u, wdt_ref[...])


def _ssd_direction(xbc, dt_raw, bias, a_neg, expand, state_ref, reverse, d_inner):
    q = xbc.shape[0]
    gn = SSM_GROUPS * SSM_STATE
    hpg = d_inner // SSM_HEAD_DIM // SSM_GROUPS
    xs = xbc[:, :d_inner]
    b_in = xbc[:, d_inner:d_inner + gn]
    c_in = xbc[:, d_inner + gn:]
    dt = _softplus(dt_raw + bias)
    a_dt = dt * a_neg
    ti = lax.broadcasted_iota(jnp.int32, (q, q), 0)
    si = lax.broadcasted_iota(jnp.int32, (q, q), 1)
    keep = (si >= ti) if reverse else (si <= ti)
    tri = jnp.where(keep, 1.0, 0.0).astype(BF16)
    cum = _dot_split(tri, a_dt, terms=3)
    cum_t = jnp.transpose(cum)
    dt_t = jnp.transpose(dt)
    edge = cum[0:1, :] if reverse else cum[q - 1:q, :]
    x_end = (xs * _split_dot(dt * jnp.exp(edge - cum), expand)).astype(BF16)
    scale_in = _split_dot(jnp.exp(cum), expand)
    decay_state = _split_dot(jnp.exp(edge), expand)
    xs_b = xs.astype(BF16)
    gw = hpg * SSM_HEAD_DIM
    groups = range(SSM_GROUPS)
    gcols = lambda g: slice(g * gw, (g + 1) * gw)
    bg = [b_in[:, g * SSM_STATE:(g + 1) * SSM_STATE].astype(BF16) for g in groups]
    cg = [c_in[:, g * SSM_STATE:(g + 1) * SSM_STATE].astype(BF16) for g in groups]
    st = [state_ref[:, gcols(g)] for g in groups]
    cb = [_dot_nt(cg[g], bg[g]) for g in groups]
    y_off = [_dot(cg[g], st[g].astype(BF16)) * scale_in[:, gcols(g)] for g in groups]
    lane = lax.broadcasted_iota(jnp.int32, (q, 2 * SSM_HEAD_DIM), 1)
    keep_l = jnp.where(lane < SSM_HEAD_DIM, 1.0, 0.0).astype(BF16)
    keep_r = jnp.where(lane < SSM_HEAD_DIM, 0.0, 1.0).astype(BF16)
    outs = []
    for pr in range(d_inner // (2 * SSM_HEAD_DIM)):
        m_pair = []
        for hd in (2 * pr, 2 * pr + 1):
            seg = cum[:, hd:hd + 1] - cum_t[hd:hd + 1, :]
            decay = jnp.exp(jnp.where(keep, seg, -jnp.inf))
            m_pair.append((cb[hd // hpg] * decay * dt_t[hd:hd + 1, :]).astype(BF16))
        x_pair = xs_b[:, pr * 2 * SSM_HEAD_DIM:(pr + 1) * 2 * SSM_HEAD_DIM]
        x_bd = jnp.concatenate([x_pair * keep_l, x_pair * keep_r], axis=0)
        outs.append(_dot(jnp.concatenate(m_pair, axis=1), x_bd))
    y = jnp.concatenate(outs, axis=-1) + jnp.concatenate(y_off, axis=-1)
    new_state = [st[g] * decay_state[:, gcols(g)] + _dot_tn(bg[g], x_end[:, gcols(g)]) for g in groups]
    for g in groups:
        state_ref[:, gcols(g)] = new_state[g]
    return y


def _ssd_scan_body(xf_ref, dtf_ref, xb_ref, dtb_ref, bias_ref, alog_ref, exp_ref,
                   yf_ref, yb_ref, sf_ref, sb_ref, *, d_inner):
    @pl.when(pl.program_id(1) == 0)
    def _():
        sf_ref[...] = jnp.zeros(sf_ref.shape, F32)
        sb_ref[...] = jnp.zeros(sb_ref.shape, F32)

    nh = d_inner // SSM_HEAD_DIM
    a_neg = -jnp.exp(alog_ref[...])
    expand = exp_ref[...]
    yf_ref[0] = _ssd_direction(xf_ref[0], dtf_ref[0][:, :nh], bias_ref[0:1, :], a_neg[0:1, :], expand,
                               sf_ref, False, d_inner)
    yb_ref[0] = _ssd_direction(xb_ref[0], dtb_ref[0][:, nh:], bias_ref[1:2, :], a_neg[1:2, :], expand,
                               sb_ref, True, d_inner)


def _ssd_out(yf_ref, yb_ref, xs_ref, z_ref, dsk_ref, gn_ref):
    gw = yf_ref.shape[1] // SSM_GROUPS
    blocks = []
    for g in range(SSM_GROUPS):
        cols = slice(g * gw, (g + 1) * gw)
        y = (yf_ref[:, cols] + yb_ref[:, cols] + xs_ref[:, cols] * dsk_ref[:, cols]) * _silu(z_ref[:, cols])
        blocks.append(_rms(y, gn_ref[:, cols]).astype(BF16))
    return blocks


def _ssd_mixer(h3, g_pre, w_in, conv_w, conv_b, a_log, dt_bias, d_skip, gate_norm, *, tm=512,
               col_block=512, tail_tm=256):
    bsz, length, d = h3.shape
    t = bsz * length
    nh = a_log.shape[-1]
    d_inner = nh * SSM_HEAD_DIM
    conv_dim = d_inner + 2 * SSM_GROUPS * SSM_STATE
    q = SSM_CHUNK
    nc = length // q
    assert length % tm == 0 and length % q == 0
    hb = tm // SUBLANES
    nhb = length // SUBLANES
    cur = lambda c: pl.BlockSpec((1, tm, c), lambda b, i: (b, i, 0))
    z, xbc, dt3 = pl.pallas_call(
        functools.partial(_ssd_in_body, col_block=col_block),
        grid=(bsz, length // tm),
        in_specs=[cur(d),
                  pl.BlockSpec((1, SUBLANES, d), lambda b, i: (b, jnp.maximum(i * hb - 1, 0), 0)),
                  pl.BlockSpec((1, SUBLANES, d), lambda b, i: (b, jnp.minimum((i + 1) * hb, nhb - 1), 0)),
                  _const_spec((1, d)), _const_spec((d, d_inner)), _const_spec((d, conv_dim)),
                  _const_spec((d, 2 * nh)), _const_spec((SSM_CONV, conv_dim)), _const_spec((1, conv_dim))],
        out_specs=[cur(d_inner), cur(conv_dim), cur(2 * nh)],
        out_shape=[jax.ShapeDtypeStruct((bsz, length, d_inner), F32),
                   jax.ShapeDtypeStruct((bsz, length, conv_dim), F32),
                   jax.ShapeDtypeStruct((bsz, length, 2 * nh), F32)],
        compiler_params=_params(("parallel", "parallel")),
        name="ssd_in",
    )(h3, h3, h3, _row(g_pre), w_in[:, :d_inner].astype(BF16),
      w_in[:, d_inner:d_inner + conv_dim].astype(BF16), w_in[:, d_inner + conv_dim:].astype(BF16),
      conv_w.astype(F32), _row(conv_b))
    z = z.reshape(t, d_inner)
    expand = jnp.repeat(jnp.eye(nh, dtype=BF16), SSM_HEAD_DIM, axis=1)
    fwd = lambda c: pl.BlockSpec((1, q, c), lambda b, i: (b, i, 0))
    bwd = lambda c: pl.BlockSpec((1, q, c), lambda b, i: (b, nc - 1 - i, 0))
    y_f, y_b = pl.pallas_call(
        functools.partial(_ssd_scan_body, d_inner=d_inner),
        grid=(bsz, nc),
        in_specs=[fwd(conv_dim), fwd(2 * nh), bwd(conv_dim), bwd(2 * nh),
                  _const_spec((2, nh)), _const_spec((2, nh)), _const_spec((nh, d_inner))],
        out_specs=[fwd(d_inner), bwd(d_inner)],
        out_shape=[jax.ShapeDtypeStruct((bsz, length, d_inner), F32)] * 2,
        scratch_shapes=[pltpu.VMEM((SSM_STATE, d_inner), F32)] * 2,
        compiler_params=_params(("parallel", "arbitrary")),
        name="ssd_scan",
    )(xbc, dt3, xbc, dt3, dt_bias.astype(F32), a_log.astype(F32), expand)
    row_inputs = [(y_f.reshape(t, d_inner), d_inner), (y_b.reshape(t, d_inner), d_inner),
                  (xbc.reshape(t, conv_dim), d_inner), (z, d_inner)]
    consts = [_row(jnp.repeat(d_skip.astype(F32), SSM_HEAD_DIM)), _row(gate_norm)]
    return _ssd_out, row_inputs, consts, tail_tm


def _rwkv_in_body(h_ref, hp_ref, hn_ref, gpre_ref, mu_ref, wr_ref, wk_ref, wv_ref, w0_ref, w1_ref,
                  w2f_ref, w2b_ref, a0_ref, a1_ref, a2_ref, g1_ref, g2_ref, kk_ref, ka_ref,
                  rk_ref, seg_ref, segt_ref, tri_ref, up_ref,
                  v_out, bv_out, g_out, pendf_out, pendb_out, *dir_outs):
    i = pl.program_id(1)
    n_i = pl.num_programs(1)
    gpre = gpre_ref[...]
    x = _rms(h_ref[0], gpre)
    tm = x.shape[0]
    prev_row = _rms(hp_ref[0][SUBLANES - 1:SUBLANES, :], gpre) * jnp.where(i > 0, 1.0, 0.0)
    next_row = _rms(hn_ref[0][0:1, :], gpre) * jnp.where(i < n_i - 1, 1.0, 0.0)
    t = lax.broadcasted_iota(jnp.int32, x.shape, 0)
    x_prev = jnp.where(t == 0, prev_row, pltpu.roll(x, 1, axis=0))
    x_next = jnp.where(t == tm - 1, next_row, pltpu.roll(x, tm - 1, axis=0))
    xx = 0.5 * (x_prev + x_next) - x
    mix = lambda j: (x + xx * mu_ref[j:j + 1, :]).astype(BF16)
    xr, xw, xk, xv, xa, xg = (mix(j) for j in range(6))
    r = _dot(xr, wr_ref[...])
    k = _dot(xk, wk_ref[...])
    v = _dot(xv, wv_ref[...])
    lora_w = jnp.tanh(_dot(xw, w1_ref[...])).astype(BF16)
    log_decay = lambda w: -np.float32(np.exp(-0.5)) * jax.nn.sigmoid(w)
    lw_f = log_decay(w0_ref[0:1, :] + _dot(lora_w, w2f_ref[...]))
    lw_b = log_decay(w0_ref[1:2, :] + _dot(lora_w, w2b_ref[...]))
    a = jax.nn.sigmoid(a0_ref[...] + _dot(_dot(xa, a1_ref[...]).astype(BF16), a2_ref[...]))
    g_out[0] = _dot(jax.nn.sigmoid(_dot(xg, g1_ref[...])).astype(BF16), g2_ref[...])
    kk = k * kk_ref[...]
    norm = jnp.maximum(jnp.sqrt(_split_dot(kk * kk, seg_ref[...])), 1e-12)
    kk = kk * _split_dot(1.0 / norm, segt_ref[...])
    k = k * (1.0 + (a - 1.0) * ka_ref[...])
    rm = -kk
    add = kk * a
    v_out[0] = v.astype(v_out.dtype)
    bonus = _split_dot(_split_dot(r * k * rk_ref[...], seg_ref[...]), segt_ref[...])
    bv_out[0] = bonus * v
    c = RWKV_CHUNK
    for lw, cum, last, pend_out, outs in (
            (lw_f, _dot_split(tri_ref[...], lw_f), c - 1, pendf_out, dir_outs[:4]),
            (lw_b, lw_b + _dot_split(up_ref[...], lw_b), 0, pendb_out, dir_outs[4:])):
        e_neg = jnp.exp(-cum)
        values = (r * jnp.exp(cum), rm * jnp.exp(cum - lw), add * e_neg, k * e_neg)
        for o_ref, val in zip(outs, values):
            o_ref[0] = val.astype(o_ref.dtype)
        for m in range(tm // c):
            pend_out[0, m] = jnp.exp(cum[m * c + last:m * c + last + 1, :])


def _rwkv_chunk(r_t, rm_t, add_t, k_t, v_b, p_end, s_all, n_forward):
    c = r_t.shape[0]
    hd = RWKV_HEAD_DIM
    pair = 2 * hd
    assert c == hd and pair == LANES
    lane = lax.broadcasted_iota(jnp.int32, (c, pair), 1)
    ti = lax.broadcasted_iota(jnp.int32, (c, pair), 0)
    si = lane & (hd - 1)
    left = lane < hd
    keep_l = jnp.where(left, 1.0, 0.0).astype(BF16)
    keep_r = jnp.where(left, 0.0, 1.0).astype(BF16)
    bd = lambda x: jnp.concatenate([x * keep_l, x * keep_r], axis=0)
    eye = jnp.where(si == ti, 1.0, 0.0).astype(F32)
    pairs = range(r_t.shape[1] // pair)
    incl_of = lambda j: (si <= ti) if j < n_forward else (si >= ti)
    strict_of = lambda j: (si < ti) if j < n_forward else (si > ti)
    col = lambda x, j: x[:, j * pair:(j + 1) * pair]
    rows = lambda *xs: jnp.concatenate(xs, axis=0)
    v_bd = [bd(col(v_b, j)) for j in pairs]
    s0 = [col(s_all, j) for j in pairs]
    s0_bd = [bd(s.astype(BF16)) for s in s0]
    sc = [_dot_nt(rows(col(rm_t, j), col(r_t, j)), rows(bd(col(add_t, j)), bd(col(k_t, j)))) for j in pairs]
    n_mat = [jnp.where(strict_of(j), sc[j][:c, :pair], 0.0) for j in pairs]
    a_ak = [jnp.where(strict_of(j), sc[j][:c, pair:], 0.0).astype(BF16) for j in pairs]
    a_rb = [jnp.where(incl_of(j), sc[j][c:, :pair], 0.0).astype(BF16) for j in pairs]
    a_rk = [jnp.where(incl_of(j), sc[j][c:, pair:], 0.0).astype(BF16) for j in pairs]
    xkv = [_dot(rows(a_ak[j], a_rk[j]), v_bd[j]) for j in pairs]
    rs = [_dot_nt(col(r_t, j), s0_bd[j]) for j in pairs]
    inv = [eye + n_mat[j] for j in pairs]
    pw = [n_mat[j].astype(BF16) for j in pairs]
    pw = [_dot(pw[j], bd(pw[j])).astype(BF16) for j in pairs]
    span = 2
    while span < c:
        if 2 * span < c:
            both = [_dot(rows(pw[j], inv[j].astype(BF16)), bd(pw[j])) for j in pairs]
            pw = [both[j][:c].astype(BF16) for j in pairs]
            inv = [inv[j] + both[j][c:] for j in pairs]
        else:
            inv = [inv[j] + _dot(inv[j].astype(BF16), bd(pw[j])) for j in pairs]
        span *= 2
    wu = [_dot(inv[j].astype(BF16),
               jnp.concatenate([bd(col(rm_t, j)), bd(xkv[j][:c].astype(BF16))], axis=1)) for j in pairs]
    sa_b = [(_dot_nt(wu[j][:, :pair].astype(BF16), s0_bd[j]) + wu[j][:, pair:]).astype(BF16)
            for j in pairs]
    outs = [rs[j] + _dot(a_rb[j], bd(sa_b[j])) + xkv[j][c:] for j in pairs]
    full = [_dot_tn(rows(sa_b[j], col(v_b, j)), rows(col(add_t, j), col(k_t, j))) for j in pairs]
    states = [(s0[j] + jnp.where(left, full[j][:hd], full[j][hd:])) * col(p_end, j) for j in pairs]
    return jnp.concatenate(outs, axis=-1), jnp.concatenate(states, axis=-1)


def _rwkv_scan_body(*refs):
    f_in, b_in = refs[:6], refs[6:12]
    yf_ref, yb_ref, sf_ref, sb_ref = refs[12:]

    @pl.when(pl.program_id(2) == 0)
    def _():
        sf_ref[...] = jnp.zeros(sf_ref.shape, F32)
        sb_ref[...] = jnp.zeros(sb_ref.shape, F32)

    width = sf_ref.shape[1]
    both = lambda f, b: jnp.concatenate([f, b], axis=-1)
    ops = [both(f[0], b[0]) for f, b in zip(f_in[:5], b_in[:5])]
    y, s_new = _rwkv_chunk(*ops, both(f_in[5][0, 0], b_in[5][0, 0]), both(sf_ref[...], sb_ref[...]),
                           width // LANES)
    yf_ref[0] = y[:, :width]
    yb_ref[0] = y[:, width:]
    sf_ref[...] = s_new[:, :width]
    sb_ref[...] = s_new[:, width:]


def _rwkv_out(yf_ref, yb_ref, bv_ref, g_ref, lnw_ref, lnb_ref, seg_ref, segt_ref):
    width = 2 * LANES
    blocks = []
    for j in range(yf_ref.shape[1] // width):
        cols = slice(j * width, (j + 1) * width)
        seg, segt = seg_ref[cols, :], segt_ref[:, cols]
        head_mean = lambda x: _split_dot(_split_dot(x, seg) * (1.0 / RWKV_HEAD_DIM), segt)
        y = yf_ref[:, cols] + yb_ref[:, cols]
        cen = y - head_mean(y)
        var = head_mean(cen * cen)
        y = cen * lax.rsqrt(var + RWKV_LN_EPS) * lnw_ref[:, cols] + lnb_ref[:, cols]
        blocks.append(((y + bv_ref[:, cols]) * g_ref[:, cols]).astype(BF16))
    return blocks


def _pad_to(w, axis, size):
    pad = [(0, 0)] * w.ndim
    pad[axis] = (0, size - w.shape[axis])
    return jnp.pad(w, pad)


def _rwkv_mixer(h3, g_pre, mu, w_rkv, w0, w1, w2, a0, a1, a2, g1, g2, k_k, k_a, r_k, ln_w, ln_b, *,
                tm=256, heads_per_step=16, tail_tm=512):
    bsz, length, d = h3.shape
    t = bsz * length
    n_heads = d // RWKV_HEAD_DIM
    c = RWKV_CHUNK
    nc = length // c
    hw = heads_per_step * RWKV_HEAD_DIM
    assert length % tm == 0 and length % c == 0 and n_heads % heads_per_step == 0
    assert n_heads <= LANES
    rank_w = w1.shape[-1]
    w1_cat = jnp.concatenate([w1[0], w1[1]], axis=-1).astype(BF16)
    w2f = jnp.concatenate([w2[0], jnp.zeros_like(w2[1])], axis=0).astype(BF16)
    w2b = jnp.concatenate([jnp.zeros_like(w2[0]), w2[1]], axis=0).astype(BF16)
    a1p = _pad_to(a1, 1, LANES).astype(BF16)
    a2p = _pad_to(a2, 0, LANES).astype(BF16)
    g_rank = -(-g1.shape[1] // LANES) * LANES
    g1p = _pad_to(g1, 1, g_rank).astype(BF16)
    g2p = _pad_to(g2, 0, g_rank).astype(BF16)
    seg = _pad_to(jnp.repeat(jnp.eye(n_heads, dtype=BF16), RWKV_HEAD_DIM, axis=0), 1, LANES)
    segt = seg.T
    hb = tm // SUBLANES
    nhb = length // SUBLANES
    pos = np.arange(tm)
    same_chunk = (pos[:, None] // c) == (pos[None, :] // c)
    tri = jnp.asarray(same_chunk & (pos[None, :] <= pos[:, None]), BF16)
    up = jnp.asarray(same_chunk & (pos[None, :] > pos[:, None]), BF16)
    cur = pl.BlockSpec((1, tm, d), lambda b, i: (b, i, 0))
    pend_spec = pl.BlockSpec((1, tm // c, 1, d), lambda b, i: (b, i, 0, 0))
    pend_shape = jax.ShapeDtypeStruct((bsz, nc, 1, d), F32)
    act = jax.ShapeDtypeStruct((bsz, length, d), BF16)
    outs = pl.pallas_call(
        _rwkv_in_body,
        grid=(bsz, length // tm),
        in_specs=[cur,
                  pl.BlockSpec((1, SUBLANES, d), lambda b, i: (b, jnp.maximum(i * hb - 1, 0), 0)),
                  pl.BlockSpec((1, SUBLANES, d), lambda b, i: (b, jnp.minimum((i + 1) * hb, nhb - 1), 0)),
                  _const_spec((1, d)), _const_spec((6, d)),
                  _const_spec((d, d)), _const_spec((d, d)), _const_spec((d, d)),
                  _const_spec((2, d)), _const_spec((d, 2 * rank_w)),
                  _const_spec((2 * rank_w, d)), _const_spec((2 * rank_w, d)),
                  _const_spec((1, d)), _const_spec((d, LANES)), _const_spec((LANES, d)),
                  _const_spec((d, g_rank)), _const_spec((g_rank, d)),
                  _const_spec((1, d)), _const_spec((1, d)), _const_spec((1, d)),
                  _const_spec((d, LANES)), _const_spec((LANES, d)),
                  _const_spec((tm, tm)), _const_spec((tm, tm))],
        out_specs=[cur, cur, cur, pend_spec, pend_spec] + [cur] * 8,
        out_shape=[act, jax.ShapeDtypeStruct((bsz, length, d), F32),
                   jax.ShapeDtypeStruct((bsz, length, d), F32), pend_shape, pend_shape] + [act] * 8,
        compiler_params=_params(("parallel", "parallel")),
        name="rwkv_in",
    )(h3, h3, h3, _row(g_pre), mu.astype(F32), w_rkv[0].astype(BF16), w_rkv[1].astype(BF16),
      w_rkv[2].astype(BF16), w0.astype(F32), w1_cat, w2f, w2b, _row(a0), a1p, a2p, g1p, g2p,
      _row(k_k), _row(k_a), _row(r_k), seg, segt, tri, up)
    v, bv, g, pend_f, pend_b = outs[:5]
    ops_f, ops_b = outs[5:9], outs[9:]
    fwd = pl.BlockSpec((1, c, hw), lambda b, hg, i: (b, i, hg))
    bwd = pl.BlockSpec((1, c, hw), lambda b, hg, i: (b, nc - 1 - i, hg))
    fwd_p = pl.BlockSpec((1, 1, 1, hw), lambda b, hg, i: (b, i, 0, hg))
    bwd_p = pl.BlockSpec((1, 1, 1, hw), lambda b, hg, i: (b, nc - 1 - i, 0, hg))
    y_f, y_b = pl.pallas_call(
        _rwkv_scan_body,
        grid=(bsz, n_heads // heads_per_step, nc),
        in_specs=[fwd] * 5 + [fwd_p] + [bwd] * 5 + [bwd_p],
        out_specs=[fwd, bwd],
        out_shape=[jax.ShapeDtypeStruct((bsz, length, d), F32)] * 2,
        scratch_shapes=[pltpu.VMEM((RWKV_HEAD_DIM, hw), F32)] * 2,
        compiler_params=_params(("parallel", "parallel", "arbitrary")),
        name="rwkv_scan",
    )(*ops_f, v, pend_f, *ops_b, v, pend_b)
    row_inputs = [(x.reshape(t, d), d) for x in (y_f, y_b, bv, g)]
    return _rwkv_out, row_inputs, [_row(ln_w), _row(ln_b), seg, segt], tail_tm


def kernel(x_prompt, x_sample, mix_pre_norm, mix_post_norm, ffn_pre_norm, ffn_post_norm, mla_w_in, mla_q_norm, mla_w_qb, mla_kv_norm, mla_w_kvb, mla_w_o, ssm_w_in, ssm_conv_w, ssm_conv_b, ssm_a_log, ssm_dt_bias, ssm_d_skip, ssm_gate_norm, ssm_w_out, rwkv_mu, rwkv_w_rkv, rwkv_w0, rwkv_w1, rwkv_w2, rwkv_a0, rwkv_a1, rwkv_a2, rwkv_g1, rwkv_g2, rwkv_k_k, rwkv_k_a, rwkv_r_k, rwkv_ln_w, rwkv_ln_b, rwkv_w_o, ffn_w_in, ffn_w_out):
    depth = mix_pre_norm.shape[0]

    def trunk(x):
        bsz, length, d = x.shape
        h = x.reshape(bsz * length, d)
        for i in range(depth):
            kind, j = i % N_MIXERS, i // N_MIXERS
            h3 = h.reshape(bsz, length, d)
            if kind == 0:
                mixer = _mla_mixer(h3, mix_pre_norm[i], mla_w_in[j], mla_q_norm[j], mla_w_qb[j],
                                   mla_kv_norm[j], mla_w_kvb[j])
                w_o = mla_w_o[j]
            elif kind == 1:
                mixer = _ssd_mixer(h3, mix_pre_norm[i], ssm_w_in[j], ssm_conv_w[j], ssm_conv_b[j],
                                   ssm_a_log[j], ssm_dt_bias[j], ssm_d_skip[j], ssm_gate_norm[j])
                w_o = ssm_w_out[j]
            else:
                mixer = _rwkv_mixer(h3, mix_pre_norm[i], rwkv_mu[j], rwkv_w_rkv[j], rwkv_w0[j],
                                    rwkv_w1[j], rwkv_w2[j], rwkv_a0[j], rwkv_a1[j], rwkv_a2[j],
                                    rwkv_g1[j], rwkv_g2[j], rwkv_k_k[j], rwkv_k_a[j], rwkv_r_k[j],
                                    rwkv_ln_w[j], rwkv_ln_b[j])
                w_o = rwkv_w_o[j]
            h = _layer_tail(h, mixer, w_o, mix_post_norm[i], ffn_pre_norm[i], ffn_w_in[i],
                            ffn_w_out[i], ffn_post_norm[i])
        return h.reshape(bsz, length, d)

    return (trunk(x_prompt), trunk(x_sample))
```

```python
import functools

import jax
import jax.numpy as jnp
import numpy as np
from jax import lax
from jax.experimental import pallas as pl
from jax.experimental.pallas import tpu as pltpu

F32 = jnp.float32
BF16 = jnp.bfloat16

NORM_EPS = 1e-6
N_MIXERS = 3

VMEM_LIMIT_BYTES = 56 * 1024 * 1024
LANES = 128
SUBLANES = 8

MLA_HEADS = 8
MLA_NOPE = 128
MLA_ROPE = 64
MLA_V = 128
MLA_QK = MLA_NOPE + 2 * MLA_ROPE
ROPE_THETA = 10000.0

SSM_HEAD_DIM = 64
SSM_GROUPS = 4
SSM_STATE = 128
SSM_CONV = 5
SSM_CHUNK = 128

RWKV_HEAD_DIM = 64
RWKV_LN_EPS = 64e-5
RWKV_CHUNK = 64


def _rms(x, g, eps=NORM_EPS):
    return x * lax.rsqrt(jnp.mean(x * x, axis=-1, keepdims=True) + eps) * g


def _dot(a, b):
    return jnp.dot(a, b, preferred_element_type=F32)


def _dot_nt(a, b):
    return lax.dot_general(a, b, (((1,), (1,)), ((), ())), preferred_element_type=F32)


def _dot_tn(a, b):
    return lax.dot_general(a, b, (((0,), (0,)), ((), ())), preferred_element_type=F32)


def _bf16_terms(x, terms):
    pieces = []
    for _ in range(terms):
        p = x.astype(BF16)
        pieces.append(p)
        x = x - p.astype(F32)
    return pieces


def _dot_split(m, x, terms=2):
    return sum(_dot(m, p) for p in _bf16_terms(x, terms))


def _split_dot(x, m, terms=2):
    return sum(_dot(p, m) for p in _bf16_terms(x, terms))


def _silu(x):
    return x * jax.nn.sigmoid(x)


def _softplus(x):
    return jnp.maximum(x, 0.0) + jnp.log(1.0 + jnp.exp(-jnp.abs(x)))


def _const_spec(shape):
    nd = len(shape)
    return pl.BlockSpec(shape, lambda *_: (0,) * nd, pipeline_mode=pl.Buffered(1))


def _params(semantics):
    return pltpu.CompilerParams(dimension_semantics=semantics, vmem_limit_bytes=VMEM_LIMIT_BYTES)


def _row(v):
    return v.reshape(1, -1).astype(F32)


def _tail_body(*refs, n_mixer_refs, mixer_out, ff_chunk):
    h_ref = refs[0]
    wo_ref, gpost_ref, gpre_ref, win_ref, wout_ref, gfpost_ref, o_ref = refs[1 + n_mixer_refs:]
    d_ff = wout_ref.shape[0]
    m = jnp.zeros(h_ref.shape, F32)
    k0 = 0
    for blk in mixer_out(*refs[1:1 + n_mixer_refs]):
        m = m + _dot(blk, wo_ref[k0:k0 + blk.shape[1], :])
        k0 += blk.shape[1]
    h1 = h_ref[...] + _rms(m, gpost_ref[...])
    u = _rms(h1, gpre_ref[...]).astype(BF16)
    n_chunks = d_ff // ff_chunk
    gate_up = lambda c: (_dot(u, win_ref[:, c * ff_chunk:(c + 1) * ff_chunk]),
                         _dot(u, win_ref[:, d_ff + c * ff_chunk:d_ff + (c + 1) * ff_chunk]))
    acc = jnp.zeros(h1.shape, F32)
    nxt = gate_up(0)
    for c in range(n_chunks):
        gate, up = nxt
        if c + 1 < n_chunks:
            nxt = gate_up(c + 1)
        act = (_silu(gate) * up).astype(BF16)
        acc = acc + _dot(act, wout_ref[c * ff_chunk:(c + 1) * ff_chunk, :])
    o_ref[...] = h1 + _rms(acc, gfpost_ref[...])


def _layer_tail(h, mixer, w_o, g_post, g_pre, w_in, w_out, g_fpost, *, ff_chunk=256):
    mixer_out, row_inputs, const_inputs, tm = mixer
    t, d = h.shape
    ky = w_o.shape[0]
    d_ff = w_out.shape[0]
    assert t % tm == 0 and d_ff % ff_chunk == 0 and ff_chunk % LANES == 0
    row = lambda c: pl.BlockSpec((tm, c), lambda i: (i, 0))
    return pl.pallas_call(
        functools.partial(_tail_body, n_mixer_refs=len(row_inputs) + len(const_inputs),
                          mixer_out=mixer_out, ff_chunk=ff_chunk),
        grid=(t // tm,),
        in_specs=[row(d)] + [row(width) for _, width in row_inputs]
        + [_const_spec(c.shape) for c in const_inputs]
        + [_const_spec((ky, d)), _const_spec((1, d)), _const_spec((1, d)),
           _const_spec((d, 2 * d_ff)), _const_spec((d_ff, d)), _const_spec((1, d))],
        out_specs=row(d),
        out_shape=jax.ShapeDtypeStruct((t, d), F32),
        compiler_params=_params(("parallel",)),
        name="layer_tail",
    )(h, *[a for a, _ in row_inputs], *const_inputs, w_o.astype(BF16), _row(g_post), _row(g_pre),
      w_in.astype(BF16), w_out.astype(BF16), _row(g_fpost))


def _mla_in_body(h_ref, fq_ref, fk_ref, gpre_ref, win_ref, qn_ref, wq_ref, kvn_ref, wkv_ref,
                 q_ref, k_ref, v_ref, *, q_rank, kv_rank):
    u = _rms(h_ref[0], gpre_ref[...]).astype(BF16)
    lat = _dot(u, win_ref[...])
    q_lat = _rms(lat[:, :q_rank], qn_ref[...]).astype(BF16)
    kv_lat = _rms(lat[:, q_rank:q_rank + kv_rank], kvn_ref[...]).astype(BF16)
    kr = lat[:, q_rank + kv_rank:q_rank + kv_rank + LANES]
    kr_rot = lat[:, q_rank + kv_rank + LANES:q_rank + kv_rank + 2 * LANES]
    fk = fk_ref[...]
    k_rope = (kr * fk[:, :LANES] + kr_rot * fk[:, LANES:]).astype(BF16)
    fq = fq_ref[...]
    q = _dot(q_lat, wq_ref[...])
    kv = _dot(kv_lat, wkv_ref[...])
    for hd in range(MLA_HEADS):
        q_ref[0, :, hd * MLA_QK:(hd + 1) * MLA_QK] = (
            q[:, hd * MLA_QK:(hd + 1) * MLA_QK] * fq).astype(BF16)
        k_ref[0, :, hd * MLA_QK:hd * MLA_QK + MLA_NOPE] = (
            kv[:, hd * MLA_NOPE:(hd + 1) * MLA_NOPE].astype(BF16))
        k_ref[0, :, hd * MLA_QK + MLA_NOPE:(hd + 1) * MLA_QK] = k_rope
    v_ref[0] = jnp.transpose(kv[:, MLA_HEADS * MLA_NOPE:]).astype(BF16)


def _mla_attn_body(q_ref, k_ref, vt_ref, o_ref, *, q_sub, kv_chunk):
    n_chunks = k_ref.shape[1] // kv_chunk
    n_sub = q_ref.shape[1] // q_sub
    kv_rows = lambda c: slice(c * kv_chunk, (c + 1) * kv_chunk)
    score = lambda t: _dot_nt(k_ref[0], q_ref[0, t * q_sub:(t + 1) * q_sub, :])
    s_cur = score(0)
    for t in range(n_sub):
        s_next = score(t + 1) if t + 1 < n_sub else None
        m = jnp.max(s_cur, axis=0, keepdims=True)
        l = jnp.zeros((1, q_sub), F32)
        acc = jnp.zeros((vt_ref.shape[1], q_sub), F32)
        for c in range(n_chunks):
            p = jnp.exp2(s_cur[kv_rows(c), :] - m)
            l = l + jnp.sum(p, axis=0, keepdims=True)
            acc = acc + _dot(vt_ref[0, :, kv_rows(c)], p.astype(BF16))
        o_ref[0, t * q_sub:(t + 1) * q_sub, :] = jnp.transpose(acc / l).astype(o_ref.dtype)
        s_cur = s_next


def _rot_cols(w):
    half = w.shape[-1] // 2
    return jnp.concatenate([-w[..., half:], w[..., :half]], axis=-1)


def _mla_mixer(h3, g_pre, w_in, q_norm, w_qb, kv_norm, w_kvb, *, tm=512, tq=2048, q_sub=256,
               kv_chunk=512, tail_tm=512):
    bsz, length, d = h3.shape
    q_rank, kv_rank = q_norm.shape[0], kv_norm.shape[0]
    hd_q = MLA_NOPE + MLA_ROPE
    inv = 1.0 / (ROPE_THETA ** (jnp.arange(0, MLA_ROPE, 2, dtype=F32) / MLA_ROPE))
    ang = jnp.arange(length, dtype=F32)[:, None] * inv[None, :]
    cc = jnp.concatenate([jnp.cos(ang), jnp.cos(ang)], axis=-1)
    ss = jnp.concatenate([jnp.sin(ang), jnp.sin(ang)], axis=-1)
    scale = hd_q ** -0.5 * np.log2(np.e)
    fq = jnp.concatenate([jnp.full((length, MLA_NOPE), scale, F32), cc * scale, ss * scale], axis=-1)
    fk = jnp.concatenate([cc, cc, ss, ss], axis=-1)
    w_kr = w_in[:, q_rank + kv_rank:]
    w_in_ext = jnp.concatenate([w_in[:, :q_rank + kv_rank], w_kr, w_kr, _rot_cols(w_kr), _rot_cols(w_kr)],
                               axis=-1).astype(BF16)
    wq = w_qb.reshape(q_rank, MLA_HEADS, hd_q)
    wq_ext = jnp.concatenate([wq, _rot_cols(wq[..., MLA_NOPE:])], axis=-1).reshape(
        q_rank, MLA_HEADS * MLA_QK).astype(BF16)
    wkv = w_kvb.reshape(kv_rank, MLA_HEADS, MLA_NOPE + MLA_V)
    wkv_perm = jnp.concatenate([wkv[..., :MLA_NOPE].reshape(kv_rank, -1),
                                wkv[..., MLA_NOPE:].reshape(kv_rank, -1)], axis=-1).astype(BF16)
    n_in = w_in_ext.shape[1]
    assert length % tm == 0 and length % tq == 0
    q, k, v = pl.pallas_call(
        functools.partial(_mla_in_body, q_rank=q_rank, kv_rank=kv_rank),
        grid=(bsz, length // tm),
        in_specs=[pl.BlockSpec((1, tm, d), lambda b, i: (b, i, 0)),
                  pl.BlockSpec((tm, MLA_QK), lambda b, i: (i, 0)),
                  pl.BlockSpec((tm, MLA_QK), lambda b, i: (i, 0)),
                  _const_spec((1, d)), _const_spec((d, n_in)), _const_spec((1, q_rank)),
                  _const_spec((q_rank, MLA_HEADS * MLA_QK)), _const_spec((1, kv_rank)),
                  _const_spec((kv_rank, MLA_HEADS * (MLA_NOPE + MLA_V)))],
        out_specs=[pl.BlockSpec((1, tm, MLA_HEADS * MLA_QK), lambda b, i: (b, i, 0)),
                   pl.BlockSpec((1, tm, MLA_HEADS * MLA_QK), lambda b, i: (b, i, 0)),
                   pl.BlockSpec((1, MLA_HEADS * MLA_V, tm), lambda b, i: (b, 0, i))],
        out_shape=[jax.ShapeDtypeStruct((bsz, length, MLA_HEADS * MLA_QK), BF16),
                   jax.ShapeDtypeStruct((bsz, length, MLA_HEADS * MLA_QK), BF16),
                   jax.ShapeDtypeStruct((bsz, MLA_HEADS * MLA_V, length), BF16)],
        compiler_params=_params(("parallel", "parallel")),
        name="mla_in",
    )(h3, fq, fk, _row(g_pre), w_in_ext, _row(q_norm), wq_ext, _row(kv_norm), wkv_perm)
    o = pl.pallas_call(
        functools.partial(_mla_attn_body, q_sub=q_sub, kv_chunk=kv_chunk),
        grid=(bsz, MLA_HEADS, length // tq),
        in_specs=[pl.BlockSpec((1, tq, MLA_QK), lambda b, hd, i: (b, i, hd)),
                  pl.BlockSpec((1, length, MLA_QK), lambda b, hd, i: (b, 0, hd)),
                  pl.BlockSpec((1, MLA_V, length), lambda b, hd, i: (b, hd, 0))],
        out_specs=pl.BlockSpec((1, tq, MLA_V), lambda b, hd, i: (b, i, hd)),
        out_shape=jax.ShapeDtypeStruct((bsz, length, MLA_HEADS * MLA_V), BF16),
        compiler_params=_params(("parallel", "parallel", "parallel")),
        name="mla_attn",
    )(q, k, v)
    width = MLA_HEADS * MLA_V
    return (lambda o_ref: [o_ref[...]]), [(o.reshape(bsz * length, width), width)], [], tail_tm


def _ssd_in_body(h_ref, hp_ref, hn_ref, gpre_ref, wz_ref, wx_ref, wdt_ref, cw_ref, cb_ref,
                 z_ref, xbc_ref, dt_ref, *, col_block):
    i = pl.program_id(1)
    n_i = pl.num_programs(1)
    gpre = gpre_ref[...]
    u = _rms(h_ref[0], gpre).astype(BF16)
    halo = _rms(jnp.concatenate([hp_ref[0], hn_ref[0]], axis=0), gpre).astype(BF16)
    tm = u.shape[0]
    pad = SSM_CONV // 2
    keep_prev = jnp.where(i > 0, 1.0, 0.0)
    keep_next = jnp.where(i < n_i - 1, 1.0, 0.0)
    n_blocks = xbc_ref.shape[2] // col_block
    z_blocks = z_ref.shape[2] // col_block
    cols = lambda j: slice(j * col_block, (j + 1) * col_block)
    proj = lambda j: (_dot(u, wx_ref[:, cols(j)]), _dot(halo, wx_ref[:, cols(j)]))

    def conv(j, x, x_halo):
        x_prev = x_halo[:SUBLANES] * keep_prev
        x_next = x_halo[SUBLANES:] * keep_next
        head = jnp.concatenate([x_prev, x[:SUBLANES]], axis=0)
        tail = jnp.concatenate([x[tm - SUBLANES:], x_next], axis=0)
        acc = x * cw_ref[pad:pad + 1, cols(j)] + cb_ref[:, cols(j)]
        for kk in range(SSM_CONV):
            off = kk - pad
            if off == 0:
                continue
            bulk = pltpu.roll(x, (-off) % tm, axis=0)
            if off < 0:
                first = pltpu.roll(head, (-off) % (2 * SUBLANES), axis=0)[SUBLANES:]
                shifted = jnp.concatenate([first, bulk[SUBLANES:]], axis=0)
            else:
                last = pltpu.roll(tail, (-off) % (2 * SUBLANES), axis=0)[:SUBLANES]
                shifted = jnp.concatenate([bulk[:tm - SUBLANES], last], axis=0)
            acc = acc + shifted * cw_ref[kk:kk + 1, cols(j)]
        xbc_ref[0, :, cols(j)] = _silu(acc)

    nxt = proj(0)
    for j in range(n_blocks):
        cur = nxt
        if j + 1 < n_blocks:
            nxt = proj(j + 1)
        if j < z_blocks:
            z_ref[0, :, cols(j)] = _dot(u, wz_ref[:, cols(j)])
        conv(j, *cur)
    for j in range(n_blocks, z_blocks):
        z_ref[0, :, cols(j)] = _dot(u, wz_ref[:, cols(j)])
    dt_ref[0] = _dot(u, wdt_ref[...])


def _ssd_direction(xbc, dt_raw, bias, a_neg, expand, state_ref, reverse, d_inner):
    q = xbc.shape[0]
    gn = SSM_GROUPS * SSM_STATE
    hpg = d_inner // SSM_HEAD_DIM // SSM_GROUPS
    xs = xbc[:, :d_inner]
    b_in = xbc[:, d_inner:d_inner + gn]
    c_in = xbc[:, d_inner + gn:]
    dt = _softplus(dt_raw + bias)
    a_dt = dt * a_neg
    ti = lax.broadcasted_iota(jnp.int32, (q, q), 0)
    si = lax.broadcasted_iota(jnp.int32, (q, q), 1)
    keep = (si >= ti) if reverse else (si <= ti)
    tri = jnp.where(keep, 1.0, 0.0).astype(BF16)
    cum = _dot_split(tri, a_dt, terms=3)
    cum_t = jnp.transpose(cum)
    dt_t = jnp.transpose(dt)
    edge = cum[0:1, :] if reverse else cum[q - 1:q, :]
    x_end = (xs * _split_dot(dt * jnp.exp(edge - cum), expand)).astype(BF16)
    scale_in = _split_dot(jnp.exp(cum), expand)
    decay_state = _split_dot(jnp.exp(edge), expand)
    xs_b = xs.astype(BF16)
    gw = hpg * SSM_HEAD_DIM
    groups = range(SSM_GROUPS)
    gcols = lambda g: slice(g * gw, (g + 1) * gw)
    bg = [b_in[:, g * SSM_STATE:(g + 1) * SSM_STATE].astype(BF16) for g in groups]
    cg = [c_in[:, g * SSM_STATE:(g + 1) * SSM_STATE].astype(BF16) for g in groups]
    st = [state_ref[:, gcols(g)] for g in groups]
    cb = [_dot_nt(cg[g], bg[g]) for g in groups]
    y_off = [_dot(cg[g], st[g].astype(BF16)) * scale_in[:, gcols(g)] for g in groups]
    lane = lax.broadcasted_iota(jnp.int32, (q, 2 * SSM_HEAD_DIM), 1)
    keep_l = jnp.where(lane < SSM_HEAD_DIM, 1.0, 0.0).astype(BF16)
    keep_r = jnp.where(lane < SSM_HEAD_DIM, 0.0, 1.0).astype(BF16)
    outs = []
    for pr in range(d_inner // (2 * SSM_HEAD_DIM)):
        m_pair = []
        for hd in (2 * pr, 2 * pr + 1):
            seg = cum[:, hd:hd + 1] - cum_t[hd:hd + 1, :]
            decay = jnp.exp(jnp.where(keep, seg, -jnp.inf))
            m_pair.append((cb[hd // hpg] * decay * dt_t[hd:hd + 1, :]).astype(BF16))
        x_pair = xs_b[:, pr * 2 * SSM_HEAD_DIM:(pr + 1) * 2 * SSM_HEAD_DIM]
        x_bd = jnp.concatenate([x_pair * keep_l, x_pair * keep_r], axis=0)
        outs.append(_dot(jnp.concatenate(m_pair, axis=1), x_bd))
    y = jnp.concatenate(outs, axis=-1) + jnp.concatenate(y_off, axis=-1)
    new_state = [st[g] * decay_state[:, gcols(g)] + _dot_tn(bg[g], x_end[:, gcols(g)]) for g in groups]
    for g in groups:
        state_ref[:, gcols(g)] = new_state[g]
    return y


def _ssd_scan_body(xf_ref, dtf_ref, xb_ref, dtb_ref, bias_ref, alog_ref, exp_ref,
                   yf_ref, yb_ref, sf_ref, sb_ref, *, d_inner):
    @pl.when(pl.program_id(1) == 0)
    def _():
        sf_ref[...] = jnp.zeros(sf_ref.shape, F32)
        sb_ref[...] = jnp.zeros(sb_ref.shape, F32)

    nh = d_inner // SSM_HEAD_DIM
    a_neg = -jnp.exp(alog_ref[...])
    expand = exp_ref[...]
    q = SSM_CHUNK
    n_sub = xf_ref.shape[1] // q
    for sub in range(n_sub):
        rf = slice(sub * q, (sub + 1) * q)
        rb = slice((n_sub - 1 - sub) * q, (n_sub - sub) * q)
        yf_ref[0, rf, :] = _ssd_direction(xf_ref[0, rf, :], dtf_ref[0, rf, :][:, :nh], bias_ref[0:1, :],
                                          a_neg[0:1, :], expand, sf_ref, False, d_inner)
        yb_ref[0, rb, :] = _ssd_direction(xb_ref[0, rb, :], dtb_ref[0, rb, :][:, nh:], bias_ref[1:2, :],
                                          a_neg[1:2, :], expand, sb_ref, True, d_inner)


def _ssd_out(yf_ref, yb_ref, xs_ref, z_ref, dsk_ref, gn_ref):
    gw = yf_ref.shape[1] // SSM_GROUPS
    blocks = []
    for g in range(SSM_GROUPS):
        cols = slice(g * gw, (g + 1) * gw)
        y = (yf_ref[:, cols] + yb_ref[:, cols] + xs_ref[:, cols] * dsk_ref[:, cols]) * _silu(z_ref[:, cols])
        blocks.append(_rms(y, gn_ref[:, cols]).astype(BF16))
    return blocks


def _ssd_mixer(h3, g_pre, w_in, conv_w, conv_b, a_log, dt_bias, d_skip, gate_norm, *, tm=512,
               col_block=512, chunks_per_step=2, tail_tm=256):
    bsz, length, d = h3.shape
    t = bsz * length
    nh = a_log.shape[-1]
    d_inner = nh * SSM_HEAD_DIM
    conv_dim = d_inner + 2 * SSM_GROUPS * SSM_STATE
    q = SSM_CHUNK
    nc = length // q
    assert length % tm == 0 and length % (q * chunks_per_step) == 0
    hb = tm // SUBLANES
    nhb = length // SUBLANES
    cur = lambda c: pl.BlockSpec((1, tm, c), lambda b, i: (b, i, 0))
    z, xbc, dt3 = pl.pallas_call(
        functools.partial(_ssd_in_body, col_block=col_block),
        grid=(bsz, length // tm),
        in_specs=[cur(d),
                  pl.BlockSpec((1, SUBLANES, d), lambda b, i: (b, jnp.maximum(i * hb - 1, 0), 0)),
                  pl.BlockSpec((1, SUBLANES, d), lambda b, i: (b, jnp.minimum((i + 1) * hb, nhb - 1), 0)),
                  _const_spec((1, d)), _const_spec((d, d_inner)), _const_spec((d, conv_dim)),
                  _const_spec((d, 2 * nh)), _const_spec((SSM_CONV, conv_dim)), _const_spec((1, conv_dim))],
        out_specs=[cur(d_inner), cur(conv_dim), cur(2 * nh)],
        out_shape=[jax.ShapeDtypeStruct((bsz, length, d_inner), F32),
                   jax.ShapeDtypeStruct((bsz, length, conv_dim), F32),
                   jax.ShapeDtypeStruct((bsz, length, 2 * nh), F32)],
        compiler_params=_params(("parallel", "parallel")),
        name="ssd_in",
    )(h3, h3, h3, _row(g_pre), w_in[:, :d_inner].astype(BF16),
      w_in[:, d_inner:d_inner + conv_dim].astype(BF16), w_in[:, d_inner + conv_dim:].astype(BF16),
      conv_w.astype(F32), _row(conv_b))
    z = z.reshape(t, d_inner)
    expand = jnp.repeat(jnp.eye(nh, dtype=BF16), SSM_HEAD_DIM, axis=1)
    steps = nc // chunks_per_step
    rows = chunks_per_step * q
    fwd = lambda c: pl.BlockSpec((1, rows, c), lambda b, i: (b, i, 0))
    bwd = lambda c: pl.BlockSpec((1, rows, c), lambda b, i: (b, steps - 1 - i, 0))
    y_f, y_b = pl.pallas_call(
        functools.partial(_ssd_scan_body, d_inner=d_inner),
        grid=(bsz, steps),
        in_specs=[fwd(conv_dim), fwd(2 * nh), bwd(conv_dim), bwd(2 * nh),
                  _const_spec((2, nh)), _const_spec((2, nh)), _const_spec((nh, d_inner))],
        out_specs=[fwd(d_inner), bwd(d_inner)],
        out_shape=[jax.ShapeDtypeStruct((bsz, length, d_inner), F32)] * 2,
        scratch_shapes=[pltpu.VMEM((SSM_STATE, d_inner), F32)] * 2,
        compiler_params=_params(("parallel", "arbitrary")),
        name="ssd_scan",
    )(xbc, dt3, xbc, dt3, dt_bias.astype(F32), a_log.astype(F32), expand)
    row_inputs = [(y_f.reshape(t, d_inner), d_inner), (y_b.reshape(t, d_inner), d_inner),
                  (xbc.reshape(t, conv_dim), d_inner), (z, d_inner)]
    consts = [_row(jnp.repeat(d_skip.astype(F32), SSM_HEAD_DIM)), _row(gate_norm)]
    return _ssd_out, row_inputs, consts, tail_tm


def _rwkv_in_body(h_ref, hp_ref, hn_ref, gpre_ref, mu_ref, wr_ref, wk_ref, wv_ref, w0_ref, w1_ref,
                  w2f_ref, w2b_ref, a0_ref, a1_ref, a2_ref, g1_ref, g2_ref, kk_ref, ka_ref,
                  rk_ref, seg_ref, segt_ref, tri_ref, up_ref,
                  v_out, bv_out, g_out, pendf_out, pendb_out, *dir_outs):
    i = pl.program_id(1)
    n_i = pl.num_programs(1)
    gpre = gpre_ref[...]
    x = _rms(h_ref[0], gpre)
    tm = x.shape[0]
    prev_row = _rms(hp_ref[0][SUBLANES - 1:SUBLANES, :], gpre) * jnp.where(i > 0, 1.0, 0.0)
    next_row = _rms(hn_ref[0][0:1, :], gpre) * jnp.where(i < n_i - 1, 1.0, 0.0)
    t = lax.broadcasted_iota(jnp.int32, x.shape, 0)
    x_prev = jnp.where(t == 0, prev_row, pltpu.roll(x, 1, axis=0))
    x_next = jnp.where(t == tm - 1, next_row, pltpu.roll(x, tm - 1, axis=0))
    xx = 0.5 * (x_prev + x_next) - x
    mix = lambda j: (x + xx * mu_ref[j:j + 1, :]).astype(BF16)
    xr, xw, xk, xv, xa, xg = (mix(j) for j in range(6))
    r = _dot(xr, wr_ref[...])
    k = _dot(xk, wk_ref[...])
    v = _dot(xv, wv_ref[...])
    lora_w = jnp.tanh(_dot(xw, w1_ref[...])).astype(BF16)
    log_decay = lambda w: -np.float32(np.exp(-0.5)) * jax.nn.sigmoid(w)
    lw_f = log_decay(w0_ref[0:1, :] + _dot(lora_w, w2f_ref[...]))
    lw_b = log_decay(w0_ref[1:2, :] + _dot(lora_w, w2b_ref[...]))
    a = jax.nn.sigmoid(a0_ref[...] + _dot(_dot(xa, a1_ref[...]).astype(BF16), a2_ref[...]))
    g_out[0] = _dot(jax.nn.sigmoid(_dot(xg, g1_ref[...])).astype(BF16), g2_ref[...])
    kk = k * kk_ref[...]
    norm = jnp.maximum(jnp.sqrt(_split_dot(kk * kk, seg_ref[...])), 1e-12)
    kk = kk * _split_dot(1.0 / norm, segt_ref[...])
    k = k * (1.0 + (a - 1.0) * ka_ref[...])
    rm = -kk
    add = kk * a
    v_out[0] = v.astype(v_out.dtype)
    bonus = _split_dot(_split_dot(r * k * rk_ref[...], seg_ref[...]), segt_ref[...])
    bv_out[0] = bonus * v
    c = RWKV_CHUNK
    for lw, cum, last, pend_out, outs in (
            (lw_f, _dot_split(tri_ref[...], lw_f), c - 1, pendf_out, dir_outs[:4]),
            (lw_b, lw_b + _dot_split(up_ref[...], lw_b), 0, pendb_out, dir_outs[4:])):
        e_neg = jnp.exp(-cum)
        values = (r * jnp.exp(cum), rm * jnp.exp(cum - lw), add * e_neg, k * e_neg)
        for o_ref, val in zip(outs, values):
            o_ref[0] = val.astype(o_ref.dtype)
        for m in range(tm // c):
            pend_out[0, m] = jnp.exp(cum[m * c + last:m * c + last + 1, :])


def _rwkv_chunk(r_t, rm_t, add_t, k_t, v_b, p_end, s_all, n_forward):
    c = r_t.shape[0]
    hd = RWKV_HEAD_DIM
    pair = 2 * hd
    assert c == hd and pair == LANES
    lane = lax.broadcasted_iota(jnp.int32, (c, pair), 1)
    ti = lax.broadcasted_iota(jnp.int32, (c, pair), 0)
    si = lane & (hd - 1)
    left = lane < hd
    keep_l = jnp.where(left, 1.0, 0.0).astype(BF16)
    keep_r = jnp.where(left, 0.0, 1.0).astype(BF16)
    bd = lambda x: jnp.concatenate([x * keep_l, x * keep_r], axis=0)
    eye = jnp.where(si == ti, 1.0, 0.0).astype(F32)
    pairs = range(r_t.shape[1] // pair)
    incl_of = lambda j: (si <= ti) if j < n_forward else (si >= ti)
    strict_of = lambda j: (si < ti) if j < n_forward else (si > ti)
    col = lambda x, j: x[:, j * pair:(j + 1) * pair]
    rows = lambda *xs: jnp.concatenate(xs, axis=0)
    v_bd = [bd(col(v_b, j)) for j in pairs]
    s0 = [col(s_all, j) for j in pairs]
    s0_bd = [bd(s.astype(BF16)) for s in s0]
    sc = [_dot_nt(rows(col(rm_t, j), col(r_t, j)), rows(bd(col(add_t, j)), bd(col(k_t, j)))) for j in pairs]
    n_mat = [jnp.where(strict_of(j), sc[j][:c, :pair], 0.0) for j in pairs]
    a_ak = [jnp.where(strict_of(j), sc[j][:c, pair:], 0.0).astype(BF16) for j in pairs]
    a_rb = [jnp.where(incl_of(j), sc[j][c:, :pair], 0.0).astype(BF16) for j in pairs]
    a_rk = [jnp.where(incl_of(j), sc[j][c:, pair:], 0.0).astype(BF16) for j in pairs]
    xkv = [_dot(rows(a_ak[j], a_rk[j]), v_bd[j]) for j in pairs]
    rs = [_dot_nt(col(r_t, j), s0_bd[j]) for j in pairs]
    inv = [eye + n_mat[j] for j in pairs]
    pw = [n_mat[j].astype(BF16) for j in pairs]
    pw = [_dot(pw[j], bd(pw[j])).astype(BF16) for j in pairs]
    span = 2
    while span < c:
        if 2 * span < c:
            both = [_dot(rows(pw[j], inv[j].astype(BF16)), bd(pw[j])) for j in pairs]
            pw = [both[j][:c].astype(BF16) for j in pairs]
            inv = [inv[j] + both[j][c:] for j in pairs]
        else:
            inv = [inv[j] + _dot(inv[j].astype(BF16), bd(pw[j])) for j in pairs]
        span *= 2
    wu = [_dot(inv[j].astype(BF16),
               jnp.concatenate([bd(col(rm_t, j)), bd(xkv[j][:c].astype(BF16))], axis=1)) for j in pairs]
    sa_b = [(_dot_nt(wu[j][:, :pair].astype(BF16), s0_bd[j]) + wu[j][:, pair:]).astype(BF16)
            for j in pairs]
    outs = [rs[j] + _dot(a_rb[j], bd(sa_b[j])) + xkv[j][c:] for j in pairs]
    full = [_dot_tn(rows(sa_b[j], col(v_b, j)), rows(col(add_t, j), col(k_t, j))) for j in pairs]
    states = [(s0[j] + jnp.where(left, full[j][:hd], full[j][hd:])) * col(p_end, j) for j in pairs]
    return jnp.concatenate(outs, axis=-1), jnp.concatenate(states, axis=-1)


def _rwkv_scan_body(*refs):
    f_in, b_in = refs[:6], refs[6:12]
    yf_ref, yb_ref, sf_ref, sb_ref = refs[12:]

    @pl.when(pl.program_id(2) == 0)
    def _():
        sf_ref[...] = jnp.zeros(sf_ref.shape, F32)
        sb_ref[...] = jnp.zeros(sb_ref.shape, F32)

    width = sf_ref.shape[1]
    c = RWKV_CHUNK
    n_sub = f_in[0].shape[1] // c
    both = lambda f, b: jnp.concatenate([f, b], axis=-1)
    state = both(sf_ref[...], sb_ref[...])
    for sub in range(n_sub):
        rf = slice(sub * c, (sub + 1) * c)
        rb = slice((n_sub - 1 - sub) * c, (n_sub - sub) * c)
        ops = [both(f[0, rf, :], b[0, rb, :]) for f, b in zip(f_in[:5], b_in[:5])]
        p_end = both(f_in[5][0, sub], b_in[5][0, n_sub - 1 - sub])
        y, state = _rwkv_chunk(*ops, p_end, state, width // LANES)
        yf_ref[0, rf, :] = y[:, :width]
        yb_ref[0, rb, :] = y[:, width:]
    sf_ref[...] = state[:, :width]
    sb_ref[...] = state[:, width:]


def _rwkv_out(yf_ref, yb_ref, bv_ref, g_ref, lnw_ref, lnb_ref, seg_ref, segt_ref):
    width = 2 * LANES
    blocks = []
    for j in range(yf_ref.shape[1] // width):
        cols = slice(j * width, (j + 1) * width)
        seg, segt = seg_ref[cols, :], segt_ref[:, cols]
        head_mean = lambda x: _split_dot(_split_dot(x, seg) * (1.0 / RWKV_HEAD_DIM), segt)
        y = yf_ref[:, cols] + yb_ref[:, cols]
        cen = y - head_mean(y)
        var = head_mean(cen * cen)
        y = cen * lax.rsqrt(var + RWKV_LN_EPS) * lnw_ref[:, cols] + lnb_ref[:, cols]
        blocks.append(((y + bv_ref[:, cols]) * g_ref[:, cols]).astype(BF16))
    return blocks


def _pad_to(w, axis, size):
    pad = [(0, 0)] * w.ndim
    pad[axis] = (0, size - w.shape[axis])
    return jnp.pad(w, pad)


def _rwkv_mixer(h3, g_pre, mu, w_rkv, w0, w1, w2, a0, a1, a2, g1, g2, k_k, k_a, r_k, ln_w, ln_b, *,
                tm=256, heads_per_step=16, chunks_per_step=4, tail_tm=512):
    bsz, length, d = h3.shape
    t = bsz * length
    n_heads = d // RWKV_HEAD_DIM
    c = RWKV_CHUNK
    nc = length // c
    hw = heads_per_step * RWKV_HEAD_DIM
    assert length % tm == 0 and length % (c * chunks_per_step) == 0 and n_heads % heads_per_step == 0
    assert n_heads <= LANES
    rank_w = w1.shape[-1]
    w1_cat = jnp.concatenate([w1[0], w1[1]], axis=-1).astype(BF16)
    w2f = jnp.concatenate([w2[0], jnp.zeros_like(w2[1])], axis=0).astype(BF16)
    w2b = jnp.concatenate([jnp.zeros_like(w2[0]), w2[1]], axis=0).astype(BF16)
    a1p = _pad_to(a1, 1, LANES).astype(BF16)
    a2p = _pad_to(a2, 0, LANES).astype(BF16)
    g_rank = -(-g1.shape[1] // LANES) * LANES
    g1p = _pad_to(g1, 1, g_rank).astype(BF16)
    g2p = _pad_to(g2, 0, g_rank).astype(BF16)
    seg = _pad_to(jnp.repeat(jnp.eye(n_heads, dtype=BF16), RWKV_HEAD_DIM, axis=0), 1, LANES)
    segt = seg.T
    hb = tm // SUBLANES
    nhb = length // SUBLANES
    pos = np.arange(tm)
    same_chunk = (pos[:, None] // c) == (pos[None, :] // c)
    tri = jnp.asarray(same_chunk & (pos[None, :] <= pos[:, None]), BF16)
    up = jnp.asarray(same_chunk & (pos[None, :] > pos[:, None]), BF16)
    cur = pl.BlockSpec((1, tm, d), lambda b, i: (b, i, 0))
    pend_spec = pl.BlockSpec((1, tm // c, 1, d), lambda b, i: (b, i, 0, 0))
    pend_shape = jax.ShapeDtypeStruct((bsz, nc, 1, d), F32)
    act = jax.ShapeDtypeStruct((bsz, length, d), BF16)
    outs = pl.pallas_call(
        _rwkv_in_body,
        grid=(bsz, length // tm),
        in_specs=[cur,
                  pl.BlockSpec((1, SUBLANES, d), lambda b, i: (b, jnp.maximum(i * hb - 1, 0), 0)),
                  pl.BlockSpec((1, SUBLANES, d), lambda b, i: (b, jnp.minimum((i + 1) * hb, nhb - 1), 0)),
                  _const_spec((1, d)), _const_spec((6, d)),
                  _const_spec((d, d)), _const_spec((d, d)), _const_spec((d, d)),
                  _const_spec((2, d)), _const_spec((d, 2 * rank_w)),
                  _const_spec((2 * rank_w, d)), _const_spec((2 * rank_w, d)),
                  _const_spec((1, d)), _const_spec((d, LANES)), _const_spec((LANES, d)),
                  _const_spec((d, g_rank)), _const_spec((g_rank, d)),
                  _const_spec((1, d)), _const_spec((1, d)), _const_spec((1, d)),
                  _const_spec((d, LANES)), _const_spec((LANES, d)),
                  _const_spec((tm, tm)), _const_spec((tm, tm))],
        out_specs=[cur, cur, cur, pend_spec, pend_spec] + [cur] * 8,
        out_shape=[act, jax.ShapeDtypeStruct((bsz, length, d), F32),
                   jax.ShapeDtypeStruct((bsz, length, d), F32), pend_shape, pend_shape] + [act] * 8,
        compiler_params=_params(("parallel", "parallel")),
        name="rwkv_in",
    )(h3, h3, h3, _row(g_pre), mu.astype(F32), w_rkv[0].astype(BF16), w_rkv[1].astype(BF16),
      w_rkv[2].astype(BF16), w0.astype(F32), w1_cat, w2f, w2b, _row(a0), a1p, a2p, g1p, g2p,
      _row(k_k), _row(k_a), _row(r_k), seg, segt, tri, up)
    v, bv, g, pend_f, pend_b = outs[:5]
    ops_f, ops_b = outs[5:9], outs[9:]
    steps = nc // chunks_per_step
    rows = chunks_per_step * c
    fwd = pl.BlockSpec((1, rows, hw), lambda b, hg, i: (b, i, hg))
    bwd = pl.BlockSpec((1, rows, hw), lambda b, hg, i: (b, steps - 1 - i, hg))
    fwd_p = pl.BlockSpec((1, chunks_per_step, 1, hw), lambda b, hg, i: (b, i, 0, hg))
    bwd_p = pl.BlockSpec((1, chunks_per_step, 1, hw), lambda b, hg, i: (b, steps - 1 - i, 0, hg))
    y_f, y_b = pl.pallas_call(
        _rwkv_scan_body,
        grid=(bsz, n_heads // heads_per_step, steps),
        in_specs=[fwd] * 5 + [fwd_p] + [bwd] * 5 + [bwd_p],
        out_specs=[fwd, bwd],
        out_shape=[jax.ShapeDtypeStruct((bsz, length, d), F32)] * 2,
        scratch_shapes=[pltpu.VMEM((RWKV_HEAD_DIM, hw), F32)] * 2,
        compiler_params=_params(("parallel", "parallel", "arbitrary")),
        name="rwkv_scan",
    )(*ops_f, v, pend_f, *ops_b, v, pend_b)
    row_inputs = [(x.reshape(t, d), d) for x in (y_f, y_b, bv, g)]
    return _rwkv_out, row_inputs, [_row(ln_w), _row(ln_b), seg, segt], tail_tm


def kernel(x_prompt, x_sample, mix_pre_norm, mix_post_norm, ffn_pre_norm, ffn_post_norm, mla_w_in, mla_q_norm, mla_w_qb, mla_kv_norm, mla_w_kvb, mla_w_o, ssm_w_in, ssm_conv_w, ssm_conv_b, ssm_a_log, ssm_dt_bias, ssm_d_skip, ssm_gate_norm, ssm_w_out, rwkv_mu, rwkv_w_rkv, rwkv_w0, rwkv_w1, rwkv_w2, rwkv_a0, rwkv_a1, rwkv_a2, rwkv_g1, rwkv_g2, rwkv_k_k, rwkv_k_a, rwkv_r_k, rwkv_ln_w, rwkv_ln_b, rwkv_w_o, ffn_w_in, ffn_w_out):
    depth = mix_pre_norm.shape[0]

    def trunk(x):
        bsz, length, d = x.shape
        h = x.reshape(bsz * length, d)
        for i in range(depth):
            kind, j = i % N_MIXERS, i // N_MIXERS
            h3 = h.reshape(bsz, length, d)
            if kind == 0:
                mixer = _mla_mixer(h3, mix_pre_norm[i], mla_w_in[j], mla_q_norm[j], mla_w_qb[j],
                                   mla_kv_norm[j], mla_w_kvb[j])
                w_o = mla_w_o[j]
            elif kind == 1:
                mixer = _ssd_mixer(h3, mix_pre_norm[i], ssm_w_in[j], ssm_conv_w[j], ssm_conv_b[j],
                                   ssm_a_log[j], ssm_dt_bias[j], ssm_d_skip[j], ssm_gate_norm[j])
                w_o = ssm_w_out[j]
            else:
                mixer = _rwkv_mixer(h3, mix_pre_norm[i], rwkv_mu[j], rwkv_w_rkv[j], rwkv_w0[j],
                                    rwkv_w1[j], rwkv_w2[j], rwkv_a0[j], rwkv_a1[j], rwkv_a2[j],
                                    rwkv_g1[j], rwkv_g2[j], rwkv_k_k[j], rwkv_k_a[j], rwkv_r_k[j],
                                    rwkv_ln_w[j], rwkv_ln_b[j])
                w_o = rwkv_w_o[j]
            h = _layer_tail(h, mixer, w_o, mix_post_norm[i], ffn_pre_norm[i], ffn_w_in[i],
                            ffn_w_out[i], ffn_post_norm[i])
        return h.reshape(bsz, length, d)

    return (trunk(x_prompt), trunk(x_sample))
```

```python
import functools

import jax
import jax.numpy as jnp
import numpy as np
from jax import lax
from jax.experimental import pallas as pl
from jax.experimental.pallas import tpu as pltpu

F32 = jnp.float32
BF16 = jnp.bfloat16

NORM_EPS = 1e-6
N_MIXERS = 3

VMEM_LIMIT_BYTES = 56 * 1024 * 1024
LANES = 128
SUBLANES = 8

MLA_HEADS = 8
MLA_NOPE = 128
MLA_ROPE = 64
MLA_V = 128
MLA_QK = MLA_NOPE + 2 * MLA_ROPE
ROPE_THETA = 10000.0

SSM_HEAD_DIM = 64
SSM_GROUPS = 4
SSM_STATE = 128
SSM_CONV = 5
SSM_CHUNK = 128

RWKV_HEAD_DIM = 64
RWKV_LN_EPS = 64e-5
RWKV_CHUNK = 64


def _rms(x, g, eps=NORM_EPS):
    return x * lax.rsqrt(jnp.mean(x * x, axis=-1, keepdims=True) + eps) * g


def _dot(a, b):
    return jnp.dot(a, b, preferred_element_type=F32)


def _dot_nt(a, b):
    return lax.dot_general(a, b, (((1,), (1,)), ((), ())), preferred_element_type=F32)


def _dot_tn(a, b):
    return lax.dot_general(a, b, (((0,), (0,)), ((), ())), preferred_element_type=F32)


def _bf16_terms(x, terms):
    pieces = []
    for _ in range(terms):
        p = x.astype(BF16)
        pieces.append(p)
        x = x - p.astype(F32)
    return pieces


def _dot_split(m, x, terms=2):
    return sum(_dot(m, p) for p in _bf16_terms(x, terms))


def _split_dot(x, m, terms=2):
    return sum(_dot(p, m) for p in _bf16_terms(x, terms))


def _silu(x):
    return x * jax.nn.sigmoid(x)


def _softplus(x):
    return jnp.maximum(x, 0.0) + jnp.log(1.0 + jnp.exp(-jnp.abs(x)))


def _const_spec(shape):
    nd = len(shape)
    return pl.BlockSpec(shape, lambda *_: (0,) * nd, pipeline_mode=pl.Buffered(1))


def _params(semantics):
    return pltpu.CompilerParams(dimension_semantics=semantics, vmem_limit_bytes=VMEM_LIMIT_BYTES)


def _row(v):
    return v.reshape(1, -1).astype(F32)


def _tail_body(*refs, n_mixer_refs, mixer_out, ff_chunk):
    h_ref = refs[0]
    wo_ref, gpost_ref, gpre_ref, win_ref, wout_ref, gfpost_ref, o_ref = refs[1 + n_mixer_refs:]
    d_ff = wout_ref.shape[0]
    m = jnp.zeros(h_ref.shape, F32)
    k0 = 0
    for blk in mixer_out(*refs[1:1 + n_mixer_refs]):
        m = m + _dot(blk, wo_ref[k0:k0 + blk.shape[1], :])
        k0 += blk.shape[1]
    h1 = h_ref[...] + _rms(m, gpost_ref[...])
    u = _rms(h1, gpre_ref[...]).astype(BF16)
    n_chunks = d_ff // ff_chunk
    gate_up = lambda c: (_dot(u, win_ref[:, c * ff_chunk:(c + 1) * ff_chunk]),
                         _dot(u, win_ref[:, d_ff + c * ff_chunk:d_ff + (c + 1) * ff_chunk]))
    acc = jnp.zeros(h1.shape, F32)
    nxt = gate_up(0)
    for c in range(n_chunks):
        gate, up = nxt
        if c + 1 < n_chunks:
            nxt = gate_up(c + 1)
        act = (_silu(gate) * up).astype(BF16)
        acc = acc + _dot(act, wout_ref[c * ff_chunk:(c + 1) * ff_chunk, :])
    o_ref[...] = h1 + _rms(acc, gfpost_ref[...])


def _layer_tail(h, mixer, w_o, g_post, g_pre, w_in, w_out, g_fpost, *, ff_chunk=256):
    mixer_out, row_inputs, const_inputs, tm = mixer
    t, d = h.shape
    ky = w_o.shape[0]
    d_ff = w_out.shape[0]
    assert t % tm == 0 and d_ff % ff_chunk == 0 and ff_chunk % LANES == 0
    row = lambda c: pl.BlockSpec((tm, c), lambda i: (i, 0))
    return pl.pallas_call(
        functools.partial(_tail_body, n_mixer_refs=len(row_inputs) + len(const_inputs),
                          mixer_out=mixer_out, ff_chunk=ff_chunk),
        grid=(t // tm,),
        in_specs=[row(d)] + [row(width) for _, width in row_inputs]
        + [_const_spec(c.shape) for c in const_inputs]
        + [_const_spec((ky, d)), _const_spec((1, d)), _const_spec((1, d)),
           _const_spec((d, 2 * d_ff)), _const_spec((d_ff, d)), _const_spec((1, d))],
        out_specs=row(d),
        out_shape=jax.ShapeDtypeStruct((t, d), F32),
        compiler_params=_params(("parallel",)),
        name="layer_tail",
    )(h, *[a for a, _ in row_inputs], *const_inputs, w_o.astype(BF16), _row(g_post), _row(g_pre),
      w_in.astype(BF16), w_out.astype(BF16), _row(g_fpost))


def _mla_in_body(h_ref, fq_ref, fk_ref, gpre_ref, win_ref, qn_ref, wq_ref, kvn_ref, wkv_ref,
                 q_ref, k_ref, v_ref, *, q_rank, kv_rank):
    u = _rms(h_ref[0], gpre_ref[...]).astype(BF16)
    lat = _dot(u, win_ref[...])
    q_lat = _rms(lat[:, :q_rank], qn_ref[...]).astype(BF16)
    kv_lat = _rms(lat[:, q_rank:q_rank + kv_rank], kvn_ref[...]).astype(BF16)
    kr = lat[:, q_rank + kv_rank:q_rank + kv_rank + LANES]
    kr_rot = lat[:, q_rank + kv_rank + LANES:q_rank + kv_rank + 2 * LANES]
    fk = fk_ref[...]
    k_rope = (kr * fk[:, :LANES] + kr_rot * fk[:, LANES:]).astype(BF16)
    fq = fq_ref[...]
    q = _dot(q_lat, wq_ref[...])
    kv = _dot(kv_lat, wkv_ref[...])
    for hd in range(MLA_HEADS):
        q_ref[0, :, hd * MLA_QK:(hd + 1) * MLA_QK] = (
            q[:, hd * MLA_QK:(hd + 1) * MLA_QK] * fq).astype(BF16)
        k_ref[0, :, hd * MLA_QK:hd * MLA_QK + MLA_NOPE] = (
            kv[:, hd * MLA_NOPE:(hd + 1) * MLA_NOPE].astype(BF16))
        k_ref[0, :, hd * MLA_QK + MLA_NOPE:(hd + 1) * MLA_QK] = k_rope
    v_ref[0] = jnp.transpose(kv[:, MLA_HEADS * MLA_NOPE:]).astype(BF16)


def _mla_attn_body(q_ref, k_ref, vt_ref, o_ref, *, q_sub, kv_chunk):
    n_chunks = k_ref.shape[1] // kv_chunk
    n_sub = q_ref.shape[1] // q_sub
    kv_rows = lambda c: slice(c * kv_chunk, (c + 1) * kv_chunk)
    score = lambda t: _dot_nt(k_ref[0], q_ref[0, t * q_sub:(t + 1) * q_sub, :])
    s_cur = score(0)
    for t in range(n_sub):
        s_next = score(t + 1) if t + 1 < n_sub else None
        m = jnp.max(s_cur, axis=0, keepdims=True)
        l = jnp.zeros((1, q_sub), F32)
        acc = jnp.zeros((vt_ref.shape[1], q_sub), F32)
        for c in range(n_chunks):
            p = jnp.exp2(s_cur[kv_rows(c), :] - m)
            l = l + jnp.sum(p, axis=0, keepdims=True)
            acc = acc + _dot(vt_ref[0, :, kv_rows(c)], p.astype(BF16))
        o_ref[0, t * q_sub:(t + 1) * q_sub, :] = jnp.transpose(acc / l).astype(o_ref.dtype)
        s_cur = s_next


def _rot_cols(w):
    half = w.shape[-1] // 2
    return jnp.concatenate([-w[..., half:], w[..., :half]], axis=-1)


def _mla_mixer(h3, g_pre, w_in, q_norm, w_qb, kv_norm, w_kvb, *, tm=512, tq=2048, q_sub=256,
               kv_chunk=512, tail_tm=512):
    bsz, length, d = h3.shape
    q_rank, kv_rank = q_norm.shape[0], kv_norm.shape[0]
    hd_q = MLA_NOPE + MLA_ROPE
    inv = 1.0 / (ROPE_THETA ** (jnp.arange(0, MLA_ROPE, 2, dtype=F32) / MLA_ROPE))
    ang = jnp.arange(length, dtype=F32)[:, None] * inv[None, :]
    cc = jnp.concatenate([jnp.cos(ang), jnp.cos(ang)], axis=-1)
    ss = jnp.concatenate([jnp.sin(ang), jnp.sin(ang)], axis=-1)
    scale = hd_q ** -0.5 * np.log2(np.e)
    fq = jnp.concatenate([jnp.full((length, MLA_NOPE), scale, F32), cc * scale, ss * scale], axis=-1)
    fk = jnp.concatenate([cc, cc, ss, ss], axis=-1)
    w_kr = w_in[:, q_rank + kv_rank:]
    w_in_ext = jnp.concatenate([w_in[:, :q_rank + kv_rank], w_kr, w_kr, _rot_cols(w_kr), _rot_cols(w_kr)],
                               axis=-1).astype(BF16)
    wq = w_qb.reshape(q_rank, MLA_HEADS, hd_q)
    wq_ext = jnp.concatenate([wq, _rot_cols(wq[..., MLA_NOPE:])], axis=-1).reshape(
        q_rank, MLA_HEADS * MLA_QK).astype(BF16)
    wkv = w_kvb.reshape(kv_rank, MLA_HEADS, MLA_NOPE + MLA_V)
    wkv_perm = jnp.concatenate([wkv[..., :MLA_NOPE].reshape(kv_rank, -1),
                                wkv[..., MLA_NOPE:].reshape(kv_rank, -1)], axis=-1).astype(BF16)
    n_in = w_in_ext.shape[1]
    assert length % tm == 0 and length % tq == 0
    q, k, v = pl.pallas_call(
        functools.partial(_mla_in_body, q_rank=q_rank, kv_rank=kv_rank),
        grid=(bsz, length // tm),
        in_specs=[pl.BlockSpec((1, tm, d), lambda b, i: (b, i, 0)),
                  pl.BlockSpec((tm, MLA_QK), lambda b, i: (i, 0)),
                  pl.BlockSpec((tm, MLA_QK), lambda b, i: (i, 0)),
                  _const_spec((1, d)), _const_spec((d, n_in)), _const_spec((1, q_rank)),
                  _const_spec((q_rank, MLA_HEADS * MLA_QK)), _const_spec((1, kv_rank)),
                  _const_spec((kv_rank, MLA_HEADS * (MLA_NOPE + MLA_V)))],
        out_specs=[pl.BlockSpec((1, tm, MLA_HEADS * MLA_QK), lambda b, i: (b, i, 0)),
                   pl.BlockSpec((1, tm, MLA_HEADS * MLA_QK), lambda b, i: (b, i, 0)),
                   pl.BlockSpec((1, MLA_HEADS * MLA_V, tm), lambda b, i: (b, 0, i))],
        out_shape=[jax.ShapeDtypeStruct((bsz, length, MLA_HEADS * MLA_QK), BF16),
                   jax.ShapeDtypeStruct((bsz, length, MLA_HEADS * MLA_QK), BF16),
                   jax.ShapeDtypeStruct((bsz, MLA_HEADS * MLA_V, length), BF16)],
        compiler_params=_params(("parallel", "parallel")),
        name="mla_in",
    )(h3, fq, fk, _row(g_pre), w_in_ext, _row(q_norm), wq_ext, _row(kv_norm), wkv_perm)
    o = pl.pallas_call(
        functools.partial(_mla_attn_body, q_sub=q_sub, kv_chunk=kv_chunk),
        grid=(bsz, MLA_HEADS, length // tq),
        in_specs=[pl.BlockSpec((1, tq, MLA_QK), lambda b, hd, i: (b, i, hd)),
                  pl.BlockSpec((1, length, MLA_QK), lambda b, hd, i: (b, 0, hd)),
                  pl.BlockSpec((1, MLA_V, length), lambda b, hd, i: (b, hd, 0))],
        out_specs=pl.BlockSpec((1, tq, MLA_V), lambda b, hd, i: (b, i, hd)),
        out_shape=jax.ShapeDtypeStruct((bsz, length, MLA_HEADS * MLA_V), BF16),
        compiler_params=_params(("parallel", "parallel", "parallel")),
        name="mla_attn",
    )(q, k, v)
    width = MLA_HEADS * MLA_V
    return (lambda o_ref: [o_ref[...]]), [(o.reshape(bsz * length, width), width)], [], tail_tm


def _ssd_in_body(h_ref, hp_ref, hn_ref, gpre_ref, wz_ref, wx_ref, wdt_ref, cw_ref, cb_ref,
                 z_ref, xbc_ref, dt_ref, xpad_ref, *, col_block):
    i = pl.program_id(1)
    n_i = pl.num_programs(1)
    gpre = gpre_ref[...]
    u = _rms(h_ref[0], gpre).astype(BF16)
    halo = _rms(jnp.concatenate([hp_ref[0], hn_ref[0]], axis=0), gpre).astype(BF16)
    tm = u.shape[0]
    pad = SSM_CONV // 2
    keep_prev = jnp.where(i > 0, 1.0, 0.0)
    keep_next = jnp.where(i < n_i - 1, 1.0, 0.0)
    n_blocks = xbc_ref.shape[2] // col_block
    z_blocks = z_ref.shape[2] // col_block
    cols = lambda j: slice(j * col_block, (j + 1) * col_block)
    proj = lambda j: (_dot(u, wx_ref[:, cols(j)]), _dot(halo, wx_ref[:, cols(j)]))

    def conv(j, x, x_halo):
        pad_ref = xpad_ref.at[j % 2]
        pad_ref[0:SUBLANES, :] = x_halo[:SUBLANES] * keep_prev
        pad_ref[SUBLANES:SUBLANES + tm, :] = x
        pad_ref[SUBLANES + tm:, :] = x_halo[SUBLANES:] * keep_next
        acc = x * cw_ref[pad:pad + 1, cols(j)] + cb_ref[:, cols(j)]
        for kk in range(SSM_CONV):
            off = kk - pad
            if off != 0:
                acc = acc + pad_ref[SUBLANES + off:SUBLANES + off + tm, :] * cw_ref[kk:kk + 1, cols(j)]
        xbc_ref[0, :, cols(j)] = _silu(acc)

    nxt = proj(0)
    for j in range(n_blocks):
        cur = nxt
        if j + 1 < n_blocks:
            nxt = proj(j + 1)
        if j < z_blocks:
            z_ref[0, :, cols(j)] = _dot(u, wz_ref[:, cols(j)])
        conv(j, *cur)
    for j in range(n_blocks, z_blocks):
        z_ref[0, :, cols(j)] = _dot(u, wz_ref[:, cols(j)])
    dt_ref[0] = _dot(u, wdt_ref[...])


def _ssd_direction(xbc, dt_raw, bias, a_neg, expand, state_ref, reverse, d_inner):
    q = xbc.shape[0]
    gn = SSM_GROUPS * SSM_STATE
    hpg = d_inner // SSM_HEAD_DIM // SSM_GROUPS
    xs = xbc[:, :d_inner]
    b_in = xbc[:, d_inner:d_inner + gn]
    c_in = xbc[:, d_inner + gn:]
    dt = _softplus(dt_raw + bias)
    a_dt = dt * a_neg
    ti = lax.broadcasted_iota(jnp.int32, (q, q), 0)
    si = lax.broadcasted_iota(jnp.int32, (q, q), 1)
    keep = (si >= ti) if reverse else (si <= ti)
    tri = jnp.where(keep, 1.0, 0.0).astype(BF16)
    cum = _dot_split(tri, a_dt, terms=3)
    cum_t = jnp.transpose(cum)
    dt_t = jnp.transpose(dt)
    edge = cum[0:1, :] if reverse else cum[q - 1:q, :]
    x_end = (xs * _split_dot(dt * jnp.exp(edge - cum), expand)).astype(BF16)
    scale_in = _split_dot(jnp.exp(cum), expand)
    decay_state = _split_dot(jnp.exp(edge), expand)
    xs_b = xs.astype(BF16)
    gw = hpg * SSM_HEAD_DIM
    groups = range(SSM_GROUPS)
    gcols = lambda g: slice(g * gw, (g + 1) * gw)
    bg = [b_in[:, g * SSM_STATE:(g + 1) * SSM_STATE].astype(BF16) for g in groups]
    cg = [c_in[:, g * SSM_STATE:(g + 1) * SSM_STATE].astype(BF16) for g in groups]
    st = [state_ref[:, gcols(g)] for g in groups]
    cb = [_dot_nt(cg[g], bg[g]) for g in groups]
    y_off = [_dot(cg[g], st[g].astype(BF16)) * scale_in[:, gcols(g)] for g in groups]
    lane = lax.broadcasted_iota(jnp.int32, (q, 2 * SSM_HEAD_DIM), 1)
    keep_l = jnp.where(lane < SSM_HEAD_DIM, 1.0, 0.0).astype(BF16)
    keep_r = jnp.where(lane < SSM_HEAD_DIM, 0.0, 1.0).astype(BF16)
    outs = []
    for pr in range(d_inner // (2 * SSM_HEAD_DIM)):
        m_pair = []
        for hd in (2 * pr, 2 * pr + 1):
            seg = cum[:, hd:hd + 1] - cum_t[hd:hd + 1, :]
            decay = jnp.exp(jnp.where(keep, seg, -jnp.inf))
            m_pair.append((cb[hd // hpg] * decay * dt_t[hd:hd + 1, :]).astype(BF16))
        x_pair = xs_b[:, pr * 2 * SSM_HEAD_DIM:(pr + 1) * 2 * SSM_HEAD_DIM]
        x_bd = jnp.concatenate([x_pair * keep_l, x_pair * keep_r], axis=0)
        outs.append(_dot(jnp.concatenate(m_pair, axis=1), x_bd))
    y = jnp.concatenate(outs, axis=-1) + jnp.concatenate(y_off, axis=-1)
    new_state = [st[g] * decay_state[:, gcols(g)] + _dot_tn(bg[g], x_end[:, gcols(g)]) for g in groups]
    for g in groups:
        state_ref[:, gcols(g)] = new_state[g]
    return y


def _ssd_scan_body(xf_ref, dtf_ref, xb_ref, dtb_ref, bias_ref, alog_ref, exp_ref,
                   yf_ref, yb_ref, sf_ref, sb_ref, *, d_inner):
    @pl.when(pl.program_id(1) == 0)
    def _():
        sf_ref[...] = jnp.zeros(sf_ref.shape, F32)
        sb_ref[...] = jnp.zeros(sb_ref.shape, F32)

    nh = d_inner // SSM_HEAD_DIM
    a_neg = -jnp.exp(alog_ref[...])
    expand = exp_ref[...]
    q = SSM_CHUNK
    n_sub = xf_ref.shape[1] // q
    for sub in range(n_sub):
        rf = slice(sub * q, (sub + 1) * q)
        rb = slice((n_sub - 1 - sub) * q, (n_sub - sub) * q)
        yf_ref[0, rf, :] = _ssd_direction(xf_ref[0, rf, :], dtf_ref[0, rf, :][:, :nh], bias_ref[0:1, :],
                                          a_neg[0:1, :], expand, sf_ref, False, d_inner)
        yb_ref[0, rb, :] = _ssd_direction(xb_ref[0, rb, :], dtb_ref[0, rb, :][:, nh:], bias_ref[1:2, :],
                                          a_neg[1:2, :], expand, sb_ref, True, d_inner)


def _ssd_out(yf_ref, yb_ref, xs_ref, z_ref, dsk_ref, gn_ref):
    gw = yf_ref.shape[1] // SSM_GROUPS
    blocks = []
    for g in range(SSM_GROUPS):
        cols = slice(g * gw, (g + 1) * gw)
        y = (yf_ref[:, cols] + yb_ref[:, cols] + xs_ref[:, cols] * dsk_ref[:, cols]) * _silu(z_ref[:, cols])
        blocks.append(_rms(y, gn_ref[:, cols]).astype(BF16))
    return blocks


def _ssd_mixer(h3, g_pre, w_in, conv_w, conv_b, a_log, dt_bias, d_skip, gate_norm, *, tm=512,
               col_block=512, chunks_per_step=2, tail_tm=256):
    bsz, length, d = h3.shape
    t = bsz * length
    nh = a_log.shape[-1]
    d_inner = nh * SSM_HEAD_DIM
    conv_dim = d_inner + 2 * SSM_GROUPS * SSM_STATE
    q = SSM_CHUNK
    nc = length // q
    assert length % tm == 0 and length % (q * chunks_per_step) == 0
    hb = tm // SUBLANES
    nhb = length // SUBLANES
    cur = lambda c: pl.BlockSpec((1, tm, c), lambda b, i: (b, i, 0))
    z, xbc, dt3 = pl.pallas_call(
        functools.partial(_ssd_in_body, col_block=col_block),
        grid=(bsz, length // tm),
        in_specs=[cur(d),
                  pl.BlockSpec((1, SUBLANES, d), lambda b, i: (b, jnp.maximum(i * hb - 1, 0), 0)),
                  pl.BlockSpec((1, SUBLANES, d), lambda b, i: (b, jnp.minimum((i + 1) * hb, nhb - 1), 0)),
                  _const_spec((1, d)), _const_spec((d, d_inner)), _const_spec((d, conv_dim)),
                  _const_spec((d, 2 * nh)), _const_spec((SSM_CONV, conv_dim)), _const_spec((1, conv_dim))],
        out_specs=[cur(d_inner), cur(conv_dim), cur(2 * nh)],
        out_shape=[jax.ShapeDtypeStruct((bsz, length, d_inner), F32),
                   jax.ShapeDtypeStruct((bsz, length, conv_dim), F32),
                   jax.ShapeDtypeStruct((bsz, length, 2 * nh), F32)],
        scratch_shapes=[pltpu.VMEM((2, tm + 2 * SUBLANES, col_block), F32)],
        compiler_params=_params(("parallel", "parallel")),
        name="ssd_in",
    )(h3, h3, h3, _row(g_pre), w_in[:, :d_inner].astype(BF16),
      w_in[:, d_inner:d_inner + conv_dim].astype(BF16), w_in[:, d_inner + conv_dim:].astype(BF16),
      conv_w.astype(F32), _row(conv_b))
    z = z.reshape(t, d_inner)
    expand = jnp.repeat(jnp.eye(nh, dtype=BF16), SSM_HEAD_DIM, axis=1)
    steps = nc // chunks_per_step
    rows = chunks_per_step * q
    fwd = lambda c: pl.BlockSpec((1, rows, c), lambda b, i: (b, i, 0))
    bwd = lambda c: pl.BlockSpec((1, rows, c), lambda b, i: (b, steps - 1 - i, 0))
    y_f, y_b = pl.pallas_call(
        functools.partial(_ssd_scan_body, d_inner=d_inner),
        grid=(bsz, steps),
        in_specs=[fwd(conv_dim), fwd(2 * nh), bwd(conv_dim), bwd(2 * nh),
                  _const_spec((2, nh)), _const_spec((2, nh)), _const_spec((nh, d_inner))],
        out_specs=[fwd(d_inner), bwd(d_inner)],
        out_shape=[jax.ShapeDtypeStruct((bsz, length, d_inner), F32)] * 2,
        scratch_shapes=[pltpu.VMEM((SSM_STATE, d_inner), F32)] * 2,
        compiler_params=_params(("parallel", "arbitrary")),
        name="ssd_scan",
    )(xbc, dt3, xbc, dt3, dt_bias.astype(F32), a_log.astype(F32), expand)
    row_inputs = [(y_f.reshape(t, d_inner), d_inner), (y_b.reshape(t, d_inner), d_inner),
                  (xbc.reshape(t, conv_dim), d_inner), (z, d_inner)]
    consts = [_row(jnp.repeat(d_skip.astype(F32), SSM_HEAD_DIM)), _row(gate_norm)]
    return _ssd_out, row_inputs, consts, tail_tm


def _rwkv_in_body(h_ref, hp_ref, hn_ref, gpre_ref, mu_ref, wr_ref, wk_ref, wv_ref, w0_ref, w1_ref,
                  w2f_ref, w2b_ref, a0_ref, a1_ref, a2_ref, g1_ref, g2_ref, kk_ref, ka_ref,
                  rk_ref, seg_ref, segt_ref, tri_ref, up_ref,
                  v_out, bv_out, g_out, pendf_out, pendb_out, *dir_outs):
    i = pl.program_id(1)
    n_i = pl.num_programs(1)
    gpre = gpre_ref[...]
    x = _rms(h_ref[0], gpre)
    tm = x.shape[0]
    prev_row = _rms(hp_ref[0][SUBLANES - 1:SUBLANES, :], gpre) * jnp.where(i > 0, 1.0, 0.0)
    next_row = _rms(hn_ref[0][0:1, :], gpre) * jnp.where(i < n_i - 1, 1.0, 0.0)
    t = lax.broadcasted_iota(jnp.int32, x.shape, 0)
    x_prev = jnp.where(t == 0, prev_row, pltpu.roll(x, 1, axis=0))
    x_next = jnp.where(t == tm - 1, next_row, pltpu.roll(x, tm - 1, axis=0))
    xx = 0.5 * (x_prev + x_next) - x
    mix = lambda j: (x + xx * mu_ref[j:j + 1, :]).astype(BF16)
    xr, xw, xk, xv, xa, xg = (mix(j) for j in range(6))
    r = _dot(xr, wr_ref[...])
    k = _dot(xk, wk_ref[...])
    v = _dot(xv, wv_ref[...])
    lora_w = jnp.tanh(_dot(xw, w1_ref[...])).astype(BF16)
    log_decay = lambda w: -np.float32(np.exp(-0.5)) * jax.nn.sigmoid(w)
    lw_f = log_decay(w0_ref[0:1, :] + _dot(lora_w, w2f_ref[...]))
    lw_b = log_decay(w0_ref[1:2, :] + _dot(lora_w, w2b_ref[...]))
    a = jax.nn.sigmoid(a0_ref[...] + _dot(_dot(xa, a1_ref[...]).astype(BF16), a2_ref[...]))
    g_out[0] = _dot(jax.nn.sigmoid(_dot(xg, g1_ref[...])).astype(BF16), g2_ref[...])
    kk = k * kk_ref[...]
    norm = jnp.maximum(jnp.sqrt(_split_dot(kk * kk, seg_ref[...])), 1e-12)
    kk = kk * _split_dot(1.0 / norm, segt_ref[...])
    k = k * (1.0 + (a - 1.0) * ka_ref[...])
    rm = -kk
    add = kk * a
    v_out[0] = v.astype(v_out.dtype)
    bonus = _split_dot(_split_dot(r * k * rk_ref[...], seg_ref[...]), segt_ref[...])
    bv_out[0] = bonus * v
    c = RWKV_CHUNK
    for lw, cum, last, pend_out, outs in (
            (lw_f, _dot_split(tri_ref[...], lw_f), c - 1, pendf_out, dir_outs[:4]),
            (lw_b, lw_b + _dot_split(up_ref[...], lw_b), 0, pendb_out, dir_outs[4:])):
        e_neg = jnp.exp(-cum)
        values = (r * jnp.exp(cum), rm * jnp.exp(cum - lw), add * e_neg, k * e_neg)
        for o_ref, val in zip(outs, values):
            o_ref[0] = val.astype(o_ref.dtype)
        for m in range(tm // c):
            pend_out[0, m] = jnp.exp(cum[m * c + last:m * c + last + 1, :])


def _rwkv_chunk(r_t, rm_t, add_t, k_t, v_b, p_end, s_all, n_forward):
    c = r_t.shape[0]
    hd = RWKV_HEAD_DIM
    pair = 2 * hd
    assert c == hd and pair == LANES
    lane = lax.broadcasted_iota(jnp.int32, (c, pair), 1)
    ti = lax.broadcasted_iota(jnp.int32, (c, pair), 0)
    si = lane & (hd - 1)
    left = lane < hd
    keep_l = jnp.where(left, 1.0, 0.0).astype(BF16)
    keep_r = jnp.where(left, 0.0, 1.0).astype(BF16)
    bd = lambda x: jnp.concatenate([x * keep_l, x * keep_r], axis=0)
    eye = jnp.where(si == ti, 1.0, 0.0).astype(F32)
    pairs = range(r_t.shape[1] // pair)
    incl_of = lambda j: (si <= ti) if j < n_forward else (si >= ti)
    strict_of = lambda j: (si < ti) if j < n_forward else (si > ti)
    col = lambda x, j: x[:, j * pair:(j + 1) * pair]
    rows = lambda *xs: jnp.concatenate(xs, axis=0)
    v_bd = [bd(col(v_b, j)) for j in pairs]
    s0 = [col(s_all, j) for j in pairs]
    s0_bd = [bd(s.astype(BF16)) for s in s0]
    sc = [_dot_nt(rows(col(rm_t, j), col(r_t, j)), rows(bd(col(add_t, j)), bd(col(k_t, j)))) for j in pairs]
    n_mat = [jnp.where(strict_of(j), sc[j][:c, :pair], 0.0) for j in pairs]
    a_ak = [jnp.where(strict_of(j), sc[j][:c, pair:], 0.0).astype(BF16) for j in pairs]
    a_rb = [jnp.where(incl_of(j), sc[j][c:, :pair], 0.0).astype(BF16) for j in pairs]
    a_rk = [jnp.where(incl_of(j), sc[j][c:, pair:], 0.0).astype(BF16) for j in pairs]
    xkv = [_dot(rows(a_ak[j], a_rk[j]), v_bd[j]) for j in pairs]
    rs = [_dot_nt(col(r_t, j), s0_bd[j]) for j in pairs]
    inv = [eye + n_mat[j] for j in pairs]
    pw = [n_mat[j].astype(BF16) for j in pairs]
    pw = [_dot(pw[j], bd(pw[j])).astype(BF16) for j in pairs]
    span = 2
    while span < c:
        if 2 * span < c:
            both = [_dot(rows(pw[j], inv[j].astype(BF16)), bd(pw[j])) for j in pairs]
            pw = [both[j][:c].astype(BF16) for j in pairs]
            inv = [inv[j] + both[j][c:] for j in pairs]
        else:
            inv = [inv[j] + _dot(inv[j].astype(BF16), bd(pw[j])) for j in pairs]
        span *= 2
    wu = [_dot(inv[j].astype(BF16),
               jnp.concatenate([bd(col(rm_t, j)), bd(xkv[j][:c].astype(BF16))], axis=1)) for j in pairs]
    sa_b = [(_dot_nt(wu[j][:, :pair].astype(BF16), s0_bd[j]) + wu[j][:, pair:]).astype(BF16)
            for j in pairs]
    outs = [rs[j] + _dot(a_rb[j], bd(sa_b[j])) + xkv[j][c:] for j in pairs]
    full = [_dot_tn(rows(sa_b[j], col(v_b, j)), rows(col(add_t, j), col(k_t, j))) for j in pairs]
    states = [(s0[j] + jnp.where(left, full[j][:hd], full[j][hd:])) * col(p_end, j) for j in pairs]
    return jnp.concatenate(outs, axis=-1), jnp.concatenate(states, axis=-1)


def _rwkv_scan_body(*refs):
    f_in, b_in = refs[:6], refs[6:12]
    yf_ref, yb_ref, sf_ref, sb_ref = refs[12:]

    @pl.when(pl.program_id(2) == 0)
    def _():
        sf_ref[...] = jnp.zeros(sf_ref.shape, F32)
        sb_ref[...] = jnp.zeros(sb_ref.shape, F32)

    width = sf_ref.shape[1]
    c = RWKV_CHUNK
    n_sub = f_in[0].shape[1] // c
    both = lambda f, b: jnp.concatenate([f, b], axis=-1)
    state = both(sf_ref[...], sb_ref[...])
    for sub in range(n_sub):
        rf = slice(sub * c, (sub + 1) * c)
        rb = slice((n_sub - 1 - sub) * c, (n_sub - sub) * c)
        ops = [both(f[0, rf, :], b[0, rb, :]) for f, b in zip(f_in[:5], b_in[:5])]
        p_end = both(f_in[5][0, sub], b_in[5][0, n_sub - 1 - sub])
        y, state = _rwkv_chunk(*ops, p_end, state, width // LANES)
        yf_ref[0, rf, :] = y[:, :width]
        yb_ref[0, rb, :] = y[:, width:]
    sf_ref[...] = state[:, :width]
    sb_ref[...] = state[:, width:]


def _rwkv_out(yf_ref, yb_ref, bv_ref, g_ref, lnw_ref, lnb_ref, seg_ref, segt_ref):
    width = 2 * LANES
    blocks = []
    for j in range(yf_ref.shape[1] // width):
        cols = slice(j * width, (j + 1) * width)
        seg, segt = seg_ref[cols, :], segt_ref[:, cols]
        head_mean = lambda x: _split_dot(_split_dot(x, seg) * (1.0 / RWKV_HEAD_DIM), segt)
        y = yf_ref[:, cols] + yb_ref[:, cols]
        cen = y - head_mean(y)
        var = head_mean(cen * cen)
        y = cen * lax.rsqrt(var + RWKV_LN_EPS) * lnw_ref[:, cols] + lnb_ref[:, cols]
        blocks.append(((y + bv_ref[:, cols]) * g_ref[:, cols]).astype(BF16))
    return blocks


def _pad_to(w, axis, size):
    pad = [(0, 0)] * w.ndim
    pad[axis] = (0, size - w.shape[axis])
    return jnp.pad(w, pad)


def _rwkv_mixer(h3, g_pre, mu, w_rkv, w0, w1, w2, a0, a1, a2, g1, g2, k_k, k_a, r_k, ln_w, ln_b, *,
                tm=256, heads_per_step=16, chunks_per_step=4, tail_tm=512):
    bsz, length, d = h3.shape
    t = bsz * length
    n_heads = d // RWKV_HEAD_DIM
    c = RWKV_CHUNK
    nc = length // c
    hw = heads_per_step * RWKV_HEAD_DIM
    assert length % tm == 0 and length % (c * chunks_per_step) == 0 and n_heads % heads_per_step == 0
    assert n_heads <= LANES
    rank_w = w1.shape[-1]
    w1_cat = jnp.concatenate([w1[0], w1[1]], axis=-1).astype(BF16)
    w2f = jnp.concatenate([w2[0], jnp.zeros_like(w2[1])], axis=0).astype(BF16)
    w2b = jnp.concatenate([jnp.zeros_like(w2[0]), w2[1]], axis=0).astype(BF16)
    a1p = _pad_to(a1, 1, LANES).astype(BF16)
    a2p = _pad_to(a2, 0, LANES).astype(BF16)
    g_rank = -(-g1.shape[1] // LANES) * LANES
    g1p = _pad_to(g1, 1, g_rank).astype(BF16)
    g2p = _pad_to(g2, 0, g_rank).astype(BF16)
    seg = _pad_to(jnp.repeat(jnp.eye(n_heads, dtype=BF16), RWKV_HEAD_DIM, axis=0), 1, LANES)
    segt = seg.T
    hb = tm // SUBLANES
    nhb = length // SUBLANES
    pos = np.arange(tm)
    same_chunk = (pos[:, None] // c) == (pos[None, :] // c)
    tri = jnp.asarray(same_chunk & (pos[None, :] <= pos[:, None]), BF16)
    up = jnp.asarray(same_chunk & (pos[None, :] > pos[:, None]), BF16)
    cur = pl.BlockSpec((1, tm, d), lambda b, i: (b, i, 0))
    pend_spec = pl.BlockSpec((1, tm // c, 1, d), lambda b, i: (b, i, 0, 0))
    pend_shape = jax.ShapeDtypeStruct((bsz, nc, 1, d), F32)
    act = jax.ShapeDtypeStruct((bsz, length, d), BF16)
    outs = pl.pallas_call(
        _rwkv_in_body,
        grid=(bsz, length // tm),
        in_specs=[cur,
                  pl.BlockSpec((1, SUBLANES, d), lambda b, i: (b, jnp.maximum(i * hb - 1, 0), 0)),
                  pl.BlockSpec((1, SUBLANES, d), lambda b, i: (b, jnp.minimum((i + 1) * hb, nhb - 1), 0)),
                  _const_spec((1, d)), _const_spec((6, d)),
                  _const_spec((d, d)), _const_spec((d, d)), _const_spec((d, d)),
                  _const_spec((2, d)), _const_spec((d, 2 * rank_w)),
                  _const_spec((2 * rank_w, d)), _const_spec((2 * rank_w, d)),
                  _const_spec((1, d)), _const_spec((d, LANES)), _const_spec((LANES, d)),
                  _const_spec((d, g_rank)), _const_spec((g_rank, d)),
                  _const_spec((1, d)), _const_spec((1, d)), _const_spec((1, d)),
                  _const_spec((d, LANES)), _const_spec((LANES, d)),
                  _const_spec((tm, tm)), _const_spec((tm, tm))],
        out_specs=[cur, cur, cur, pend_spec, pend_spec] + [cur] * 8,
        out_shape=[act, jax.ShapeDtypeStruct((bsz, length, d), F32),
                   jax.ShapeDtypeStruct((bsz, length, d), F32), pend_shape, pend_shape] + [act] * 8,
        compiler_params=_params(("parallel", "parallel")),
        name="rwkv_in",
    )(h3, h3, h3, _row(g_pre), mu.astype(F32), w_rkv[0].astype(BF16), w_rkv[1].astype(BF16),
      w_rkv[2].astype(BF16), w0.astype(F32), w1_cat, w2f, w2b, _row(a0), a1p, a2p, g1p, g2p,
      _row(k_k), _row(k_a), _row(r_k), seg, segt, tri, up)
    v, bv, g, pend_f, pend_b = outs[:5]
    ops_f, ops_b = outs[5:9], outs[9:]
    steps = nc // chunks_per_step
    rows = chunks_per_step * c
    fwd = pl.BlockSpec((1, rows, hw), lambda b, hg, i: (b, i, hg))
    bwd = pl.BlockSpec((1, rows, hw), lambda b, hg, i: (b, steps - 1 - i, hg))
    fwd_p = pl.BlockSpec((1, chunks_per_step, 1, hw), lambda b, hg, i: (b, i, 0, hg))
    bwd_p = pl.BlockSpec((1, chunks_per_step, 1, hw), lambda b, hg, i: (b, steps - 1 - i, 0, hg))
    y_f, y_b = pl.pallas_call(
        _rwkv_scan_body,
        grid=(bsz, n_heads // heads_per_step, steps),
        in_specs=[fwd] * 5 + [fwd_p] + [bwd] * 5 + [bwd_p],
        out_specs=[fwd, bwd],
        out_shape=[jax.ShapeDtypeStruct((bsz, length, d), F32)] * 2,
        scratch_shapes=[pltpu.VMEM((RWKV_HEAD_DIM, hw), F32)] * 2,
        compiler_params=_params(("parallel", "parallel", "arbitrary")),
        name="rwkv_scan",
    )(*ops_f, v, pend_f, *ops_b, v, pend_b)
    row_inputs = [(x.reshape(t, d), d) for x in (y_f, y_b, bv, g)]
    return _rwkv_out, row_inputs, [_row(ln_w), _row(ln_b), seg, segt], tail_tm


def kernel(x_prompt, x_sample, mix_pre_norm, mix_post_norm, ffn_pre_norm, ffn_post_norm, mla_w_in, mla_q_norm, mla_w_qb, mla_kv_norm, mla_w_kvb, mla_w_o, ssm_w_in, ssm_conv_w, ssm_conv_b, ssm_a_log, ssm_dt_bias, ssm_d_skip, ssm_gate_norm, ssm_w_out, rwkv_mu, rwkv_w_rkv, rwkv_w0, rwkv_w1, rwkv_w2, rwkv_a0, rwkv_a1, rwkv_a2, rwkv_g1, rwkv_g2, rwkv_k_k, rwkv_k_a, rwkv_r_k, rwkv_ln_w, rwkv_ln_b, rwkv_w_o, ffn_w_in, ffn_w_out):
    depth = mix_pre_norm.shape[0]

    def trunk(x):
        bsz, length, d = x.shape
        h = x.reshape(bsz * length, d)
        for i in range(depth):
            kind, j = i % N_MIXERS, i // N_MIXERS
            h3 = h.reshape(bsz, length, d)
            if kind == 0:
                mixer = _mla_mixer(h3, mix_pre_norm[i], mla_w_in[j], mla_q_norm[j], mla_w_qb[j],
                                   mla_kv_norm[j], mla_w_kvb[j])
                w_o = mla_w_o[j]
            elif kind == 1:
                mixer = _ssd_mixer(h3, mix_pre_norm[i], ssm_w_in[j], ssm_conv_w[j], ssm_conv_b[j],
                                   ssm_a_log[j], ssm_dt_bias[j], ssm_d_skip[j], ssm_gate_norm[j])
                w_o = ssm_w_out[j]
            else:
                mixer = _rwkv_mixer(h3, mix_pre_norm[i], rwkv_mu[j], rwkv_w_rkv[j], rwkv_w0[j],
                                    rwkv_w1[j], rwkv_w2[j], rwkv_a0[j], rwkv_a1[j], rwkv_a2[j],
                                    rwkv_g1[j], rwkv_g2[j], rwkv_k_k[j], rwkv_k_a[j], rwkv_r_k[j],
                                    rwkv_ln_w[j], rwkv_ln_b[j])
                w_o = rwkv_w_o[j]
            h = _layer_tail(h, mixer, w_o, mix_post_norm[i], ffn_pre_norm[i], ffn_w_in[i],
                            ffn_w_out[i], ffn_post_norm[i])
        return h.reshape(bsz, length, d)

    return (trunk(x_prompt), trunk(x_sample))
```

```python
import functools

import jax
import jax.numpy as jnp
import numpy as np
from jax import lax
from jax.experimental import pallas as pl
from jax.experimental.pallas import tpu as pltpu

F32 = jnp.float32
BF16 = jnp.bfloat16

NORM_EPS = 1e-6
N_MIXERS = 3

VMEM_LIMIT_BYTES = 56 * 1024 * 1024
LANES = 128
SUBLANES = 8

MLA_HEADS = 8
MLA_NOPE = 128
MLA_ROPE = 64
MLA_V = 128
MLA_QK = MLA_NOPE + 2 * MLA_ROPE
ROPE_THETA = 10000.0

SSM_HEAD_DIM = 64
SSM_GROUPS = 4
SSM_STATE = 128
SSM_CONV = 5
SSM_CHUNK = 128

RWKV_HEAD_DIM = 64
RWKV_LN_EPS = 64e-5
RWKV_CHUNK = 64


def _rms(x, g, eps=NORM_EPS):
    return x * lax.rsqrt(jnp.mean(x * x, axis=-1, keepdims=True) + eps) * g


def _dot(a, b):
    return jnp.dot(a, b, preferred_element_type=F32)


def _dot_nt(a, b):
    return lax.dot_general(a, b, (((1,), (1,)), ((), ())), preferred_element_type=F32)


def _dot_tn(a, b):
    return lax.dot_general(a, b, (((0,), (0,)), ((), ())), preferred_element_type=F32)


def _bf16_terms(x, terms):
    pieces = []
    for _ in range(terms):
        p = x.astype(BF16)
        pieces.append(p)
        x = x - p.astype(F32)
    return pieces


def _dot_split(m, x, terms=2):
    return sum(_dot(m, p) for p in _bf16_terms(x, terms))


def _split_dot(x, m, terms=2):
    return sum(_dot(p, m) for p in _bf16_terms(x, terms))


def _silu(x):
    return x * jax.nn.sigmoid(x)


def _softplus(x):
    return jnp.maximum(x, 0.0) + jnp.log(1.0 + jnp.exp(-jnp.abs(x)))


def _const_spec(shape):
    nd = len(shape)
    return pl.BlockSpec(shape, lambda *_: (0,) * nd, pipeline_mode=pl.Buffered(1))


def _params(semantics):
    return pltpu.CompilerParams(dimension_semantics=semantics, vmem_limit_bytes=VMEM_LIMIT_BYTES)


def _row(v):
    return v.reshape(1, -1).astype(F32)


def _tail_body(*refs, n_mixer_refs, mixer_out, ff_chunk):
    h_ref = refs[0]
    wo_ref, gpost_ref, gpre_ref, win_ref, wout_ref, gfpost_ref, o_ref = refs[1 + n_mixer_refs:]
    d_ff = wout_ref.shape[0]
    m = jnp.zeros(h_ref.shape, F32)
    k0 = 0
    for blk in mixer_out(*refs[1:1 + n_mixer_refs]):
        m = m + _dot(blk, wo_ref[k0:k0 + blk.shape[1], :])
        k0 += blk.shape[1]
    h1 = h_ref[...] + _rms(m, gpost_ref[...])
    u = _rms(h1, gpre_ref[...]).astype(BF16)
    n_chunks = d_ff // ff_chunk
    gate_up = lambda c: (_dot(u, win_ref[:, c * ff_chunk:(c + 1) * ff_chunk]),
                         _dot(u, win_ref[:, d_ff + c * ff_chunk:d_ff + (c + 1) * ff_chunk]))
    acc = jnp.zeros(h1.shape, F32)
    nxt = gate_up(0)
    for c in range(n_chunks):
        gate, up = nxt
        if c + 1 < n_chunks:
            nxt = gate_up(c + 1)
        act = (_silu(gate) * up).astype(BF16)
        acc = acc + _dot(act, wout_ref[c * ff_chunk:(c + 1) * ff_chunk, :])
    o_ref[...] = h1 + _rms(acc, gfpost_ref[...])


def _layer_tail(h, mixer, w_o, g_post, g_pre, w_in, w_out, g_fpost, *, ff_chunk=256):
    mixer_out, row_inputs, const_inputs, tm = mixer
    t, d = h.shape
    ky = w_o.shape[0]
    d_ff = w_out.shape[0]
    assert t % tm == 0 and d_ff % ff_chunk == 0 and ff_chunk % LANES == 0
    row = lambda c: pl.BlockSpec((tm, c), lambda i: (i, 0))
    return pl.pallas_call(
        functools.partial(_tail_body, n_mixer_refs=len(row_inputs) + len(const_inputs),
                          mixer_out=mixer_out, ff_chunk=ff_chunk),
        grid=(t // tm,),
        in_specs=[row(d)] + [row(width) for _, width in row_inputs]
        + [_const_spec(c.shape) for c in const_inputs]
        + [_const_spec((ky, d)), _const_spec((1, d)), _const_spec((1, d)),
           _const_spec((d, 2 * d_ff)), _const_spec((d_ff, d)), _const_spec((1, d))],
        out_specs=row(d),
        out_shape=jax.ShapeDtypeStruct((t, d), F32),
        compiler_params=_params(("parallel",)),
        name="layer_tail",
    )(h, *[a for a, _ in row_inputs], *const_inputs, w_o.astype(BF16), _row(g_post), _row(g_pre),
      w_in.astype(BF16), w_out.astype(BF16), _row(g_fpost))


def _mla_in_body(h_ref, fq_ref, fk_ref, gpre_ref, win_ref, qn_ref, wq_ref, kvn_ref, wkv_ref,
                 q_ref, k_ref, v_ref, *, q_rank, kv_rank):
    u = _rms(h_ref[0], gpre_ref[...]).astype(BF16)
    lat = _dot(u, win_ref[...])
    q_lat = _rms(lat[:, :q_rank], qn_ref[...]).astype(BF16)
    kv_lat = _rms(lat[:, q_rank:q_rank + kv_rank], kvn_ref[...]).astype(BF16)
    kr = lat[:, q_rank + kv_rank:q_rank + kv_rank + LANES]
    kr_rot = lat[:, q_rank + kv_rank + LANES:q_rank + kv_rank + 2 * LANES]
    fk = fk_ref[...]
    k_rope = (kr * fk[:, :LANES] + kr_rot * fk[:, LANES:]).astype(BF16)
    fq = fq_ref[...]
    q = _dot(q_lat, wq_ref[...])
    kv = _dot(kv_lat, wkv_ref[...])
    for hd in range(MLA_HEADS):
        q_ref[0, :, hd * MLA_QK:(hd + 1) * MLA_QK] = (
            q[:, hd * MLA_QK:(hd + 1) * MLA_QK] * fq).astype(BF16)
        k_ref[0, :, hd * MLA_QK:hd * MLA_QK + MLA_NOPE] = (
            kv[:, hd * MLA_NOPE:(hd + 1) * MLA_NOPE].astype(BF16))
        k_ref[0, :, hd * MLA_QK + MLA_NOPE:(hd + 1) * MLA_QK] = k_rope
    v_ref[0] = jnp.transpose(kv[:, MLA_HEADS * MLA_NOPE:]).astype(BF16)


def _mla_attn_body(q_ref, k_ref, vt_ref, o_ref, *, q_sub, kv_chunk):
    n_chunks = k_ref.shape[1] // kv_chunk
    n_sub = q_ref.shape[1] // q_sub
    kv_rows = lambda c: slice(c * kv_chunk, (c + 1) * kv_chunk)
    score = lambda t: _dot_nt(k_ref[0], q_ref[0, t * q_sub:(t + 1) * q_sub, :])
    s_cur = score(0)
    for t in range(n_sub):
        s_next = score(t + 1) if t + 1 < n_sub else None
        m = jnp.max(s_cur, axis=0, keepdims=True)
        l = jnp.zeros((1, q_sub), F32)
        acc = jnp.zeros((vt_ref.shape[1], q_sub), F32)
        for c in range(n_chunks):
            p = jnp.exp2(s_cur[kv_rows(c), :] - m)
            l = l + jnp.sum(p, axis=0, keepdims=True)
            acc = acc + _dot(vt_ref[0, :, kv_rows(c)], p.astype(BF16))
        o_ref[0, t * q_sub:(t + 1) * q_sub, :] = jnp.transpose(acc / l).astype(o_ref.dtype)
        s_cur = s_next


def _rot_cols(w):
    half = w.shape[-1] // 2
    return jnp.concatenate([-w[..., half:], w[..., :half]], axis=-1)


def _mla_mixer(h3, g_pre, w_in, q_norm, w_qb, kv_norm, w_kvb, *, tm=512, tq=4096, q_sub=256,
               kv_chunk=512, tail_tm=512):
    bsz, length, d = h3.shape
    q_rank, kv_rank = q_norm.shape[0], kv_norm.shape[0]
    hd_q = MLA_NOPE + MLA_ROPE
    inv = 1.0 / (ROPE_THETA ** (jnp.arange(0, MLA_ROPE, 2, dtype=F32) / MLA_ROPE))
    ang = jnp.arange(length, dtype=F32)[:, None] * inv[None, :]
    cc = jnp.concatenate([jnp.cos(ang), jnp.cos(ang)], axis=-1)
    ss = jnp.concatenate([jnp.sin(ang), jnp.sin(ang)], axis=-1)
    scale = hd_q ** -0.5 * np.log2(np.e)
    fq = jnp.concatenate([jnp.full((length, MLA_NOPE), scale, F32), cc * scale, ss * scale], axis=-1)
    fk = jnp.concatenate([cc, cc, ss, ss], axis=-1)
    w_kr = w_in[:, q_rank + kv_rank:]
    w_in_ext = jnp.concatenate([w_in[:, :q_rank + kv_rank], w_kr, w_kr, _rot_cols(w_kr), _rot_cols(w_kr)],
                               axis=-1).astype(BF16)
    wq = w_qb.reshape(q_rank, MLA_HEADS, hd_q)
    wq_ext = jnp.concatenate([wq, _rot_cols(wq[..., MLA_NOPE:])], axis=-1).reshape(
        q_rank, MLA_HEADS * MLA_QK).astype(BF16)
    wkv = w_kvb.reshape(kv_rank, MLA_HEADS, MLA_NOPE + MLA_V)
    wkv_perm = jnp.concatenate([wkv[..., :MLA_NOPE].reshape(kv_rank, -1),
                                wkv[..., MLA_NOPE:].reshape(kv_rank, -1)], axis=-1).astype(BF16)
    n_in = w_in_ext.shape[1]
    tq = min(tq, length)
    assert length % tm == 0 and length % tq == 0
    q, k, v = pl.pallas_call(
        functools.partial(_mla_in_body, q_rank=q_rank, kv_rank=kv_rank),
        grid=(bsz, length // tm),
        in_specs=[pl.BlockSpec((1, tm, d), lambda b, i: (b, i, 0)),
                  pl.BlockSpec((tm, MLA_QK), lambda b, i: (i, 0)),
                  pl.BlockSpec((tm, MLA_QK), lambda b, i: (i, 0)),
                  _const_spec((1, d)), _const_spec((d, n_in)), _const_spec((1, q_rank)),
                  _const_spec((q_rank, MLA_HEADS * MLA_QK)), _const_spec((1, kv_rank)),
                  _const_spec((kv_rank, MLA_HEADS * (MLA_NOPE + MLA_V)))],
        out_specs=[pl.BlockSpec((1, tm, MLA_HEADS * MLA_QK), lambda b, i: (b, i, 0)),
                   pl.BlockSpec((1, tm, MLA_HEADS * MLA_QK), lambda b, i: (b, i, 0)),
                   pl.BlockSpec((1, MLA_HEADS * MLA_V, tm), lambda b, i: (b, 0, i))],
        out_shape=[jax.ShapeDtypeStruct((bsz, length, MLA_HEADS * MLA_QK), BF16),
                   jax.ShapeDtypeStruct((bsz, length, MLA_HEADS * MLA_QK), BF16),
                   jax.ShapeDtypeStruct((bsz, MLA_HEADS * MLA_V, length), BF16)],
        compiler_params=_params(("parallel", "parallel")),
        name="mla_in",
    )(h3, fq, fk, _row(g_pre), w_in_ext, _row(q_norm), wq_ext, _row(kv_norm), wkv_perm)
    o = pl.pallas_call(
        functools.partial(_mla_attn_body, q_sub=q_sub, kv_chunk=kv_chunk),
        grid=(bsz, MLA_HEADS, length // tq),
        in_specs=[pl.BlockSpec((1, tq, MLA_QK), lambda b, hd, i: (b, i, hd)),
                  pl.BlockSpec((1, length, MLA_QK), lambda b, hd, i: (b, 0, hd)),
                  pl.BlockSpec((1, MLA_V, length), lambda b, hd, i: (b, hd, 0))],
        out_specs=pl.BlockSpec((1, tq, MLA_V), lambda b, hd, i: (b, i, hd)),
        out_shape=jax.ShapeDtypeStruct((bsz, length, MLA_HEADS * MLA_V), BF16),
        compiler_params=_params(("parallel", "parallel", "parallel")),
        name="mla_attn",
    )(q, k, v)
    width = MLA_HEADS * MLA_V
    return (lambda o_ref: [o_ref[...]]), [(o.reshape(bsz * length, width), width)], [], tail_tm


def _ssd_in_body(h_ref, hp_ref, hn_ref, gpre_ref, wz_ref, wx_ref, wdt_ref, cw_ref, cb_ref,
                 z_ref, xbc_ref, dt_ref, xpad_ref, *, col_block):
    i = pl.program_id(1)
    n_i = pl.num_programs(1)
    gpre = gpre_ref[...]
    u = _rms(h_ref[0], gpre).astype(BF16)
    halo = _rms(jnp.concatenate([hp_ref[0], hn_ref[0]], axis=0), gpre).astype(BF16)
    tm = u.shape[0]
    pad = SSM_CONV // 2
    keep_prev = jnp.where(i > 0, 1.0, 0.0)
    keep_next = jnp.where(i < n_i - 1, 1.0, 0.0)
    n_blocks = xbc_ref.shape[2] // col_block
    z_blocks = z_ref.shape[2] // col_block
    cols = lambda j: slice(j * col_block, (j + 1) * col_block)
    proj = lambda j: (_dot(u, wx_ref[:, cols(j)]), _dot(halo, wx_ref[:, cols(j)]))

    def conv(j, x, x_halo):
        pad_ref = xpad_ref.at[j % 2]
        pad_ref[0:SUBLANES, :] = x_halo[:SUBLANES] * keep_prev
        pad_ref[SUBLANES:SUBLANES + tm, :] = x
        pad_ref[SUBLANES + tm:, :] = x_halo[SUBLANES:] * keep_next
        acc = x * cw_ref[pad:pad + 1, cols(j)] + cb_ref[:, cols(j)]
        for kk in range(SSM_CONV):
            off = kk - pad
            if off != 0:
                acc = acc + pad_ref[SUBLANES + off:SUBLANES + off + tm, :] * cw_ref[kk:kk + 1, cols(j)]
        xbc_ref[0, :, cols(j)] = _silu(acc)

    nxt = proj(0)
    for j in range(n_blocks):
        cur = nxt
        if j + 1 < n_blocks:
            nxt = proj(j + 1)
        if j < z_blocks:
            z_ref[0, :, cols(j)] = _dot(u, wz_ref[:, cols(j)])
        conv(j, *cur)
    for j in range(n_blocks, z_blocks):
        z_ref[0, :, cols(j)] = _dot(u, wz_ref[:, cols(j)])
    dt_ref[0] = _dot(u, wdt_ref[...])


def _ssd_direction(xbc, dt_raw, bias, a_neg, expand, state_ref, reverse, d_inner):
    q = xbc.shape[0]
    gn = SSM_GROUPS * SSM_STATE
    hpg = d_inner // SSM_HEAD_DIM // SSM_GROUPS
    xs = xbc[:, :d_inner]
    b_in = xbc[:, d_inner:d_inner + gn]
    c_in = xbc[:, d_inner + gn:]
    dt = _softplus(dt_raw + bias)
    a_dt = dt * a_neg
    ti = lax.broadcasted_iota(jnp.int32, (q, q), 0)
    si = lax.broadcasted_iota(jnp.int32, (q, q), 1)
    keep = (si >= ti) if reverse else (si <= ti)
    tri = jnp.where(keep, 1.0, 0.0).astype(BF16)
    cum = _dot_split(tri, a_dt, terms=3)
    cum_t = jnp.transpose(cum)
    dt_t = jnp.transpose(dt)
    edge = cum[0:1, :] if reverse else cum[q - 1:q, :]
    x_end = (xs * _split_dot(dt * jnp.exp(edge - cum), expand)).astype(BF16)
    scale_in = _split_dot(jnp.exp(cum), expand)
    decay_state = _split_dot(jnp.exp(edge), expand)
    xs_b = xs.astype(BF16)
    gw = hpg * SSM_HEAD_DIM
    groups = range(SSM_GROUPS)
    gcols = lambda g: slice(g * gw, (g + 1) * gw)
    bg = [b_in[:, g * SSM_STATE:(g + 1) * SSM_STATE].astype(BF16) for g in groups]
    cg = [c_in[:, g * SSM_STATE:(g + 1) * SSM_STATE].astype(BF16) for g in groups]
    st = [state_ref[:, gcols(g)] for g in groups]
    cb = [_dot_nt(cg[g], bg[g]) for g in groups]
    y_off = [_dot(cg[g], st[g].astype(BF16)) * scale_in[:, gcols(g)] for g in groups]
    lane = lax.broadcasted_iota(jnp.int32, (q, 2 * SSM_HEAD_DIM), 1)
    keep_l = jnp.where(lane < SSM_HEAD_DIM, 1.0, 0.0).astype(BF16)
    keep_r = jnp.where(lane < SSM_HEAD_DIM, 0.0, 1.0).astype(BF16)
    outs = []
    for pr in range(d_inner // (2 * SSM_HEAD_DIM)):
        m_pair = []
        for hd in (2 * pr, 2 * pr + 1):
            seg = cum[:, hd:hd + 1] - cum_t[hd:hd + 1, :]
            decay = jnp.exp(jnp.where(keep, seg, -jnp.inf))
            m_pair.append((cb[hd // hpg] * decay * dt_t[hd:hd + 1, :]).astype(BF16))
        x_pair = xs_b[:, pr * 2 * SSM_HEAD_DIM:(pr + 1) * 2 * SSM_HEAD_DIM]
        x_bd = jnp.concatenate([x_pair * keep_l, x_pair * keep_r], axis=0)
        outs.append(_dot(jnp.concatenate(m_pair, axis=1), x_bd))
    y = jnp.concatenate(outs, axis=-1) + jnp.concatenate(y_off, axis=-1)
    new_state = [st[g] * decay_state[:, gcols(g)] + _dot_tn(bg[g], x_end[:, gcols(g)]) for g in groups]
    for g in groups:
        state_ref[:, gcols(g)] = new_state[g]
    return y


def _ssd_scan_body(xf_ref, dtf_ref, xb_ref, dtb_ref, bias_ref, alog_ref, exp_ref,
                   yf_ref, yb_ref, sf_ref, sb_ref, *, d_inner):
    @pl.when(pl.program_id(1) == 0)
    def _():
        sf_ref[...] = jnp.zeros(sf_ref.shape, F32)
        sb_ref[...] = jnp.zeros(sb_ref.shape, F32)

    nh = d_inner // SSM_HEAD_DIM
    a_neg = -jnp.exp(alog_ref[...])
    expand = exp_ref[...]
    q = SSM_CHUNK
    n_sub = xf_ref.shape[1] // q
    for sub in range(n_sub):
        rf = slice(sub * q, (sub + 1) * q)
        rb = slice((n_sub - 1 - sub) * q, (n_sub - sub) * q)
        yf_ref[0, rf, :] = _ssd_direction(xf_ref[0, rf, :], dtf_ref[0, rf, :][:, :nh], bias_ref[0:1, :],
                                          a_neg[0:1, :], expand, sf_ref, False, d_inner)
        yb_ref[0, rb, :] = _ssd_direction(xb_ref[0, rb, :], dtb_ref[0, rb, :][:, nh:], bias_ref[1:2, :],
                                          a_neg[1:2, :], expand, sb_ref, True, d_inner)


def _ssd_out(yf_ref, yb_ref, xs_ref, z_ref, dsk_ref, gn_ref):
    gw = yf_ref.shape[1] // SSM_GROUPS
    blocks = []
    for g in range(SSM_GROUPS):
        cols = slice(g * gw, (g + 1) * gw)
        y = (yf_ref[:, cols] + yb_ref[:, cols] + xs_ref[:, cols] * dsk_ref[:, cols]) * _silu(z_ref[:, cols])
        blocks.append(_rms(y, gn_ref[:, cols]).astype(BF16))
    return blocks


def _ssd_mixer(h3, g_pre, w_in, conv_w, conv_b, a_log, dt_bias, d_skip, gate_norm, *, tm=512,
               col_block=512, chunks_per_step=2, tail_tm=256):
    bsz, length, d = h3.shape
    t = bsz * length
    nh = a_log.shape[-1]
    d_inner = nh * SSM_HEAD_DIM
    conv_dim = d_inner + 2 * SSM_GROUPS * SSM_STATE
    q = SSM_CHUNK
    nc = length // q
    assert length % tm == 0 and length % (q * chunks_per_step) == 0
    hb = tm // SUBLANES
    nhb = length // SUBLANES
    cur = lambda c: pl.BlockSpec((1, tm, c), lambda b, i: (b, i, 0))
    z, xbc, dt3 = pl.pallas_call(
        functools.partial(_ssd_in_body, col_block=col_block),
        grid=(bsz, length // tm),
        in_specs=[cur(d),
                  pl.BlockSpec((1, SUBLANES, d), lambda b, i: (b, jnp.maximum(i * hb - 1, 0), 0)),
                  pl.BlockSpec((1, SUBLANES, d), lambda b, i: (b, jnp.minimum((i + 1) * hb, nhb - 1), 0)),
                  _const_spec((1, d)), _const_spec((d, d_inner)), _const_spec((d, conv_dim)),
                  _const_spec((d, 2 * nh)), _const_spec((SSM_CONV, conv_dim)), _const_spec((1, conv_dim))],
        out_specs=[cur(d_inner), cur(conv_dim), cur(2 * nh)],
        out_shape=[jax.ShapeDtypeStruct((bsz, length, d_inner), F32),
                   jax.ShapeDtypeStruct((bsz, length, conv_dim), F32),
                   jax.ShapeDtypeStruct((bsz, length, 2 * nh), F32)],
        scratch_shapes=[pltpu.VMEM((2, tm + 2 * SUBLANES, col_block), F32)],
        compiler_params=_params(("parallel", "parallel")),
        name="ssd_in",
    )(h3, h3, h3, _row(g_pre), w_in[:, :d_inner].astype(BF16),
      w_in[:, d_inner:d_inner + conv_dim].astype(BF16), w_in[:, d_inner + conv_dim:].astype(BF16),
      conv_w.astype(F32), _row(conv_b))
    z = z.reshape(t, d_inner)
    expand = jnp.repeat(jnp.eye(nh, dtype=BF16), SSM_HEAD_DIM, axis=1)
    steps = nc // chunks_per_step
    rows = chunks_per_step * q
    fwd = lambda c: pl.BlockSpec((1, rows, c), lambda b, i: (b, i, 0))
    bwd = lambda c: pl.BlockSpec((1, rows, c), lambda b, i: (b, steps - 1 - i, 0))
    y_f, y_b = pl.pallas_call(
        functools.partial(_ssd_scan_body, d_inner=d_inner),
        grid=(bsz, steps),
        in_specs=[fwd(conv_dim), fwd(2 * nh), bwd(conv_dim), bwd(2 * nh),
                  _const_spec((2, nh)), _const_spec((2, nh)), _const_spec((nh, d_inner))],
        out_specs=[fwd(d_inner), bwd(d_inner)],
        out_shape=[jax.ShapeDtypeStruct((bsz, length, d_inner), F32)] * 2,
        scratch_shapes=[pltpu.VMEM((SSM_STATE, d_inner), F32)] * 2,
        compiler_params=_params(("parallel", "arbitrary")),
        name="ssd_scan",
    )(xbc, dt3, xbc, dt3, dt_bias.astype(F32), a_log.astype(F32), expand)
    row_inputs = [(y_f.reshape(t, d_inner), d_inner), (y_b.reshape(t, d_inner), d_inner),
                  (xbc.reshape(t, conv_dim), d_inner), (z, d_inner)]
    consts = [_row(jnp.repeat(d_skip.astype(F32), SSM_HEAD_DIM)), _row(gate_norm)]
    return _ssd_out, row_inputs, consts, tail_tm


def _rwkv_in_body(h_ref, hp_ref, hn_ref, gpre_ref, mu_ref, wr_ref, wk_ref, wv_ref, w0_ref, w1_ref,
                  w2f_ref, w2b_ref, a0_ref, a1_ref, a2_ref, g1_ref, g2_ref, kk_ref, ka_ref,
                  rk_ref, seg_ref, segt_ref, tri_ref, up_ref,
                  v_out, bv_out, g_out, pendf_out, pendb_out, *dir_outs):
    i = pl.program_id(1)
    n_i = pl.num_programs(1)
    gpre = gpre_ref[...]
    x = _rms(h_ref[0], gpre)
    tm = x.shape[0]
    prev_row = _rms(hp_ref[0][SUBLANES - 1:SUBLANES, :], gpre) * jnp.where(i > 0, 1.0, 0.0)
    next_row = _rms(hn_ref[0][0:1, :], gpre) * jnp.where(i < n_i - 1, 1.0, 0.0)
    t = lax.broadcasted_iota(jnp.int32, x.shape, 0)
    x_prev = jnp.where(t == 0, prev_row, pltpu.roll(x, 1, axis=0))
    x_next = jnp.where(t == tm - 1, next_row, pltpu.roll(x, tm - 1, axis=0))
    xx = 0.5 * (x_prev + x_next) - x
    mix = lambda j: (x + xx * mu_ref[j:j + 1, :]).astype(BF16)
    xr, xw, xk, xv, xa, xg = (mix(j) for j in range(6))
    r = _dot(xr, wr_ref[...])
    k = _dot(xk, wk_ref[...])
    v = _dot(xv, wv_ref[...])
    lora_w = jnp.tanh(_dot(xw, w1_ref[...])).astype(BF16)
    log_decay = lambda w: -np.float32(np.exp(-0.5)) * jax.nn.sigmoid(w)
    lw_f = log_decay(w0_ref[0:1, :] + _dot(lora_w, w2f_ref[...]))
    lw_b = log_decay(w0_ref[1:2, :] + _dot(lora_w, w2b_ref[...]))
    a = jax.nn.sigmoid(a0_ref[...] + _dot(_dot(xa, a1_ref[...]).astype(BF16), a2_ref[...]))
    g_out[0] = _dot(jax.nn.sigmoid(_dot(xg, g1_ref[...])).astype(BF16), g2_ref[...])
    kk = k * kk_ref[...]
    norm = jnp.maximum(jnp.sqrt(_split_dot(kk * kk, seg_ref[...])), 1e-12)
    kk = kk * _split_dot(1.0 / norm, segt_ref[...])
    k = k * (1.0 + (a - 1.0) * ka_ref[...])
    rm = -kk
    add = kk * a
    v_out[0] = v.astype(v_out.dtype)
    bonus = _split_dot(_split_dot(r * k * rk_ref[...], seg_ref[...]), segt_ref[...])
    bv_out[0] = bonus * v
    c = RWKV_CHUNK
    for lw, cum, last, pend_out, outs in (
            (lw_f, _dot_split(tri_ref[...], lw_f), c - 1, pendf_out, dir_outs[:4]),
            (lw_b, lw_b + _dot_split(up_ref[...], lw_b), 0, pendb_out, dir_outs[4:])):
        e_neg = jnp.exp(-cum)
        values = (r * jnp.exp(cum), rm * jnp.exp(cum - lw), add * e_neg, k * e_neg)
        for o_ref, val in zip(outs, values):
            o_ref[0] = val.astype(o_ref.dtype)
        for m in range(tm // c):
            pend_out[0, m] = jnp.exp(cum[m * c + last:m * c + last + 1, :])


def _rwkv_chunk(r_t, rm_t, add_t, k_t, v_b, p_end, s_all, n_forward):
    c = r_t.shape[0]
    hd = RWKV_HEAD_DIM
    pair = 2 * hd
    assert c == hd and pair == LANES
    lane = lax.broadcasted_iota(jnp.int32, (c, pair), 1)
    ti = lax.broadcasted_iota(jnp.int32, (c, pair), 0)
    si = lane & (hd - 1)
    left = lane < hd
    keep_l = jnp.where(left, 1.0, 0.0).astype(BF16)
    keep_r = jnp.where(left, 0.0, 1.0).astype(BF16)
    bd = lambda x: jnp.concatenate([x * keep_l, x * keep_r], axis=0)
    eye = jnp.where(si == ti, 1.0, 0.0).astype(F32)
    pairs = range(r_t.shape[1] // pair)
    incl_of = lambda j: (si <= ti) if j < n_forward else (si >= ti)
    strict_of = lambda j: (si < ti) if j < n_forward else (si > ti)
    col = lambda x, j: x[:, j * pair:(j + 1) * pair]
    rows = lambda *xs: jnp.concatenate(xs, axis=0)
    v_bd = [bd(col(v_b, j)) for j in pairs]
    s0 = [col(s_all, j) for j in pairs]
    s0_bd = [bd(s.astype(BF16)) for s in s0]
    sc = [_dot_nt(rows(col(rm_t, j), col(r_t, j)), rows(bd(col(add_t, j)), bd(col(k_t, j)))) for j in pairs]
    n_mat = [jnp.where(strict_of(j), sc[j][:c, :pair], 0.0) for j in pairs]
    a_ak = [jnp.where(strict_of(j), sc[j][:c, pair:], 0.0).astype(BF16) for j in pairs]
    a_rb = [jnp.where(incl_of(j), sc[j][c:, :pair], 0.0).astype(BF16) for j in pairs]
    a_rk = [jnp.where(incl_of(j), sc[j][c:, pair:], 0.0).astype(BF16) for j in pairs]
    xkv = [_dot(rows(a_ak[j], a_rk[j]), v_bd[j]) for j in pairs]
    rs = [_dot_nt(col(r_t, j), s0_bd[j]) for j in pairs]
    inv = [eye + n_mat[j] for j in pairs]
    pw = [n_mat[j].astype(BF16) for j in pairs]
    pw = [_dot(pw[j], bd(pw[j])).astype(BF16) for j in pairs]
    span = 2
    while span < c:
        if 2 * span < c:
            both = [_dot(rows(pw[j], inv[j].astype(BF16)), bd(pw[j])) for j in pairs]
            pw = [both[j][:c].astype(BF16) for j in pairs]
            inv = [inv[j] + both[j][c:] for j in pairs]
        else:
            inv = [inv[j] + _dot(inv[j].astype(BF16), bd(pw[j])) for j in pairs]
        span *= 2
    wu = [_dot(inv[j].astype(BF16),
               jnp.concatenate([bd(col(rm_t, j)), bd(xkv[j][:c].astype(BF16))], axis=1)) for j in pairs]
    sa_b = [(_dot_nt(wu[j][:, :pair].astype(BF16), s0_bd[j]) + wu[j][:, pair:]).astype(BF16)
            for j in pairs]
    outs = [rs[j] + _dot(a_rb[j], bd(sa_b[j])) + xkv[j][c:] for j in pairs]
    full = [_dot_tn(rows(sa_b[j], col(v_b, j)), rows(col(add_t, j), col(k_t, j))) for j in pairs]
    states = [(s0[j] + jnp.where(left, full[j][:hd], full[j][hd:])) * col(p_end, j) for j in pairs]
    return jnp.concatenate(outs, axis=-1), jnp.concatenate(states, axis=-1)


def _rwkv_scan_body(*refs):
    f_in, b_in = refs[:6], refs[6:12]
    yf_ref, yb_ref, sf_ref, sb_ref = refs[12:]

    @pl.when(pl.program_id(2) == 0)
    def _():
        sf_ref[...] = jnp.zeros(sf_ref.shape, F32)
        sb_ref[...] = jnp.zeros(sb_ref.shape, F32)

    width = sf_ref.shape[1]
    c = RWKV_CHUNK
    n_sub = f_in[0].shape[1] // c
    both = lambda f, b: jnp.concatenate([f, b], axis=-1)
    state = both(sf_ref[...], sb_ref[...])
    for sub in range(n_sub):
        rf = slice(sub * c, (sub + 1) * c)
        rb = slice((n_sub - 1 - sub) * c, (n_sub - sub) * c)
        ops = [both(f[0, rf, :], b[0, rb, :]) for f, b in zip(f_in[:5], b_in[:5])]
        p_end = both(f_in[5][0, sub], b_in[5][0, n_sub - 1 - sub])
        y, state = _rwkv_chunk(*ops, p_end, state, width // LANES)
        yf_ref[0, rf, :] = y[:, :width]
        yb_ref[0, rb, :] = y[:, width:]
    sf_ref[...] = state[:, :width]
    sb_ref[...] = state[:, width:]


def _rwkv_out(yf_ref, yb_ref, bv_ref, g_ref, lnw_ref, lnb_ref, seg_ref, segt_ref):
    width = 2 * LANES
    blocks = []
    for j in range(yf_ref.shape[1] // width):
        cols = slice(j * width, (j + 1) * width)
        seg, segt = seg_ref[cols, :], segt_ref[:, cols]
        head_mean = lambda x: _split_dot(_split_dot(x, seg) * (1.0 / RWKV_HEAD_DIM), segt)
        y = yf_ref[:, cols] + yb_ref[:, cols]
        cen = y - head_mean(y)
        var = head_mean(cen * cen)
        y = cen * lax.rsqrt(var + RWKV_LN_EPS) * lnw_ref[:, cols] + lnb_ref[:, cols]
        blocks.append(((y + bv_ref[:, cols]) * g_ref[:, cols]).astype(BF16))
    return blocks


def _pad_to(w, axis, size):
    pad = [(0, 0)] * w.ndim
    pad[axis] = (0, size - w.shape[axis])
    return jnp.pad(w, pad)


def _rwkv_mixer(h3, g_pre, mu, w_rkv, w0, w1, w2, a0, a1, a2, g1, g2, k_k, k_a, r_k, ln_w, ln_b, *,
                tm=256, heads_per_step=16, chunks_per_step=4, tail_tm=512):
    bsz, length, d = h3.shape
    t = bsz * length
    n_heads = d // RWKV_HEAD_DIM
    c = RWKV_CHUNK
    nc = length // c
    hw = heads_per_step * RWKV_HEAD_DIM
    assert length % tm == 0 and length % (c * chunks_per_step) == 0 and n_heads % heads_per_step == 0
    assert n_heads <= LANES
    rank_w = w1.shape[-1]
    w1_cat = jnp.concatenate([w1[0], w1[1]], axis=-1).astype(BF16)
    w2f = jnp.concatenate([w2[0], jnp.zeros_like(w2[1])], axis=0).astype(BF16)
    w2b = jnp.concatenate([jnp.zeros_like(w2[0]), w2[1]], axis=0).astype(BF16)
    a1p = _pad_to(a1, 1, LANES).astype(BF16)
    a2p = _pad_to(a2, 0, LANES).astype(BF16)
    g_rank = -(-g1.shape[1] // LANES) * LANES
    g1p = _pad_to(g1, 1, g_rank).astype(BF16)
    g2p = _pad_to(g2, 0, g_rank).astype(BF16)
    seg = _pad_to(jnp.repeat(jnp.eye(n_heads, dtype=BF16), RWKV_HEAD_DIM, axis=0), 1, LANES)
    segt = seg.T
    hb = tm // SUBLANES
    nhb = length // SUBLANES
    pos = np.arange(tm)
    same_chunk = (pos[:, None] // c) == (pos[None, :] // c)
    tri = jnp.asarray(same_chunk & (pos[None, :] <= pos[:, None]), BF16)
    up = jnp.asarray(same_chunk & (pos[None, :] > pos[:, None]), BF16)
    cur = pl.BlockSpec((1, tm, d), lambda b, i: (b, i, 0))
    pend_spec = pl.BlockSpec((1, tm // c, 1, d), lambda b, i: (b, i, 0, 0))
    pend_shape = jax.ShapeDtypeStruct((bsz, nc, 1, d), F32)
    act = jax.ShapeDtypeStruct((bsz, length, d), BF16)
    outs = pl.pallas_call(
        _rwkv_in_body,
        grid=(bsz, length // tm),
        in_specs=[cur,
                  pl.BlockSpec((1, SUBLANES, d), lambda b, i: (b, jnp.maximum(i * hb - 1, 0), 0)),
                  pl.BlockSpec((1, SUBLANES, d), lambda b, i: (b, jnp.minimum((i + 1) * hb, nhb - 1), 0)),
                  _const_spec((1, d)), _const_spec((6, d)),
                  _const_spec((d, d)), _const_spec((d, d)), _const_spec((d, d)),
                  _const_spec((2, d)), _const_spec((d, 2 * rank_w)),
                  _const_spec((2 * rank_w, d)), _const_spec((2 * rank_w, d)),
                  _const_spec((1, d)), _const_spec((d, LANES)), _const_spec((LANES, d)),
                  _const_spec((d, g_rank)), _const_spec((g_rank, d)),
                  _const_spec((1, d)), _const_spec((1, d)), _const_spec((1, d)),
                  _const_spec((d, LANES)), _const_spec((LANES, d)),
                  _const_spec((tm, tm)), _const_spec((tm, tm))],
        out_specs=[cur, cur, cur, pend_spec, pend_spec] + [cur] * 8,
        out_shape=[act, jax.ShapeDtypeStruct((bsz, length, d), F32),
                   jax.ShapeDtypeStruct((bsz, length, d), F32), pend_shape, pend_shape] + [act] * 8,
        compiler_params=_params(("parallel", "parallel")),
        name="rwkv_in",
    )(h3, h3, h3, _row(g_pre), mu.astype(F32), w_rkv[0].astype(BF16), w_rkv[1].astype(BF16),
      w_rkv[2].astype(BF16), w0.astype(F32), w1_cat, w2f, w2b, _row(a0), a1p, a2p, g1p, g2p,
      _row(k_k), _row(k_a), _row(r_k), seg, segt, tri, up)
    v, bv, g, pend_f, pend_b = outs[:5]
    ops_f, ops_b = outs[5:9], outs[9:]
    steps = nc // chunks_per_step
    rows = chunks_per_step * c
    fwd = pl.BlockSpec((1, rows, hw), lambda b, hg, i: (b, i, hg))
    bwd = pl.BlockSpec((1, rows, hw), lambda b, hg, i: (b, steps - 1 - i, hg))
    fwd_p = pl.BlockSpec((1, chunks_per_step, 1, hw), lambda b, hg, i: (b, i, 0, hg))
    bwd_p = pl.BlockSpec((1, chunks_per_step, 1, hw), lambda b, hg, i: (b, steps - 1 - i, 0, hg))
    y_f, y_b = pl.pallas_call(
        _rwkv_scan_body,
        grid=(bsz, n_heads // heads_per_step, steps),
        in_specs=[fwd] * 5 + [fwd_p] + [bwd] * 5 + [bwd_p],
        out_specs=[fwd, bwd],
        out_shape=[jax.ShapeDtypeStruct((bsz, length, d), F32)] * 2,
        scratch_shapes=[pltpu.VMEM((RWKV_HEAD_DIM, hw), F32)] * 2,
        compiler_params=_params(("parallel", "parallel", "arbitrary")),
        name="rwkv_scan",
    )(*ops_f, v, pend_f, *ops_b, v, pend_b)
    row_inputs = [(x.reshape(t, d), d) for x in (y_f, y_b, bv, g)]
    return _rwkv_out, row_inputs, [_row(ln_w), _row(ln_b), seg, segt], tail_tm


def kernel(x_prompt, x_sample, mix_pre_norm, mix_post_norm, ffn_pre_norm, ffn_post_norm, mla_w_in, mla_q_norm, mla_w_qb, mla_kv_norm, mla_w_kvb, mla_w_o, ssm_w_in, ssm_conv_w, ssm_conv_b, ssm_a_log, ssm_dt_bias, ssm_d_skip, ssm_gate_norm, ssm_w_out, rwkv_mu, rwkv_w_rkv, rwkv_w0, rwkv_w1, rwkv_w2, rwkv_a0, rwkv_a1, rwkv_a2, rwkv_g1, rwkv_g2, rwkv_k_k, rwkv_k_a, rwkv_r_k, rwkv_ln_w, rwkv_ln_b, rwkv_w_o, ffn_w_in, ffn_w_out):
    depth = mix_pre_norm.shape[0]

    def trunk(x):
        bsz, length, d = x.shape
        h = x.reshape(bsz * length, d)
        for i in range(depth):
            kind, j = i % N_MIXERS, i // N_MIXERS
            h3 = h.reshape(bsz, length, d)
            if kind == 0:
                mixer = _mla_mixer(h3, mix_pre_norm[i], mla_w_in[j], mla_q_norm[j], mla_w_qb[j],
                                   mla_kv_norm[j], mla_w_kvb[j])
                w_o = mla_w_o[j]
            elif kind == 1:
                mixer = _ssd_mixer(h3, mix_pre_norm[i], ssm_w_in[j], ssm_conv_w[j], ssm_conv_b[j],
                                   ssm_a_log[j], ssm_dt_bias[j], ssm_d_skip[j], ssm_gate_norm[j])
                w_o = ssm_w_out[j]
            else:
                mixer = _rwkv_mixer(h3, mix_pre_norm[i], rwkv_mu[j], rwkv_w_rkv[j], rwkv_w0[j],
                                    rwkv_w1[j], rwkv_w2[j], rwkv_a0[j], rwkv_a1[j], rwkv_a2[j],
                                    rwkv_g1[j], rwkv_g2[j], rwkv_k_k[j], rwkv_k_a[j], rwkv_r_k[j],
                                    rwkv_ln_w[j], rwkv_ln_b[j])
                w_o = rwkv_w_o[j]
            h = _layer_tail(h, mixer, w_o, mix_post_norm[i], ffn_pre_norm[i], ffn_w_in[i],
                            ffn_w_out[i], ffn_post_norm[i])
        return h.reshape(bsz, length, d)

    return (trunk(x_prompt), trunk(x_sample))
```

```python
import functools

import jax
import jax.numpy as jnp
import numpy as np
from jax import lax
from jax.experimental import pallas as pl
from jax.experimental.pallas import tpu as pltpu

F32 = jnp.float32
BF16 = jnp.bfloat16

NORM_EPS = 1e-6
N_MIXERS = 3

VMEM_LIMIT_BYTES = 56 * 1024 * 1024
LANES = 128
SUBLANES = 8

MLA_HEADS = 8
MLA_NOPE = 128
MLA_ROPE = 64
MLA_V = 128
MLA_QK = MLA_NOPE + 2 * MLA_ROPE
ROPE_THETA = 10000.0

SSM_HEAD_DIM = 64
SSM_GROUPS = 4
SSM_STATE = 128
SSM_CONV = 5
SSM_CHUNK = 128

RWKV_HEAD_DIM = 64
RWKV_LN_EPS = 64e-5
RWKV_CHUNK = 64


def _rms(x, g, eps=NORM_EPS):
    return x * lax.rsqrt(jnp.mean(x * x, axis=-1, keepdims=True) + eps) * g


def _dot(a, b):
    return jnp.dot(a, b, preferred_element_type=F32)


def _dot_nt(a, b):
    return lax.dot_general(a, b, (((1,), (1,)), ((), ())), preferred_element_type=F32)


def _dot_tn(a, b):
    return lax.dot_general(a, b, (((0,), (0,)), ((), ())), preferred_element_type=F32)


def _bf16_terms(x, terms):
    pieces = []
    for _ in range(terms):
        p = x.astype(BF16)
        pieces.append(p)
        x = x - p.astype(F32)
    return pieces


def _dot_split(m, x, terms=2):
    return sum(_dot(m, p) for p in _bf16_terms(x, terms))


def _split_dot(x, m, terms=2):
    return sum(_dot(p, m) for p in _bf16_terms(x, terms))


def _silu(x):
    return x * jax.nn.sigmoid(x)


def _softplus(x):
    return jnp.maximum(x, 0.0) + jnp.log(1.0 + jnp.exp(-jnp.abs(x)))


def _const_spec(shape):
    nd = len(shape)
    return pl.BlockSpec(shape, lambda *_: (0,) * nd, pipeline_mode=pl.Buffered(1))


def _params(semantics):
    return pltpu.CompilerParams(dimension_semantics=semantics, vmem_limit_bytes=VMEM_LIMIT_BYTES)


def _row(v):
    return v.reshape(1, -1).astype(F32)


def _tail_body(*refs, n_mixer_refs, mixer_out, ff_chunk):
    h_ref = refs[0]
    wo_ref, gpost_ref, gpre_ref, win_ref, wout_ref, gfpost_ref, o_ref = refs[1 + n_mixer_refs:]
    d_ff = wout_ref.shape[0]
    m = jnp.zeros(h_ref.shape, F32)
    k0 = 0
    for blk in mixer_out(*refs[1:1 + n_mixer_refs]):
        m = m + _dot(blk, wo_ref[k0:k0 + blk.shape[1], :])
        k0 += blk.shape[1]
    h1 = h_ref[...] + _rms(m, gpost_ref[...])
    u = _rms(h1, gpre_ref[...]).astype(BF16)
    n_chunks = d_ff // ff_chunk
    gate_up = lambda c: (_dot(u, win_ref[:, c * ff_chunk:(c + 1) * ff_chunk]),
                         _dot(u, win_ref[:, d_ff + c * ff_chunk:d_ff + (c + 1) * ff_chunk]))
    acc = jnp.zeros(h1.shape, F32)
    nxt = gate_up(0)
    for c in range(n_chunks):
        gate, up = nxt
        if c + 1 < n_chunks:
            nxt = gate_up(c + 1)
        act = (_silu(gate) * up).astype(BF16)
        acc = acc + _dot(act, wout_ref[c * ff_chunk:(c + 1) * ff_chunk, :])
    o_ref[...] = h1 + _rms(acc, gfpost_ref[...])


def _layer_tail(h, mixer, w_o, g_post, g_pre, w_in, w_out, g_fpost, *, ff_chunk=256):
    mixer_out, row_inputs, const_inputs, tm = mixer
    t, d = h.shape
    ky = w_o.shape[0]
    d_ff = w_out.shape[0]
    assert t % tm == 0 and d_ff % ff_chunk == 0 and ff_chunk % LANES == 0
    row = lambda c: pl.BlockSpec((tm, c), lambda i: (i, 0))
    return pl.pallas_call(
        functools.partial(_tail_body, n_mixer_refs=len(row_inputs) + len(const_inputs),
                          mixer_out=mixer_out, ff_chunk=ff_chunk),
        grid=(t // tm,),
        in_specs=[row(d)] + [row(width) for _, width in row_inputs]
        + [_const_spec(c.shape) for c in const_inputs]
        + [_const_spec((ky, d)), _const_spec((1, d)), _const_spec((1, d)),
           _const_spec((d, 2 * d_ff)), _const_spec((d_ff, d)), _const_spec((1, d))],
        out_specs=row(d),
        out_shape=jax.ShapeDtypeStruct((t, d), F32),
        compiler_params=_params(("parallel",)),
        name="layer_tail",
    )(h, *[a for a, _ in row_inputs], *const_inputs, w_o.astype(BF16), _row(g_post), _row(g_pre),
      w_in.astype(BF16), w_out.astype(BF16), _row(g_fpost))


def _mla_in_body(h_ref, fq_ref, fk_ref, gpre_ref, win_ref, qn_ref, wq_ref, kvn_ref, wkv_ref,
                 q_ref, k_ref, v_ref, *, q_rank, kv_rank):
    u = _rms(h_ref[0], gpre_ref[...]).astype(BF16)
    lat = _dot(u, win_ref[...])
    q_lat = _rms(lat[:, :q_rank], qn_ref[...]).astype(BF16)
    kv_lat = _rms(lat[:, q_rank:q_rank + kv_rank], kvn_ref[...]).astype(BF16)
    kr = lat[:, q_rank + kv_rank:q_rank + kv_rank + LANES]
    kr_rot = lat[:, q_rank + kv_rank + LANES:q_rank + kv_rank + 2 * LANES]
    fk = fk_ref[...]
    k_rope = (kr * fk[:, :LANES] + kr_rot * fk[:, LANES:]).astype(BF16)
    fq = fq_ref[...]
    q = _dot(q_lat, wq_ref[...])
    kv = _dot(kv_lat, wkv_ref[...])
    for hd in range(MLA_HEADS):
        q_ref[0, :, hd * MLA_QK:(hd + 1) * MLA_QK] = (
            q[:, hd * MLA_QK:(hd + 1) * MLA_QK] * fq).astype(BF16)
        k_ref[0, :, hd * MLA_QK:hd * MLA_QK + MLA_NOPE] = (
            kv[:, hd * MLA_NOPE:(hd + 1) * MLA_NOPE].astype(BF16))
        k_ref[0, :, hd * MLA_QK + MLA_NOPE:(hd + 1) * MLA_QK] = k_rope
    v_ref[0] = jnp.transpose(kv[:, MLA_HEADS * MLA_NOPE:]).astype(BF16)


def _mla_attn_body(q_ref, k_ref, vt_ref, o_ref, *, q_sub, kv_chunk):
    n_chunks = k_ref.shape[1] // kv_chunk
    n_sub = q_ref.shape[1] // q_sub
    kv_rows = lambda c: slice(c * kv_chunk, (c + 1) * kv_chunk)
    score = lambda t: _dot_nt(k_ref[0], q_ref[0, t * q_sub:(t + 1) * q_sub, :])
    s_cur = score(0)
    for t in range(n_sub):
        s_next = score(t + 1) if t + 1 < n_sub else None
        m = jnp.max(s_cur, axis=0, keepdims=True)
        l = jnp.zeros((1, q_sub), F32)
        acc = jnp.zeros((vt_ref.shape[1], q_sub), F32)
        for c in range(n_chunks):
            p = jnp.exp2(s_cur[kv_rows(c), :] - m)
            l = l + jnp.sum(p, axis=0, keepdims=True)
            acc = acc + _dot(vt_ref[0, :, kv_rows(c)], p.astype(BF16))
        o_ref[0, t * q_sub:(t + 1) * q_sub, :] = jnp.transpose(acc / l).astype(o_ref.dtype)
        s_cur = s_next


def _rot_cols(w):
    half = w.shape[-1] // 2
    return jnp.concatenate([-w[..., half:], w[..., :half]], axis=-1)


def _mla_mixer(h3, g_pre, w_in, q_norm, w_qb, kv_norm, w_kvb, *, tm=512, tq=4096, q_sub=256,
               kv_chunk=512, tail_tm=512):
    bsz, length, d = h3.shape
    q_rank, kv_rank = q_norm.shape[0], kv_norm.shape[0]
    hd_q = MLA_NOPE + MLA_ROPE
    inv = 1.0 / (ROPE_THETA ** (jnp.arange(0, MLA_ROPE, 2, dtype=F32) / MLA_ROPE))
    ang = jnp.arange(length, dtype=F32)[:, None] * inv[None, :]
    cc = jnp.concatenate([jnp.cos(ang), jnp.cos(ang)], axis=-1)
    ss = jnp.concatenate([jnp.sin(ang), jnp.sin(ang)], axis=-1)
    scale = hd_q ** -0.5 * np.log2(np.e)
    fq = jnp.concatenate([jnp.full((length, MLA_NOPE), scale, F32), cc * scale, ss * scale], axis=-1)
    fk = jnp.concatenate([cc, cc, ss, ss], axis=-1)
    w_kr = w_in[:, q_rank + kv_rank:]
    w_in_ext = jnp.concatenate([w_in[:, :q_rank + kv_rank], w_kr, w_kr, _rot_cols(w_kr), _rot_cols(w_kr)],
                               axis=-1).astype(BF16)
    wq = w_qb.reshape(q_rank, MLA_HEADS, hd_q)
    wq_ext = jnp.concatenate([wq, _rot_cols(wq[..., MLA_NOPE:])], axis=-1).reshape(
        q_rank, MLA_HEADS * MLA_QK).astype(BF16)
    wkv = w_kvb.reshape(kv_rank, MLA_HEADS, MLA_NOPE + MLA_V)
    wkv_perm = jnp.concatenate([wkv[..., :MLA_NOPE].reshape(kv_rank, -1),
                                wkv[..., MLA_NOPE:].reshape(kv_rank, -1)], axis=-1).astype(BF16)
    n_in = w_in_ext.shape[1]
    tq = min(tq, length)
    assert length % tm == 0 and length % tq == 0
    q, k, v = pl.pallas_call(
        functools.partial(_mla_in_body, q_rank=q_rank, kv_rank=kv_rank),
        grid=(bsz, length // tm),
        in_specs=[pl.BlockSpec((1, tm, d), lambda b, i: (b, i, 0)),
                  pl.BlockSpec((tm, MLA_QK), lambda b, i: (i, 0)),
                  pl.BlockSpec((tm, MLA_QK), lambda b, i: (i, 0)),
                  _const_spec((1, d)), _const_spec((d, n_in)), _const_spec((1, q_rank)),
                  _const_spec((q_rank, MLA_HEADS * MLA_QK)), _const_spec((1, kv_rank)),
                  _const_spec((kv_rank, MLA_HEADS * (MLA_NOPE + MLA_V)))],
        out_specs=[pl.BlockSpec((1, tm, MLA_HEADS * MLA_QK), lambda b, i: (b, i, 0)),
                   pl.BlockSpec((1, tm, MLA_HEADS * MLA_QK), lambda b, i: (b, i, 0)),
                   pl.BlockSpec((1, MLA_HEADS * MLA_V, tm), lambda b, i: (b, 0, i))],
        out_shape=[jax.ShapeDtypeStruct((bsz, length, MLA_HEADS * MLA_QK), BF16),
                   jax.ShapeDtypeStruct((bsz, length, MLA_HEADS * MLA_QK), BF16),
                   jax.ShapeDtypeStruct((bsz, MLA_HEADS * MLA_V, length), BF16)],
        compiler_params=_params(("parallel", "parallel")),
        name="mla_in",
    )(h3, fq, fk, _row(g_pre), w_in_ext, _row(q_norm), wq_ext, _row(kv_norm), wkv_perm)
    o = pl.pallas_call(
        functools.partial(_mla_attn_body, q_sub=q_sub, kv_chunk=kv_chunk),
        grid=(bsz, MLA_HEADS, length // tq),
        in_specs=[pl.BlockSpec((1, tq, MLA_QK), lambda b, hd, i: (b, i, hd)),
                  pl.BlockSpec((1, length, MLA_QK), lambda b, hd, i: (b, 0, hd)),
                  pl.BlockSpec((1, MLA_V, length), lambda b, hd, i: (b, hd, 0))],
        out_specs=pl.BlockSpec((1, tq, MLA_V), lambda b, hd, i: (b, i, hd)),
        out_shape=jax.ShapeDtypeStruct((bsz, length, MLA_HEADS * MLA_V), BF16),
        compiler_params=_params(("parallel", "parallel", "parallel")),
        name="mla_attn",
    )(q, k, v)
    width = MLA_HEADS * MLA_V
    return (lambda o_ref: [o_ref[...]]), [(o.reshape(bsz * length, width), width)], [], tail_tm


def _ssd_in_body(h_ref, hp_ref, hn_ref, gpre_ref, wz_ref, wx_ref, wdt_ref, cw_ref, cb_ref,
                 z_ref, xbc_ref, dt_ref, xpad_ref, *, col_block):
    i = pl.program_id(1)
    n_i = pl.num_programs(1)
    gpre = gpre_ref[...]
    u = _rms(h_ref[0], gpre).astype(BF16)
    halo = _rms(jnp.concatenate([hp_ref[0], hn_ref[0]], axis=0), gpre).astype(BF16)
    tm = u.shape[0]
    pad = SSM_CONV // 2
    keep_prev = jnp.where(i > 0, 1.0, 0.0)
    keep_next = jnp.where(i < n_i - 1, 1.0, 0.0)
    n_blocks = xbc_ref.shape[2] // col_block
    z_blocks = z_ref.shape[2] // col_block
    cols = lambda j: slice(j * col_block, (j + 1) * col_block)
    proj = lambda j: (_dot(u, wx_ref[:, cols(j)]), _dot(halo, wx_ref[:, cols(j)]))

    def conv(j, x, x_halo):
        pad_ref = xpad_ref.at[j % 2]
        pad_ref[0:SUBLANES, :] = x_halo[:SUBLANES] * keep_prev
        pad_ref[SUBLANES:SUBLANES + tm, :] = x
        pad_ref[SUBLANES + tm:, :] = x_halo[SUBLANES:] * keep_next
        acc = x * cw_ref[pad:pad + 1, cols(j)] + cb_ref[:, cols(j)]
        for kk in range(SSM_CONV):
            off = kk - pad
            if off != 0:
                acc = acc + pad_ref[SUBLANES + off:SUBLANES + off + tm, :] * cw_ref[kk:kk + 1, cols(j)]
        xbc_ref[0, :, cols(j)] = _silu(acc)

    nxt = proj(0)
    for j in range(n_blocks):
        cur = nxt
        if j + 1 < n_blocks:
            nxt = proj(j + 1)
        if j < z_blocks:
            z_ref[0, :, cols(j)] = _dot(u, wz_ref[:, cols(j)])
        conv(j, *cur)
    for j in range(n_blocks, z_blocks):
        z_ref[0, :, cols(j)] = _dot(u, wz_ref[:, cols(j)])
    dt_ref[0] = _dot(u, wdt_ref[...])


def _ssd_direction(xbc, dt_raw, bias, a_neg, expand, state_ref, reverse, d_inner):
    q = xbc.shape[0]
    gn = SSM_GROUPS * SSM_STATE
    hpg = d_inner // SSM_HEAD_DIM // SSM_GROUPS
    xs = xbc[:, :d_inner]
    b_in = xbc[:, d_inner:d_inner + gn]
    c_in = xbc[:, d_inner + gn:]
    dt = _softplus(dt_raw + bias)
    a_dt = dt * a_neg
    ti = lax.broadcasted_iota(jnp.int32, (q, q), 0)
    si = lax.broadcasted_iota(jnp.int32, (q, q), 1)
    keep = (si >= ti) if reverse else (si <= ti)
    tri = jnp.where(keep, 1.0, 0.0).astype(BF16)
    cum = _dot_split(tri, a_dt, terms=3)
    cum_t = jnp.transpose(cum)
    dt_t = jnp.transpose(dt)
    edge = cum[0:1, :] if reverse else cum[q - 1:q, :]
    x_end = (xs * _split_dot(dt * jnp.exp(edge - cum), expand)).astype(BF16)
    scale_in = _split_dot(jnp.exp(cum), expand)
    decay_state = _split_dot(jnp.exp(edge), expand)
    xs_b = xs.astype(BF16)
    gw = hpg * SSM_HEAD_DIM
    groups = range(SSM_GROUPS)
    gcols = lambda g: slice(g * gw, (g + 1) * gw)
    bg = [b_in[:, g * SSM_STATE:(g + 1) * SSM_STATE].astype(BF16) for g in groups]
    cg = [c_in[:, g * SSM_STATE:(g + 1) * SSM_STATE].astype(BF16) for g in groups]
    st = [state_ref[:, gcols(g)] for g in groups]
    cb = [_dot_nt(cg[g], bg[g]) for g in groups]
    y_off = [_dot(cg[g], st[g].astype(BF16)) * scale_in[:, gcols(g)] for g in groups]
    lane = lax.broadcasted_iota(jnp.int32, (q, 2 * SSM_HEAD_DIM), 1)
    keep_l = jnp.where(lane < SSM_HEAD_DIM, 1.0, 0.0).astype(BF16)
    keep_r = jnp.where(lane < SSM_HEAD_DIM, 0.0, 1.0).astype(BF16)
    outs = []
    for pr in range(d_inner // (2 * SSM_HEAD_DIM)):
        m_pair = []
        for hd in (2 * pr, 2 * pr + 1):
            seg = cum[:, hd:hd + 1] - cum_t[hd:hd + 1, :]
            decay = jnp.exp(jnp.where(keep, seg, -jnp.inf))
            m_pair.append((cb[hd // hpg] * decay * dt_t[hd:hd + 1, :]).astype(BF16))
        x_pair = xs_b[:, pr * 2 * SSM_HEAD_DIM:(pr + 1) * 2 * SSM_HEAD_DIM]
        x_bd = jnp.concatenate([x_pair * keep_l, x_pair * keep_r], axis=0)
        outs.append(_dot(jnp.concatenate(m_pair, axis=1), x_bd))
    y = jnp.concatenate(outs, axis=-1) + jnp.concatenate(y_off, axis=-1)
    new_state = [st[g] * decay_state[:, gcols(g)] + _dot_tn(bg[g], x_end[:, gcols(g)]) for g in groups]
    for g in groups:
        state_ref[:, gcols(g)] = new_state[g]
    return y


def _ssd_scan_body(xf_ref, dtf_ref, xb_ref, dtb_ref, bias_ref, alog_ref, exp_ref,
                   yf_ref, yb_ref, sf_ref, sb_ref, *, d_inner):
    @pl.when(pl.program_id(1) == 0)
    def _():
        sf_ref[...] = jnp.zeros(sf_ref.shape, F32)
        sb_ref[...] = jnp.zeros(sb_ref.shape, F32)

    nh = d_inner // SSM_HEAD_DIM
    a_neg = -jnp.exp(alog_ref[...])
    expand = exp_ref[...]
    q = SSM_CHUNK
    n_sub = xf_ref.shape[1] // q
    for sub in range(n_sub):
        rf = slice(sub * q, (sub + 1) * q)
        rb = slice((n_sub - 1 - sub) * q, (n_sub - sub) * q)
        yf_ref[0, rf, :] = _ssd_direction(xf_ref[0, rf, :], dtf_ref[0, rf, :][:, :nh], bias_ref[0:1, :],
                                          a_neg[0:1, :], expand, sf_ref, False, d_inner)
        yb_ref[0, rb, :] = _ssd_direction(xb_ref[0, rb, :], dtb_ref[0, rb, :][:, nh:], bias_ref[1:2, :],
                                          a_neg[1:2, :], expand, sb_ref, True, d_inner)


def _ssd_out(yf_ref, yb_ref, xs_ref, z_ref, dsk_ref, gn_ref):
    gw = yf_ref.shape[1] // SSM_GROUPS
    blocks = []
    for g in range(SSM_GROUPS):
        cols = slice(g * gw, (g + 1) * gw)
        y = (yf_ref[:, cols] + yb_ref[:, cols] + xs_ref[:, cols] * dsk_ref[:, cols]) * _silu(z_ref[:, cols])
        blocks.append(_rms(y, gn_ref[:, cols]).astype(BF16))
    return blocks


def _ssd_mixer(h3, g_pre, w_in, conv_w, conv_b, a_log, dt_bias, d_skip, gate_norm, *, tm=512,
               col_block=512, chunks_per_step=2, tail_tm=256):
    bsz, length, d = h3.shape
    t = bsz * length
    nh = a_log.shape[-1]
    d_inner = nh * SSM_HEAD_DIM
    conv_dim = d_inner + 2 * SSM_GROUPS * SSM_STATE
    q = SSM_CHUNK
    nc = length // q
    assert length % tm == 0 and length % (q * chunks_per_step) == 0
    hb = tm // SUBLANES
    nhb = length // SUBLANES
    cur = lambda c: pl.BlockSpec((1, tm, c), lambda b, i: (b, i, 0))
    z, xbc, dt3 = pl.pallas_call(
        functools.partial(_ssd_in_body, col_block=col_block),
        grid=(bsz, length // tm),
        in_specs=[cur(d),
                  pl.BlockSpec((1, SUBLANES, d), lambda b, i: (b, jnp.maximum(i * hb - 1, 0), 0)),
                  pl.BlockSpec((1, SUBLANES, d), lambda b, i: (b, jnp.minimum((i + 1) * hb, nhb - 1), 0)),
                  _const_spec((1, d)), _const_spec((d, d_inner)), _const_spec((d, conv_dim)),
                  _const_spec((d, 2 * nh)), _const_spec((SSM_CONV, conv_dim)), _const_spec((1, conv_dim))],
        out_specs=[cur(d_inner), cur(conv_dim), cur(2 * nh)],
        out_shape=[jax.ShapeDtypeStruct((bsz, length, d_inner), F32),
                   jax.ShapeDtypeStruct((bsz, length, conv_dim), F32),
                   jax.ShapeDtypeStruct((bsz, length, 2 * nh), F32)],
        scratch_shapes=[pltpu.VMEM((2, tm + 2 * SUBLANES, col_block), F32)],
        compiler_params=_params(("parallel", "parallel")),
        name="ssd_in",
    )(h3, h3, h3, _row(g_pre), w_in[:, :d_inner].astype(BF16),
      w_in[:, d_inner:d_inner + conv_dim].astype(BF16), w_in[:, d_inner + conv_dim:].astype(BF16),
      conv_w.astype(F32), _row(conv_b))
    z = z.reshape(t, d_inner)
    expand = jnp.repeat(jnp.eye(nh, dtype=BF16), SSM_HEAD_DIM, axis=1)
    steps = nc // chunks_per_step
    rows = chunks_per_step * q
    fwd = lambda c: pl.BlockSpec((1, rows, c), lambda b, i: (b, i, 0))
    bwd = lambda c: pl.BlockSpec((1, rows, c), lambda b, i: (b, steps - 1 - i, 0))
    y_f, y_b = pl.pallas_call(
        functools.partial(_ssd_scan_body, d_inner=d_inner),
        grid=(bsz, steps),
        in_specs=[fwd(conv_dim), fwd(2 * nh), bwd(conv_dim), bwd(2 * nh),
                  _const_spec((2, nh)), _const_spec((2, nh)), _const_spec((nh, d_inner))],
        out_specs=[fwd(d_inner), bwd(d_inner)],
        out_shape=[jax.ShapeDtypeStruct((bsz, length, d_inner), F32)] * 2,
        scratch_shapes=[pltpu.VMEM((SSM_STATE, d_inner), F32)] * 2,
        compiler_params=_params(("parallel", "arbitrary")),
        name="ssd_scan",
    )(xbc, dt3, xbc, dt3, dt_bias.astype(F32), a_log.astype(F32), expand)
    row_inputs = [(y_f.reshape(t, d_inner), d_inner), (y_b.reshape(t, d_inner), d_inner),
                  (xbc.reshape(t, conv_dim), d_inner), (z, d_inner)]
    consts = [_row(jnp.repeat(d_skip.astype(F32), SSM_HEAD_DIM)), _row(gate_norm)]
    return _ssd_out, row_inputs, consts, tail_tm


def _rwkv_in_body(h_ref, hp_ref, hn_ref, gpre_ref, mu_ref, wr_ref, wk_ref, wv_ref, w0_ref, w1_ref,
                  w2f_ref, w2b_ref, a0_ref, a1_ref, a2_ref, g1_ref, g2_ref, kk_ref, ka_ref,
                  rk_ref, seg_ref, segt_ref, tri_ref, up_ref,
                  v_out, bv_out, g_out, pendf_out, pendb_out, *dir_outs):
    i = pl.program_id(1)
    n_i = pl.num_programs(1)
    gpre = gpre_ref[...]
    x = _rms(h_ref[0], gpre)
    tm = x.shape[0]
    prev_row = _rms(hp_ref[0][SUBLANES - 1:SUBLANES, :], gpre) * jnp.where(i > 0, 1.0, 0.0)
    next_row = _rms(hn_ref[0][0:1, :], gpre) * jnp.where(i < n_i - 1, 1.0, 0.0)
    t = lax.broadcasted_iota(jnp.int32, x.shape, 0)
    x_prev = jnp.where(t == 0, prev_row, pltpu.roll(x, 1, axis=0))
    x_next = jnp.where(t == tm - 1, next_row, pltpu.roll(x, tm - 1, axis=0))
    xx = 0.5 * (x_prev + x_next) - x
    mix = lambda j: (x + xx * mu_ref[j:j + 1, :]).astype(BF16)
    xr, xw, xk, xv, xa, xg = (mix(j) for j in range(6))
    r = _dot(xr, wr_ref[...])
    k = _dot(xk, wk_ref[...])
    v = _dot(xv, wv_ref[...])
    lora_w = jnp.tanh(_dot(xw, w1_ref[...])).astype(BF16)
    log_decay = lambda w: -np.float32(np.exp(-0.5)) * jax.nn.sigmoid(w)
    lw_f = log_decay(w0_ref[0:1, :] + _dot(lora_w, w2f_ref[...]))
    lw_b = log_decay(w0_ref[1:2, :] + _dot(lora_w, w2b_ref[...]))
    a = jax.nn.sigmoid(a0_ref[...] + _dot(_dot(xa, a1_ref[...]).astype(BF16), a2_ref[...]))
    g_out[0] = _dot(jax.nn.sigmoid(_dot(xg, g1_ref[...])).astype(BF16), g2_ref[...])
    kk = k * kk_ref[...]
    norm = jnp.maximum(jnp.sqrt(_split_dot(kk * kk, seg_ref[...])), 1e-12)
    kk = kk * _split_dot(1.0 / norm, segt_ref[...])
    k = k * (1.0 + (a - 1.0) * ka_ref[...])
    rm = -kk
    add = kk * a
    v_out[0] = v.astype(v_out.dtype)
    bonus = _split_dot(_split_dot(r * k * rk_ref[...], seg_ref[...]), segt_ref[...])
    bv_out[0] = bonus * v
    c = RWKV_CHUNK
    for lw, cum, last, pend_out, outs in (
            (lw_f, _dot_split(tri_ref[...], lw_f), c - 1, pendf_out, dir_outs[:4]),
            (lw_b, lw_b + _dot_split(up_ref[...], lw_b), 0, pendb_out, dir_outs[4:])):
        e_neg = jnp.exp(-cum)
        values = (r * jnp.exp(cum), rm * jnp.exp(cum - lw), add * e_neg, k * e_neg)
        for o_ref, val in zip(outs, values):
            o_ref[0] = val.astype(o_ref.dtype)
        for m in range(tm // c):
            pend_out[0, m] = jnp.exp(cum[m * c + last:m * c + last + 1, :])


def _rwkv_chunk(r_t, rm_t, add_t, k_t, v_b, p_end, s_all, n_forward):
    c = r_t.shape[0]
    hd = RWKV_HEAD_DIM
    pair = 2 * hd
    assert c == hd and pair == LANES
    lane = lax.broadcasted_iota(jnp.int32, (c, pair), 1)
    ti = lax.broadcasted_iota(jnp.int32, (c, pair), 0)
    si = lane & (hd - 1)
    left = lane < hd
    keep_l = jnp.where(left, 1.0, 0.0).astype(BF16)
    keep_r = jnp.where(left, 0.0, 1.0).astype(BF16)
    bd = lambda x: jnp.concatenate([x * keep_l, x * keep_r], axis=0)
    eye = jnp.where(si == ti, 1.0, 0.0).astype(F32)
    pairs = range(r_t.shape[1] // pair)
    incl_of = lambda j: (si <= ti) if j < n_forward else (si >= ti)
    strict_of = lambda j: (si < ti) if j < n_forward else (si > ti)
    col = lambda x, j: x[:, j * pair:(j + 1) * pair]
    rows = lambda *xs: jnp.concatenate(xs, axis=0)
    v_bd = [bd(col(v_b, j)) for j in pairs]
    s0 = [col(s_all, j) for j in pairs]
    s0_bd = [bd(s.astype(BF16)) for s in s0]
    sc = [_dot_nt(rows(col(rm_t, j), col(r_t, j)), rows(bd(col(add_t, j)), bd(col(k_t, j)))) for j in pairs]
    n_mat = [jnp.where(strict_of(j), sc[j][:c, :pair], 0.0) for j in pairs]
    a_ak = [jnp.where(strict_of(j), sc[j][:c, pair:], 0.0).astype(BF16) for j in pairs]
    a_rb = [jnp.where(incl_of(j), sc[j][c:, :pair], 0.0).astype(BF16) for j in pairs]
    a_rk = [jnp.where(incl_of(j), sc[j][c:, pair:], 0.0).astype(BF16) for j in pairs]
    xkv = [_dot(rows(a_ak[j], a_rk[j]), v_bd[j]) for j in pairs]
    rs = [_dot_nt(col(r_t, j), s0_bd[j]) for j in pairs]
    inv = [eye + n_mat[j] for j in pairs]
    pw = [n_mat[j].astype(BF16) for j in pairs]
    pw = [_dot(pw[j], bd(pw[j])).astype(BF16) for j in pairs]
    span = 2
    while span < c:
        if 2 * span < c:
            both = [_dot(rows(pw[j], inv[j].astype(BF16)), bd(pw[j])) for j in pairs]
            pw = [both[j][:c].astype(BF16) for j in pairs]
            inv = [inv[j] + both[j][c:] for j in pairs]
        else:
            inv = [inv[j] + _dot(inv[j].astype(BF16), bd(pw[j])) for j in pairs]
        span *= 2
    wu = [_dot(inv[j].astype(BF16),
               jnp.concatenate([bd(col(rm_t, j)), bd(xkv[j][:c].astype(BF16))], axis=1)) for j in pairs]
    sa_b = [(_dot_nt(wu[j][:, :pair].astype(BF16), s0_bd[j]) + wu[j][:, pair:]).astype(BF16)
            for j in pairs]
    outs = [rs[j] + _dot(a_rb[j], bd(sa_b[j])) + xkv[j][c:] for j in pairs]
    full = [_dot_tn(rows(sa_b[j], col(v_b, j)), rows(col(add_t, j), col(k_t, j))) for j in pairs]
    states = [(s0[j] + jnp.where(left, full[j][:hd], full[j][hd:])) * col(p_end, j) for j in pairs]
    return jnp.concatenate(outs, axis=-1), jnp.concatenate(states, axis=-1)


def _rwkv_scan_body(*refs):
    f_in, b_in = refs[:6], refs[6:12]
    yf_ref, yb_ref, sf_ref, sb_ref = refs[12:]

    @pl.when(pl.program_id(2) == 0)
    def _():
        sf_ref[...] = jnp.zeros(sf_ref.shape, F32)
        sb_ref[...] = jnp.zeros(sb_ref.shape, F32)

    width = sf_ref.shape[1]
    c = RWKV_CHUNK
    n_sub = f_in[0].shape[1] // c
    both = lambda f, b: jnp.concatenate([f, b], axis=-1)
    state = both(sf_ref[...], sb_ref[...])
    for sub in range(n_sub):
        rf = slice(sub * c, (sub + 1) * c)
        rb = slice((n_sub - 1 - sub) * c, (n_sub - sub) * c)
        ops = [both(f[0, rf, :], b[0, rb, :]) for f, b in zip(f_in[:5], b_in[:5])]
        p_end = both(f_in[5][0, sub], b_in[5][0, n_sub - 1 - sub])
        y, state = _rwkv_chunk(*ops, p_end, state, width // LANES)
        yf_ref[0, rf, :] = y[:, :width]
        yb_ref[0, rb, :] = y[:, width:]
    sf_ref[...] = state[:, :width]
    sb_ref[...] = state[:, width:]


def _rwkv_out(yf_ref, yb_ref, bv_ref, g_ref, lnw_ref, lnb_ref, seg_ref, segt_ref):
    width = 2 * LANES
    blocks = []
    for j in range(yf_ref.shape[1] // width):
        cols = slice(j * width, (j + 1) * width)
        seg, segt = seg_ref[cols, :], segt_ref[:, cols]
        head_mean = lambda x: _split_dot(_split_dot(x, seg) * (1.0 / RWKV_HEAD_DIM), segt)
        y = yf_ref[:, cols] + yb_ref[:, cols]
        cen = y - head_mean(y)
        var = head_mean(cen * cen)
        y = cen * lax.rsqrt(var + RWKV_LN_EPS) * lnw_ref[:, cols] + lnb_ref[:, cols]
        blocks.append(((y + bv_ref[:, cols]) * g_ref[:, cols]).astype(BF16))
    return blocks


def _pad_to(w, axis, size):
    pad = [(0, 0)] * w.ndim
    pad[axis] = (0, size - w.shape[axis])
    return jnp.pad(w, pad)


def _rwkv_mixer(h3, g_pre, mu, w_rkv, w0, w1, w2, a0, a1, a2, g1, g2, k_k, k_a, r_k, ln_w, ln_b, *,
                tm=256, heads_per_step=16, chunks_per_step=8, tail_tm=512):
    bsz, length, d = h3.shape
    t = bsz * length
    n_heads = d // RWKV_HEAD_DIM
    c = RWKV_CHUNK
    nc = length // c
    hw = heads_per_step * RWKV_HEAD_DIM
    assert length % tm == 0 and length % (c * chunks_per_step) == 0 and n_heads % heads_per_step == 0
    assert n_heads <= LANES
    rank_w = w1.shape[-1]
    w1_cat = jnp.concatenate([w1[0], w1[1]], axis=-1).astype(BF16)
    w2f = jnp.concatenate([w2[0], jnp.zeros_like(w2[1])], axis=0).astype(BF16)
    w2b = jnp.concatenate([jnp.zeros_like(w2[0]), w2[1]], axis=0).astype(BF16)
    a1p = _pad_to(a1, 1, LANES).astype(BF16)
    a2p = _pad_to(a2, 0, LANES).astype(BF16)
    g_rank = -(-g1.shape[1] // LANES) * LANES
    g1p = _pad_to(g1, 1, g_rank).astype(BF16)
    g2p = _pad_to(g2, 0, g_rank).astype(BF16)
    seg = _pad_to(jnp.repeat(jnp.eye(n_heads, dtype=BF16), RWKV_HEAD_DIM, axis=0), 1, LANES)
    segt = seg.T
    hb = tm // SUBLANES
    nhb = length // SUBLANES
    pos = np.arange(tm)
    same_chunk = (pos[:, None] // c) == (pos[None, :] // c)
    tri = jnp.asarray(same_chunk & (pos[None, :] <= pos[:, None]), BF16)
    up = jnp.asarray(same_chunk & (pos[None, :] > pos[:, None]), BF16)
    cur = pl.BlockSpec((1, tm, d), lambda b, i: (b, i, 0))
    pend_spec = pl.BlockSpec((1, tm // c, 1, d), lambda b, i: (b, i, 0, 0))
    pend_shape = jax.ShapeDtypeStruct((bsz, nc, 1, d), F32)
    act = jax.ShapeDtypeStruct((bsz, length, d), BF16)
    outs = pl.pallas_call(
        _rwkv_in_body,
        grid=(bsz, length // tm),
        in_specs=[cur,
                  pl.BlockSpec((1, SUBLANES, d), lambda b, i: (b, jnp.maximum(i * hb - 1, 0), 0)),
                  pl.BlockSpec((1, SUBLANES, d), lambda b, i: (b, jnp.minimum((i + 1) * hb, nhb - 1), 0)),
                  _const_spec((1, d)), _const_spec((6, d)),
                  _const_spec((d, d)), _const_spec((d, d)), _const_spec((d, d)),
                  _const_spec((2, d)), _const_spec((d, 2 * rank_w)),
                  _const_spec((2 * rank_w, d)), _const_spec((2 * rank_w, d)),
                  _const_spec((1, d)), _const_spec((d, LANES)), _const_spec((LANES, d)),
                  _const_spec((d, g_rank)), _const_spec((g_rank, d)),
                  _const_spec((1, d)), _const_spec((1, d)), _const_spec((1, d)),
                  _const_spec((d, LANES)), _const_spec((LANES, d)),
                  _const_spec((tm, tm)), _const_spec((tm, tm))],
        out_specs=[cur, cur, cur, pend_spec, pend_spec] + [cur] * 8,
        out_shape=[act, jax.ShapeDtypeStruct((bsz, length, d), F32),
                   jax.ShapeDtypeStruct((bsz, length, d), F32), pend_shape, pend_shape] + [act] * 8,
        compiler_params=_params(("parallel", "parallel")),
        name="rwkv_in",
    )(h3, h3, h3, _row(g_pre), mu.astype(F32), w_rkv[0].astype(BF16), w_rkv[1].astype(BF16),
      w_rkv[2].astype(BF16), w0.astype(F32), w1_cat, w2f, w2b, _row(a0), a1p, a2p, g1p, g2p,
      _row(k_k), _row(k_a), _row(r_k), seg, segt, tri, up)
    v, bv, g, pend_f, pend_b = outs[:5]
    ops_f, ops_b = outs[5:9], outs[9:]
    steps = nc // chunks_per_step
    rows = chunks_per_step * c
    fwd = pl.BlockSpec((1, rows, hw), lambda b, hg, i: (b, i, hg))
    bwd = pl.BlockSpec((1, rows, hw), lambda b, hg, i: (b, steps - 1 - i, hg))
    fwd_p = pl.BlockSpec((1, chunks_per_step, 1, hw), lambda b, hg, i: (b, i, 0, hg))
    bwd_p = pl.BlockSpec((1, chunks_per_step, 1, hw), lambda b, hg, i: (b, steps - 1 - i, 0, hg))
    y_f, y_b = pl.pallas_call(
        _rwkv_scan_body,
        grid=(bsz, n_heads // heads_per_step, steps),
        in_specs=[fwd] * 5 + [fwd_p] + [bwd] * 5 + [bwd_p],
        out_specs=[fwd, bwd],
        out_shape=[jax.ShapeDtypeStruct((bsz, length, d), F32)] * 2,
        scratch_shapes=[pltpu.VMEM((RWKV_HEAD_DIM, hw), F32)] * 2,
        compiler_params=_params(("parallel", "parallel", "arbitrary")),
        name="rwkv_scan",
    )(*ops_f, v, pend_f, *ops_b, v, pend_b)
    row_inputs = [(x.reshape(t, d), d) for x in (y_f, y_b, bv, g)]
    return _rwkv_out, row_inputs, [_row(ln_w), _row(ln_b), seg, segt], tail_tm


def kernel(x_prompt, x_sample, mix_pre_norm, mix_post_norm, ffn_pre_norm, ffn_post_norm, mla_w_in, mla_q_norm, mla_w_qb, mla_kv_norm, mla_w_kvb, mla_w_o, ssm_w_in, ssm_conv_w, ssm_conv_b, ssm_a_log, ssm_dt_bias, ssm_d_skip, ssm_gate_norm, ssm_w_out, rwkv_mu, rwkv_w_rkv, rwkv_w0, rwkv_w1, rwkv_w2, rwkv_a0, rwkv_a1, rwkv_a2, rwkv_g1, rwkv_g2, rwkv_k_k, rwkv_k_a, rwkv_r_k, rwkv_ln_w, rwkv_ln_b, rwkv_w_o, ffn_w_in, ffn_w_out):
    depth = mix_pre_norm.shape[0]

    def trunk(x):
        bsz, length, d = x.shape
        h = x.reshape(bsz * length, d)
        for i in range(depth):
            kind, j = i % N_MIXERS, i // N_MIXERS
            h3 = h.reshape(bsz, length, d)
            if kind == 0:
                mixer = _mla_mixer(h3, mix_pre_norm[i], mla_w_in[j], mla_q_norm[j], mla_w_qb[j],
                                   mla_kv_norm[j], mla_w_kvb[j])
                w_o = mla_w_o[j]
            elif kind == 1:
                mixer = _ssd_mixer(h3, mix_pre_norm[i], ssm_w_in[j], ssm_conv_w[j], ssm_conv_b[j],
                                   ssm_a_log[j], ssm_dt_bias[j], ssm_d_skip[j], ssm_gate_norm[j])
                w_o = ssm_w_out[j]
            else:
                mixer = _rwkv_mixer(h3, mix_pre_norm[i], rwkv_mu[j], rwkv_w_rkv[j], rwkv_w0[j],
                                    rwkv_w1[j], rwkv_w2[j], rwkv_a0[j], rwkv_a1[j], rwkv_a2[j],
                                    rwkv_g1[j], rwkv_g2[j], rwkv_k_k[j], rwkv_k_a[j], rwkv_r_k[j],
                                    rwkv_ln_w[j], rwkv_ln_b[j])
                w_o = rwkv_w_o[j]
            h = _layer_tail(h, mixer, w_o, mix_post_norm[i], ffn_pre_norm[i], ffn_w_in[i],
                            ffn_w_out[i], ffn_post_norm[i])
        return h.reshape(bsz, length, d)

    return (trunk(x_prompt), trunk(x_sample))
```
